```python
import jax, jax.numpy as jnp
from jax import lax
import numpy as np

D_MODEL = 1024
BATCH = 16
SEQ = 4096
DEPTH = 2
DEC_BATCH = 16
DEC_SEQ = 32
PAST_LEN = 2048

CHUNK = 64
Q_BLOCK = 128
N_MEM = 256
MEM_HEADS = 4
MEM_HEAD = D_MODEL // MEM_HEADS
MEM_SCALE = MEM_HEAD ** -0.5
RG_WIDTH = D_MODEL // 4
RG_BLOCKS = 4
RG_BLOCK = RG_WIDTH // RG_BLOCKS
CONV_W = 4
LRU_C = 8.0
RW_HEADS = 4
RW_HEAD = 64
RW_WIDTH = RW_HEADS * RW_HEAD
RW_DECAY_LORA = 32
RW_A_LORA = 32
RW_G_LORA = 64
RW_IN = 3 * RW_WIDTH + RW_DECAY_LORA + RW_A_LORA + RW_G_LORA
RW_SPLITS = [RW_WIDTH, 2 * RW_WIDTH, 3 * RW_WIDTH, 3 * RW_WIDTH + RW_DECAY_LORA,
             3 * RW_WIDTH + RW_DECAY_LORA + RW_A_LORA]
GN_EPS = 64e-5
L2_EPS = 1e-12
MLA_HEADS = 8
QK_NOPE = 64
QK_ROPE = 32
V_HEAD = 64
Q_LORA = 256
KV_LORA = 128
MLA_IN = Q_LORA + KV_LORA + QK_ROPE
MLA_WIDTH = MLA_HEADS * V_HEAD
MLA_SCALE = (QK_NOPE + QK_ROPE) ** -0.5
ROPE_BASE = 10000.0
MIX_IN = 2 * RG_WIDTH + RW_IN + MLA_IN
MIX_SPLITS = [RG_WIDTH, 2 * RG_WIDTH, 2 * RG_WIDTH + RW_IN]
MIX_WIDTH = RG_WIDTH + RW_WIDTH + MLA_WIDTH
D_FF = 4 * D_MODEL
EPS = 1e-6

kernel_name = 'hybrid_streaming_encoder_step'


def rmsnorm(x, g):
    xf = x.astype(jnp.float32)
    y = xf * lax.rsqrt(jnp.mean(xf * xf, axis=-1, keepdims=True) + EPS)
    return (y * g).astype(x.dtype)


def apply_rope(x, pos):
    half = x.shape[-1] // 2
    inv = ROPE_BASE ** (-jnp.arange(half, dtype=jnp.float32) / half)
    ang = pos.astype(jnp.float32)[:, None] * inv
    shp = (pos.shape[0],) + (1,) * (x.ndim - 3) + (half,)
    cos, sin = jnp.cos(ang).reshape(shp), jnp.sin(ang).reshape(shp)
    xf = x.astype(jnp.float32)
    x1, x2 = xf[..., :half], xf[..., half:]
    return jnp.concatenate([x1 * cos - x2 * sin, x2 * cos + x1 * sin], axis=-1).astype(x.dtype)


def _lin_combine(left, right):
    a_l, b_l = left
    a_r, b_r = right
    return a_l * a_r, a_r * b_l + b_r


def rglru_mixer(xr, gate, conv_buf, h0, P):
    B, S, W = xr.shape
    xp = jnp.concatenate([conv_buf.astype(xr.dtype), xr], axis=1)
    conv = P['rg_conv_b'] + sum(xp[:, j:j + S] * P['rg_conv_w'][j] for j in range(CONV_W))
    xc = conv.astype(jnp.float32)
    xb = xc.reshape(B, S, RG_BLOCKS, RG_BLOCK)
    r = jax.nn.sigmoid(jnp.einsum('bsgi,gij->bsgj', xb, P['rg_wa']).reshape(B, S, W) + P['rg_ba'])
    i = jax.nn.sigmoid(jnp.einsum('bsgi,gij->bsgj', xb, P['rg_wx']).reshape(B, S, W) + P['rg_bx'])
    log_a = -LRU_C * r * jax.nn.softplus(-P['rg_lambda'].astype(jnp.float32))
    a = jnp.exp(log_a)
    b = jnp.sqrt(-jnp.expm1(2.0 * log_a)) * (i * xc)
    a_cum, b_cum = lax.associative_scan(_lin_combine, (a, b), axis=1)
    h = a_cum * h0.astype(jnp.float32)[:, None, :] + b_cum
    y = jax.nn.gelu(gate.astype(jnp.float32)) * h
    return y.astype(xr.dtype), xp[:, -(CONV_W - 1):], h[:, -1].astype(xr.dtype)


def rwkv7_mixer(p, shift_prev, state0, P):
    B, S, _ = p.shape
    prev = jnp.concatenate([shift_prev[:, None, :].astype(p.dtype), p[:, :-1]], axis=1)
    xs = (p + P['rw_mu'] * (prev - p)).astype(jnp.float32)
    r, k, v, wd, ad, gd = jnp.split(xs, RW_SPLITS, axis=-1)
    log_w = -jax.nn.softplus(-(P['rw_w0'] + jnp.tanh(wd) @ P['rw_w2'])) - 0.5
    w = jnp.exp(-jnp.exp(log_w))
    a = jax.nn.sigmoid(P['rw_a0'] + ad @ P['rw_a2'])
    g = jax.nn.sigmoid(gd) @ P['rw_g2']
    heads = lambda t: t.reshape(B, S, RW_HEADS, RW_HEAD)
    kk = heads(k * P['rw_k_k'])
    kk = kk * lax.rsqrt(jnp.sum(kk * kk, axis=-1, keepdims=True) + L2_EPS)
    k = k * (1.0 + (a - 1.0) * P['rw_k_a'])
    r_h, w_h, k_h, v_h, a_h = heads(r), heads(w), heads(k), heads(v), heads(a)

    def step(Sm, inp):
        r_t, w_t, k_t, v_t, kk_t, a_t = inp
        sa = jnp.einsum('bhij,bhj->bhi', Sm, kk_t)
        Sm = (Sm * w_t[:, :, None, :] - sa[..., None] * (kk_t * a_t)[:, :, None, :]
              + v_t[..., None] * k_t[:, :, None, :])
        return Sm, jnp.einsum('bhij,bhj->bhi', Sm, r_t)

    seq_first = lambda t: jnp.moveaxis(t, 1, 0)
    s_fin, y = lax.scan(step, state0.astype(jnp.float32),
                        (seq_first(r_h), seq_first(w_h), seq_first(k_h), seq_first(v_h),
                         seq_first(kk), seq_first(a_h)))
    y = jnp.moveaxis(y, 0, 1)
    mu = jnp.mean(y, axis=-1, keepdims=True)
    var = jnp.mean(jnp.square(y - mu), axis=-1, keepdims=True)
    y = ((y - mu) * lax.rsqrt(var + GN_EPS)).reshape(B, S, RW_WIDTH) * P['rw_gn_w'] + P['rw_gn_b']
    bonus = jnp.sum(r_h * k_h * P['rw_r_k'], axis=-1, keepdims=True) * v_h
    y = (y + bonus.reshape(B, S, RW_WIDTH)) * g
    return y.astype(p.dtype), p[:, -1], s_fin.astype(p.dtype)


def mla_project(p, pos, P):
    cq, ckv, kr = jnp.split(p, [Q_LORA, Q_LORA + KV_LORA], axis=-1)
    q = jnp.einsum('bsc,chd->bshd', rmsnorm(cq, P['mla_q_norm']), P['mla_w_uq'])
    q_rope = apply_rope(q[..., QK_NOPE:], pos)
    q_lat = jnp.einsum('bshd,chd->bshc', q[..., :QK_NOPE], P['mla_w_uk'])
    return q_lat, q_rope, rmsnorm(ckv, P['mla_kv_norm']), apply_rope(kr, pos)


def mla_attend_blockcausal(q_lat, q_rope, ckv, kr):
    B, S, H, C = q_lat.shape
    nb = S // Q_BLOCK
    qb = q_lat.reshape(B, nb, Q_BLOCK, H, C).transpose(1, 0, 2, 3, 4)
    rb = q_rope.reshape(B, nb, Q_BLOCK, H, QK_ROPE).transpose(1, 0, 2, 3, 4)
    key_chunk = jnp.arange(S) // CHUNK

    def one(args):
        blk, qlb, qrb = args
        q_chunk = (blk * Q_BLOCK + jnp.arange(Q_BLOCK)) // CHUNK
        s = (jnp.einsum('bqhc,bkc->bhqk', qlb, ckv)
             + jnp.einsum('bqhr,bkr->bhqk', qrb, kr)).astype(jnp.float32) * MLA_SCALE
        s = jnp.where(key_chunk[None, :] <= q_chunk[:, None], s, -jnp.inf)
        pr = jax.nn.softmax(s, axis=-1).astype(ckv.dtype)
        return jnp.einsum('bhqk,bkc->bqhc', pr, ckv)

    o = lax.map(one, (jnp.arange(nb), qb, rb))
    return o.transpose(1, 0, 2, 3, 4).reshape(B, S, H, C)


def mla_attend_full(q_lat, q_rope, ckv, kr):
    s = (jnp.einsum('bqhc,bkc->bhqk', q_lat, ckv)
         + jnp.einsum('bqhr,bkr->bhqk', q_rope, kr)).astype(jnp.float32) * MLA_SCALE
    pr = jax.nn.softmax(s, axis=-1).astype(ckv.dtype)
    return jnp.einsum('bhqk,bkc->bqhc', pr, ckv)


def token_mix(x, pos, P, rg_buf, rg_h, rw_shift, rw_state, past_ckv, past_kr):
    B, S, _ = x.shape
    p = rmsnorm(x, P['norm_mix']) @ P['w_in']
    p_rg, p_gate, p_rw, p_mla = jnp.split(p, MIX_SPLITS, axis=-1)
    y_rg, rg_buf_new, rg_h_new = rglru_mixer(p_rg, p_gate, rg_buf, rg_h, P)
    y_rw, rw_shift_new, rw_state_new = rwkv7_mixer(p_rw, rw_shift, rw_state, P)
    q_lat, q_rope, ckv, kr = mla_project(p_mla, pos, P)
    if past_ckv is None:
        o_lat = mla_attend_blockcausal(q_lat, q_rope, ckv, kr)
    else:
        o_lat = mla_attend_full(q_lat, q_rope,
                                jnp.concatenate([past_ckv.astype(ckv.dtype), ckv], axis=1),
                                jnp.concatenate([past_kr.astype(kr.dtype), kr], axis=1))
    y_mla = jnp.einsum('bshc,chv->bshv', o_lat, P['mla_w_uv']).reshape(B, S, MLA_WIDTH)
    y = jnp.concatenate([y_rg, y_rw, y_mla], axis=-1) @ P['w_out']
    return x + y, (rg_buf_new, rg_h_new, rw_shift_new, rw_state_new, ckv, kr)


def mem_project(mem, g, wk, wv):
    m = rmsnorm(mem, g)
    return jnp.einsum('bmd,dhe->bmhe', m, wk), jnp.einsum('bmd,dhe->bmhe', m, wv)


def mem_attend(h, mk, mv, wq, wo):
    q = jnp.einsum('bsd,dhe->bshe', h, wq)
    s = jnp.einsum('bshe,bmhe->bhsm', q, mk.astype(q.dtype)).astype(jnp.float32) * MEM_SCALE
    pr = jax.nn.softmax(s, axis=-1).astype(q.dtype)
    o = jnp.einsum('bhsm,bmhe->bshe', pr, mv.astype(q.dtype))
    return jnp.einsum('bshe,hed->bsd', o, wo)


def layer(x, pos, P, rg_buf, rg_h, rw_shift, rw_state, past_ckv, past_kr, mem_k, mem_v):
    x, new_state = token_mix(x, pos, P, rg_buf, rg_h, rw_shift, rw_state, past_ckv, past_kr)
    x = x + mem_attend(rmsnorm(x, P['norm_mem']), mem_k, mem_v, P['mem_w_q'], P['mem_w_o'])
    h = rmsnorm(x, P['norm_ffn'])
    x = x + jnp.square(jax.nn.relu(h @ P['ffn_w1'])) @ P['ffn_w2']
    return x, new_state


def setup_inputs(seed: int = 0) -> dict:
    key = jax.random.key(seed)
    keys = iter(jax.random.split(key, 64))
    f32 = jnp.float32
    L = DEPTH

    def nrm(shape, scale):
        return scale * jax.random.normal(next(keys), shape, f32)

    def gain(shape):
        return 1.0 + 0.05 * jax.random.normal(next(keys), shape, f32)

    def unif(shape, lo, hi):
        return jax.random.uniform(next(keys), shape, f32, lo, hi)

    return {
        'x_prompt': nrm((BATCH, SEQ, D_MODEL), 1.0),
        'x_sample': nrm((DEC_BATCH, DEC_SEQ, D_MODEL), 1.0),
        'state_rg_conv': nrm((L, DEC_BATCH, CONV_W - 1, RG_WIDTH), 1.0),
        'state_rg_h': nrm((L, DEC_BATCH, RG_WIDTH), 0.5),
        'state_rw_shift': nrm((L, DEC_BATCH, RW_IN), 1.0),
        'state_rw_wkv': nrm((L, DEC_BATCH, RW_HEADS, RW_HEAD, RW_HEAD), 0.1),
        'cache_mla_ckv': nrm((L, DEC_BATCH, PAST_LEN, KV_LORA), 1.0),
        'cache_mla_krope': nrm((L, DEC_BATCH, PAST_LEN, QK_ROPE), 1.0),
        'cache_mem_k': nrm((L, DEC_BATCH, N_MEM, MEM_HEADS, MEM_HEAD), 1.0),
        'cache_mem_v': nrm((L, DEC_BATCH, N_MEM, MEM_HEADS, MEM_HEAD), 1.0),
        'mem_prompt': nrm((BATCH, N_MEM, D_MODEL), 1.0),
        'norm_mix': gain((L, D_MODEL)),
        'w_in': nrm((L, D_MODEL, MIX_IN), D_MODEL ** -0.5),
        'rg_conv_w': nrm((L, CONV_W, RG_WIDTH), CONV_W ** -0.5),
        'rg_conv_b': nrm((L, RG_WIDTH), 0.01),
        'rg_wa': nrm((L, RG_BLOCKS, RG_BLOCK, RG_BLOCK), RG_BLOCK ** -0.5),
        'rg_ba': nrm((L, RG_WIDTH), 0.01),
        'rg_wx': nrm((L, RG_BLOCKS, RG_BLOCK, RG_BLOCK), RG_BLOCK ** -0.5),
        'rg_bx': nrm((L, RG_WIDTH), 0.01),
        'rg_lambda': unif((L, RG_WIDTH), 2.0, 6.0),
        'rw_mu': unif((L, RW_IN), 0.0, 1.0),
        'rw_w0': nrm((L, RW_WIDTH), 0.5),
        'rw_w2': nrm((L, RW_DECAY_LORA, RW_WIDTH), 0.1),
        'rw_a0': nrm((L, RW_WIDTH), 0.5),
        'rw_a2': nrm((L, RW_A_LORA, RW_WIDTH), 0.1),
        'rw_g2': nrm((L, RW_G_LORA, RW_WIDTH), RW_G_LORA ** -0.5),
        'rw_k_k': 0.85 + nrm((L, RW_WIDTH), 0.05),
        'rw_k_a': gain((L, RW_WIDTH)),
        'rw_r_k': nrm((L, RW_HEADS, RW_HEAD), 0.1),
        'rw_gn_w': gain((L, RW_WIDTH)),
        'rw_gn_b': nrm((L, RW_WIDTH), 0.01),
        'mla_q_norm': gain((L, Q_LORA)),
        'mla_kv_norm': gain((L, KV_LORA)),
        'mla_w_uq': nrm((L, Q_LORA, MLA_HEADS, QK_NOPE + QK_ROPE), Q_LORA ** -0.5),
        'mla_w_uk': nrm((L, KV_LORA, MLA_HEADS, QK_NOPE), KV_LORA ** -0.5),
        'mla_w_uv': nrm((L, KV_LORA, MLA_HEADS, V_HEAD), KV_LORA ** -0.5),
        'w_out': nrm((L, MIX_WIDTH, D_MODEL), MIX_WIDTH ** -0.5),
        'norm_mem': gain((L, D_MODEL)),
        'norm_mem_kv': gain((L, D_MODEL)),
        'mem_w_q': nrm((L, D_MODEL, MEM_HEADS, MEM_HEAD), D_MODEL ** -0.5),
        'mem_w_k': nrm((L, D_MODEL, MEM_HEADS, MEM_HEAD), D_MODEL ** -0.5),
        'mem_w_v': nrm((L, D_MODEL, MEM_HEADS, MEM_HEAD), D_MODEL ** -0.5),
        'mem_w_o': nrm((L, MEM_HEADS, MEM_HEAD, D_MODEL), D_MODEL ** -0.5),
        'norm_ffn': gain((L, D_MODEL)),
        'ffn_w1': nrm((L, D_MODEL, D_FF), D_MODEL ** -0.5),
        'ffn_w2': nrm((L, D_FF, D_MODEL), D_FF ** -0.5),
        'norm_final': gain((D_MODEL,)),
    }


def reference(x_prompt, x_sample, state_rg_conv, state_rg_h, state_rw_shift, state_rw_wkv,
              cache_mla_ckv, cache_mla_krope, cache_mem_k, cache_mem_v, mem_prompt,
              norm_mix, w_in, rg_conv_w, rg_conv_b, rg_wa, rg_ba, rg_wx, rg_bx, rg_lambda,
              rw_mu, rw_w0, rw_w2, rw_a0, rw_a2, rw_g2, rw_k_k, rw_k_a, rw_r_k, rw_gn_w, rw_gn_b,
              mla_q_norm, mla_kv_norm, mla_w_uq, mla_w_uk, mla_w_uv, w_out,
              norm_mem, norm_mem_kv, mem_w_q, mem_w_k, mem_w_v, mem_w_o,
              norm_ffn, ffn_w1, ffn_w2, norm_final):
    Bp, S_p, _ = x_prompt.shape
    S_s = x_sample.shape[1]
    past = cache_mla_ckv.shape[2]
    dt = x_prompt.dtype
    pos_p = jnp.arange(S_p, dtype=jnp.int32)
    pos_s = past + jnp.arange(S_s, dtype=jnp.int32)
    z_conv = jnp.zeros((Bp, CONV_W - 1, RG_WIDTH), dt)
    z_h = jnp.zeros((Bp, RG_WIDTH), dt)
    z_shift = jnp.zeros((Bp, RW_IN), dt)
    z_wkv = jnp.zeros((Bp, RW_HEADS, RW_HEAD, RW_HEAD), dt)

    xp, xs = x_prompt, x_sample
    p_states, s_states = [], []
    for l in range(DEPTH):
        P = {
            'norm_mix': norm_mix[l], 'w_in': w_in[l], 'w_out': w_out[l],
            'rg_conv_w': rg_conv_w[l], 'rg_conv_b': rg_conv_b[l], 'rg_wa': rg_wa[l],
            'rg_ba': rg_ba[l], 'rg_wx': rg_wx[l], 'rg_bx': rg_bx[l], 'rg_lambda': rg_lambda[l],
            'rw_mu': rw_mu[l], 'rw_w0': rw_w0[l], 'rw_w2': rw_w2[l], 'rw_a0': rw_a0[l],
            'rw_a2': rw_a2[l], 'rw_g2': rw_g2[l], 'rw_k_k': rw_k_k[l], 'rw_k_a': rw_k_a[l],
            'rw_r_k': rw_r_k[l], 'rw_gn_w': rw_gn_w[l], 'rw_gn_b': rw_gn_b[l],
            'mla_q_norm': mla_q_norm[l], 'mla_kv_norm': mla_kv_norm[l],
            'mla_w_uq': mla_w_uq[l], 'mla_w_uk': mla_w_uk[l], 'mla_w_uv': mla_w_uv[l],
            'norm_mem': norm_mem[l], 'mem_w_q': mem_w_q[l], 'mem_w_o': mem_w_o[l],
            'norm_ffn': norm_ffn[l], 'ffn_w1': ffn_w1[l], 'ffn_w2': ffn_w2[l],
        }
        mem_k_l, mem_v_l = mem_project(mem_prompt, norm_mem_kv[l], mem_w_k[l], mem_w_v[l])
        xp, st_p = layer(xp, pos_p, P, z_conv, z_h, z_shift, z_wkv, None, None, mem_k_l, mem_v_l)
        xs, st_s = layer(xs, pos_s, P, state_rg_conv[l], state_rg_h[l], state_rw_shift[l],
                         state_rw_wkv[l], cache_mla_ckv[l], cache_mla_krope[l],
                         cache_mem_k[l], cache_mem_v[l])
        p_states.append(st_p + (mem_k_l, mem_v_l))
        s_states.append(st_s)

    y_prompt = rmsnorm(xp, norm_final)
    y_sample = rmsnorm(xs, norm_final)
    sp = [jnp.stack(t) for t in zip(*p_states)]
    ss = [jnp.stack(t) for t in zip(*s_states)]
    return (y_prompt, y_sample,
            sp[0], sp[1], sp[2], sp[3], sp[4], sp[5], sp[6], sp[7],
            ss[0], ss[1], ss[2], ss[3], ss[4], ss[5])
```

```python
import functools
import math

import jax
import jax.numpy as jnp
from jax import lax
from jax.experimental import pallas as pl
from jax.experimental.pallas import tpu as pltpu

BF = jnp.bfloat16
F32 = jnp.float32

D_MODEL = 1024
RG_WIDTH = 256
RG_BLOCKS = 4
CONV_W = 4
LRU_C = 8.0
RW_HEADS = 4
RW_HEAD = 64
RW_WIDTH = 256
RW_IN = 896
GN_EPS = 64e-5
L2_EPS = 1e-12
MLA_HEADS = 8
QK_NOPE = 64
QK_ROPE = 32
V_HEAD = 64
Q_LORA = 256
KV_LORA = 128
MLA_SCALE = (QK_NOPE + QK_ROPE) ** -0.5
ROPE_BASE = 10000.0
CHUNK = 64
N_MEM = 256
MEM_HEADS = 4
MEM_HEAD = 256
MEM_SCALE = MEM_HEAD ** -0.5
D_FF = 4096
EPS = 1e-6

LANES = 128
VMEM_LIMIT = 52 * 1024 * 1024

ROW_TILE = 512
RG_TILE = 512
RW_TILE = 256
RW_CHUNK = 64
ATT_TILE = 256
FF_TILE = 1024


def _cparams(*sem):
    return pltpu.CompilerParams(dimension_semantics=sem, vmem_limit_bytes=VMEM_LIMIT)


def _rms(x, g):
    return x * lax.rsqrt(jnp.mean(x * x, axis=-1, keepdims=True) + EPS) * g


def _dot(a, b):
    return jnp.dot(a.astype(BF), b.astype(BF), preferred_element_type=F32)


def _dot_nt(a, b):
    return lax.dot_general(a.astype(BF), b.astype(BF), (((1,), (1,)), ((), ())),
                           preferred_element_type=F32)


def _split(x):
    hi = x.astype(BF)
    lo = (x - hi.astype(F32)).astype(BF)
    return hi, lo


def _mm3(a, b, dims=(((1,), (0,)), ((), ()))):
    ah, al = _split(a)
    bh, bl = _split(b)
    dg = functools.partial(lax.dot_general, dimension_numbers=dims, preferred_element_type=F32)
    return dg(ah, bh) + (dg(ah, bl) + dg(al, bh))


_NT = (((1,), (1,)), ((), ()))
_TN = (((0,), (0,)), ((), ()))


def _softplus(x):
    return jnp.maximum(x, 0.0) + jnp.log1p(jnp.exp(-jnp.abs(x)))


def _sigmoid(x):
    return 1.0 / (1.0 + jnp.exp(-x))


def _gelu_tanh(x):
    c = math.sqrt(2.0 / math.pi)
    return 0.5 * x * (1.0 + jnp.tanh(c * (x + 0.044715 * (x * x * x))))


def _shift_rows(x, d):
    return pltpu.roll(x, d, 0)


def _in_kernel(x_ref, g_ref, w_ref, gq_ref, gkv_ref, wq_ref, wuk_ref, cos_ref, sin_ref,
               pa_ref, prw_ref, ckv_ref, kr_ref, kcat_ref, qcat_ref):
    xn = _rms(x_ref[...], g_ref[...]).astype(BF)
    p = jnp.dot(xn, w_ref[...], preferred_element_type=F32)
    pa_ref[...] = p[:, :512]
    prw_ref[...] = p[:, 512:1408]
    cq = p[:, 1408:1664]
    ckv = p[:, 1664:1792]
    ka = p[:, 1792:1920]
    kb = p[:, 1920:2048]
    cos = cos_ref[...]
    sin = sin_ref[...]
    ckvn = _rms(ckv, gkv_ref[...])
    krt = ka * cos[:, :LANES] + kb * sin[:, :LANES]
    ckv_ref[...] = ckvn
    kr_ref[...] = krt[:, :QK_ROPE]
    kcat_ref[...] = jnp.concatenate([ckvn, krt], axis=-1).astype(BF)

    cqn = _rms(cq, gq_ref[...]).astype(BF)
    qq = jnp.dot(cqn, wq_ref[...], preferred_element_type=F32)
    rr = qq[:, 512:768] * cos + qq[:, 768:1024] * sin
    qlat = jnp.dot(qq[:, :512].astype(BF), wuk_ref[...], preferred_element_type=F32)
    lane_grp = lax.broadcasted_iota(jnp.int32, (1, LANES), 1) // QK_ROPE
    for h in range(MLA_HEADS):
        g = h // 4
        rpart = jnp.where(lane_grp == (h % 4), rr[:, g * LANES:(g + 1) * LANES], 0.0)
        qcat_ref[0, h, :, :LANES] = qlat[:, h * LANES:(h + 1) * LANES].astype(BF)
        qcat_ref[0, h, :, LANES:] = rpart.astype(BF)


def _in_proj(x2d, B, S, g, w_ext, gq, gkv, wq_ext, wuk_bd, cosq, sinq):
    T = B * S
    tm = min(ROW_TILE, S)
    nst = S // tm
    ntab = cosq.shape[0] // tm
    row = lambda i: (i, 0)
    const = lambda i: (0, 0)
    outs = pl.pallas_call(
        _in_kernel,
        grid=(T // tm,),
        in_specs=[
            pl.BlockSpec((tm, D_MODEL), row),
            pl.BlockSpec((1, D_MODEL), const),
            pl.BlockSpec((D_MODEL, 2048), const),
            pl.BlockSpec((1, Q_LORA), const),
            pl.BlockSpec((1, KV_LORA), const),
            pl.BlockSpec((Q_LORA, 1024), const),
            pl.BlockSpec((512, 1024), const),
            pl.BlockSpec((tm, 256), lambda i: (i % ntab, 0)),
            pl.BlockSpec((tm, 256), lambda i: (i % ntab, 0)),
        ],
        out_specs=[
            pl.BlockSpec((tm, 512), row),
            pl.BlockSpec((tm, RW_IN), row),
            pl.BlockSpec((tm, KV_LORA), row),
            pl.BlockSpec((tm, QK_ROPE), row),
            pl.BlockSpec((tm, 256), row),
            pl.BlockSpec((1, MLA_HEADS, tm, 256), lambda i: (i // nst, 0, i % nst, 0)),
        ],
        out_shape=[
            jax.ShapeDtypeStruct((T, 512), F32),
            jax.ShapeDtypeStruct((T, RW_IN), F32),
            jax.ShapeDtypeStruct((T, KV_LORA), F32),
            jax.ShapeDtypeStruct((T, QK_ROPE), F32),
            jax.ShapeDtypeStruct((T, 256), BF),
            jax.ShapeDtypeStruct((B, MLA_HEADS, S, 256), BF),
        ],
        compiler_params=_cparams("parallel"),
        name="in_proj",
    )(x2d, g, w_ext, gq, gkv, wq_ext, wuk_bd, cosq, sinq)
    return outs


def _rglru_kernel(pa_ref, buf0_ref, h0_ref, cw_ref, cb_ref, wa_ref, ba_ref, wx_ref, bx_ref,
                  lam_ref, y_ref, buf_ref, hout_ref, ext_ref, h_ref, *, R):
    t = pl.program_id(1)

    @pl.when(t == 0)
    def _():
        ext_ref[0:8, :] = buf0_ref[0]
        h_ref[...] = h0_ref[0]

    pa = pa_ref[0]
    xr = pa[:, :RG_WIDTH]
    gate = pa[:, RG_WIDTH:]
    ext_ref[8:8 + R, :] = xr
    cw = cw_ref[...]
    conv = (cb_ref[...] + cw[3:4] * xr + cw[2:3] * ext_ref[7:7 + R, :]
            + cw[1:2] * ext_ref[6:6 + R, :] + cw[0:1] * ext_ref[5:5 + R, :])
    hist = ext_ref[R:R + 8, :]
    ext_ref[0:8, :] = hist
    buf_ref[0] = hist

    ra = _sigmoid(_dot(conv, wa_ref[...]) + ba_ref[...])
    ia = _sigmoid(_dot(conv, wx_ref[...]) + bx_ref[...])
    log_a = (-LRU_C) * ra * _softplus(-lam_ref[...])
    a = jnp.exp(log_a)
    b = jnp.sqrt(-jnp.tanh(log_a) * (a * a + 1.0)) * (ia * conv)

    rows = lax.broadcasted_iota(jnp.int32, (R, RG_WIDTH), 0)
    d = 1
    while d < R:
        m = rows >= d
        b = jnp.where(m, a * _shift_rows(b, d), 0.0) + b
        a = jnp.where(m, a * _shift_rows(a, d), a)
        d *= 2
    h = a * h_ref[...] + b
    hl = h[R - 1:R, :]
    h_ref[...] = hl
    hout_ref[0] = hl
    y_ref[0] = (_gelu_tanh(gate) * h).astype(BF)


def _rglru(pa, buf8, h0, cw, cb, wa_bd, ba, wx_bd, bx, lam):
    B, S, _ = pa.shape
    R = min(RG_TILE, S)
    vec = lambda b, t: (0, 0)
    return pl.pallas_call(
        functools.partial(_rglru_kernel, R=R),
        grid=(B, S // R),
        in_specs=[
            pl.BlockSpec((1, R, 512), lambda b, t: (b, t, 0)),
            pl.BlockSpec((1, 8, RG_WIDTH), lambda b, t: (b, 0, 0)),
            pl.BlockSpec((1, 1, RG_WIDTH), lambda b, t: (b, 0, 0)),
            pl.BlockSpec((CONV_W, RG_WIDTH), vec),
            pl.BlockSpec((1, RG_WIDTH), vec),
            pl.BlockSpec((RG_WIDTH, RG_WIDTH), vec),
            pl.BlockSpec((1, RG_WIDTH), vec),
            pl.BlockSpec((RG_WIDTH, RG_WIDTH), vec),
            pl.BlockSpec((1, RG_WIDTH), vec),
            pl.BlockSpec((1, RG_WIDTH), vec),
        ],
        out_specs=[
            pl.BlockSpec((1, R, RG_WIDTH), lambda b, t: (b, t, 0)),
            pl.BlockSpec((1, 8, RG_WIDTH), lambda b, t: (b, 0, 0)),
            pl.BlockSpec((1, 1, RG_WIDTH), lambda b, t: (b, 0, 0)),
        ],
        out_shape=[
            jax.ShapeDtypeStruct((B, S, RG_WIDTH), BF),
            jax.ShapeDtypeStruct((B, 8, RG_WIDTH), F32),
            jax.ShapeDtypeStruct((B, 1, RG_WIDTH), F32),
        ],
        scratch_shapes=[pltpu.VMEM((R + 8, RG_WIDTH), F32), pltpu.VMEM((1, RG_WIDTH), F32)],
        compiler_params=_cparams("parallel", "arbitrary"),
        name="rglru",
    )(pa, buf8, h0, cw, cb, wa_bd, ba, wx_bd, bx, lam)


def _rwkv_kernel(p_ref, shift0_ref, state0_ref, mu_ref, w0_ref, w2_ref, a0_ref, a2_ref, g2_ref,
                 kk_ref, ka_ref, rk_ref, gnw_ref, gnb_ref, ones_ref,
                 y_ref, shift_ref, state_ref, prev_ref, s_ref, *, R, C):
    t = pl.program_id(1)
    W = RW_WIDTH
    HC = RW_HEADS * C

    @pl.when(t == 0)
    def _():
        prev_ref[...] = shift0_ref[0]
        s_ref[...] = state0_ref[0]

    p = p_ref[0]
    rows = lax.broadcasted_iota(jnp.int32, (R, RW_IN), 0)
    prev = jnp.where(rows == 0, prev_ref[...], _shift_rows(p, 1))
    last = p[R - 1:R, :]
    prev_ref[...] = last
    shift_ref[0] = last
    xs = p + mu_ref[...] * (prev - p)
    r = xs[:, 0:W]
    k = xs[:, W:2 * W]
    v = xs[:, 2 * W:3 * W]
    x4 = xs[:, 3 * W:]
    ones_bd = ones_ref[...]

    def segsum(x):
        hi = x.astype(BF)
        r1 = x - hi.astype(F32)
        mid = r1.astype(BF)
        lo = (r1 - mid.astype(F32)).astype(BF)
        dg = functools.partial(jnp.dot, preferred_element_type=F32)
        return dg(hi, ones_bd) + (dg(mid, ones_bd) + dg(lo, ones_bd))

    log_w = -_softplus(-(w0_ref[...] + _mm3(jnp.tanh(x4), w2_ref[...]))) - 0.5
    logw = -jnp.exp(log_w)
    a = _sigmoid(a0_ref[...] + _mm3(x4, a2_ref[...]))
    g = _mm3(_sigmoid(x4), g2_ref[...])
    kk = k * kk_ref[...]
    kk = kk * lax.rsqrt(segsum(kk * kk) + L2_EPS)
    k = k * (1.0 + (a - 1.0) * ka_ref[...])
    bonus = segsum(r * k * rk_ref[...]) * v
    kka = kk * a

    lane_head = lax.broadcasted_iota(jnp.int32, (1, W), 1) // RW_HEAD
    hmask = [lane_head == h for h in range(RW_HEADS)]

    def stack(x):
        return jnp.concatenate([jnp.where(hmask[h], x, 0.0) for h in range(RW_HEADS)], axis=0)

    ri = lax.broadcasted_iota(jnp.int32, (HC, HC), 0) % C
    ci = lax.broadcasted_iota(jnp.int32, (HC, HC), 1) % C
    strict = ri > ci
    incl = ri >= ci
    eye_hc = lax.broadcasted_iota(jnp.int32, (HC, HC), 0) == lax.broadcasted_iota(jnp.int32, (HC, HC), 1)
    eye_w = lax.broadcasted_iota(jnp.int32, (W, W), 0) == lax.broadcasted_iota(jnp.int32, (W, W), 1)
    crow = lax.broadcasted_iota(jnp.int32, (C, W), 0)

    ys = []
    for c in range(R // C):
        sl = slice(c * C, (c + 1) * C)
        lw = logw[sl]
        cum = lw
        d = 1
        while d < C:
            cum = cum + jnp.where(crow >= d, _shift_rows(cum, d), 0.0)
            d *= 2
        cl = cum[C - 1:C, :]
        w_inc = jnp.exp(cum)
        w_inv = jnp.exp(-cum)
        w_prev = jnp.exp(cum - lw)
        w_rem = jnp.exp(cl - cum)
        w_end = jnp.exp(cl)
        a_s = stack(w_prev * kk[sl])
        b_s = stack(kka[sl] * w_inv)
        k_s = stack(k[sl] * w_inv)
        r_s = stack(r[sl] * w_inc)
        v_s = stack(v[sl])
        kt_s = stack(k[sl] * w_rem)
        bt_s = stack(kka[sl] * w_rem)

        bk = jnp.concatenate([b_s, k_s], axis=0)
        g_a = _mm3(a_s, bk, _NT)
        g_r = _mm3(r_s, bk, _NT)
        n = jnp.where(strict, -g_a[:, :HC], 0.0)
        l_k = jnp.where(strict, g_a[:, HC:], 0.0)
        m_b = jnp.where(incl, g_r[:, :HC], 0.0)
        m_k = jnp.where(incl, g_r[:, HC:], 0.0)

        tinv = jnp.where(eye_hc, 1.0, 0.0) + n
        pw = n
        d = 2
        while d < C:
            pw = _mm3(pw, pw)
            tinv = tinv + _mm3(tinv, pw)
            d *= 2

        x1 = _mm3(l_k, v_s)
        tu = _mm3(tinv, jnp.concatenate([x1, a_s], axis=1))
        u = tu[:, :W]
        at = tu[:, W:]
        mb = _mm3(m_b, tu)
        y_loc = _mm3(m_k, v_s) - mb[:, :W]
        r_t = r_s - mb[:, W:]
        p_m = jnp.where(eye_w, w_end, 0.0) - _mm3(at, bt_s, _TN)
        q_m = _mm3(v_s, kt_s, _TN) - _mm3(u, bt_s, _TN)

        s0 = s_ref[...]
        y_st = _mm3(r_t, s0, _NT) + y_loc
        yc = y_st[0:C]
        for h in range(1, RW_HEADS):
            yc = yc + y_st[h * C:(h + 1) * C]
        ys.append(yc)
        s_ref[...] = _mm3(s0, p_m) + q_m

    y = jnp.concatenate(ys, axis=0) if len(ys) > 1 else ys[0]
    state_ref[0] = s_ref[...]
    inv_n = 1.0 / RW_HEAD
    mean = segsum(y) * inv_n
    yc = y - mean
    var = segsum(yc * yc) * inv_n
    yn = yc * lax.rsqrt(var + GN_EPS) * gnw_ref[...] + gnb_ref[...]
    y_ref[0] = ((yn + bonus) * g).astype(BF)


def _rwkv(prw, shift0, state_bd, mu, w0, w2p, a0, a2p, g2p, k_k, k_a, r_k, gn_w, gn_b, ones_bd):
    B, S, _ = prw.shape
    R = min(RW_TILE, S)
    C = min(RW_CHUNK, S)
    vec = lambda b, t: (0, 0)
    wv = pl.BlockSpec((1, RW_WIDTH), vec)
    lora = pl.BlockSpec((LANES, RW_WIDTH), vec)
    return pl.pallas_call(
        functools.partial(_rwkv_kernel, R=R, C=C),
        grid=(B, S // R),
        in_specs=[
            pl.BlockSpec((1, R, RW_IN), lambda b, t: (b, t, 0)),
            pl.BlockSpec((1, 1, RW_IN), lambda b, t: (b, 0, 0)),
            pl.BlockSpec((1, RW_WIDTH, RW_WIDTH), lambda b, t: (b, 0, 0)),
            pl.BlockSpec((1, RW_IN), vec),
            wv, lora, wv, lora, lora, wv, wv, wv, wv, wv,
            pl.BlockSpec((RW_WIDTH, RW_WIDTH), vec),
        ],
        out_specs=[
            pl.BlockSpec((1, R, RW_WIDTH), lambda b, t: (b, t, 0)),
            pl.BlockSpec((1, 1, RW_IN), lambda b, t: (b, 0, 0)),
            pl.BlockSpec((1, RW_WIDTH, RW_WIDTH), lambda b, t: (b, 0, 0)),
        ],
        out_shape=[
            jax.ShapeDtypeStruct((B, S, RW_WIDTH), BF),
            jax.ShapeDtypeStruct((B, 1, RW_IN), F32),
            jax.ShapeDtypeStruct((B, RW_WIDTH, RW_WIDTH), F32),
        ],
        scratch_shapes=[pltpu.VMEM((1, RW_IN), F32), pltpu.VMEM((RW_WIDTH, RW_WIDTH), F32)],
        compiler_params=_cparams("parallel", "arbitrary"),
        name="rwkv7",
    )(prw, shift0, state_bd, mu, w0, w2p, a0, a2p, g2p, k_k, k_a, r_k, gn_w, gn_b, ones_bd)


def _attn_causal_kernel(q_ref, k_ref, o_ref, m_ref, l_ref, acc_ref, *, tq):
    qi = pl.program_id(1)
    q = q_ref[0].reshape(MLA_HEADS * tq, 256)
    m_ref[...] = jnp.full(m_ref.shape, -jnp.inf, F32)
    l_ref[...] = jnp.zeros(l_ref.shape, F32)
    acc_ref[...] = jnp.zeros(acc_ref.shape, F32)

    def step(kblk, masked):
        s = lax.dot_general(q, kblk, _NT, preferred_element_type=F32) * MLA_SCALE
        if masked:
            rc = (lax.broadcasted_iota(jnp.int32, s.shape, 0) % tq) // CHUNK
            cc = lax.broadcasted_iota(jnp.int32, s.shape, 1) // CHUNK
            s = jnp.where(cc <= rc, s, -jnp.inf)
        m_prev = m_ref[...]
        m_new = jnp.maximum(m_prev, jnp.max(s, axis=-1, keepdims=True))
        alpha = jnp.exp(m_prev - m_new)
        p = jnp.exp(s - m_new)
        l_ref[...] = alpha * l_ref[...] + jnp.sum(p, axis=-1, keepdims=True)
        acc_ref[...] = alpha * acc_ref[...] + jnp.dot(p.astype(BF), kblk[:, :KV_LORA],
                                                      preferred_element_type=F32)
        m_ref[...] = m_new

    def body(ki, carry):
        step(k_ref[0, pl.ds(pl.multiple_of(ki * tq, tq), tq), :], False)
        return carry

    lax.fori_loop(0, qi, body, 0)
    step(k_ref[0, pl.ds(pl.multiple_of(qi * tq, tq), tq), :], True)
    o = acc_ref[...] / l_ref[...]
    for h in range(MLA_HEADS):
        o_ref[0, :, h * KV_LORA:(h + 1) * KV_LORA] = o[h * tq:(h + 1) * tq].astype(BF)


def _attn_causal(qcat, kcat):
    B, H, S, _ = qcat.shape
    tq = min(ATT_TILE, S)
    return pl.pallas_call(
        functools.partial(_attn_causal_kernel, tq=tq),
        grid=(B, S // tq),
        in_specs=[
            pl.BlockSpec((1, H, tq, 256), lambda b, i: (b, 0, i, 0)),
            pl.BlockSpec((1, S, 256), lambda b, i: (b, 0, 0)),
        ],
        out_specs=pl.BlockSpec((1, tq, H * KV_LORA), lambda b, i: (b, i, 0)),
        out_shape=jax.ShapeDtypeStruct((B, S, H * KV_LORA), BF),
        scratch_shapes=[pltpu.VMEM((H * tq, 1), F32), pltpu.VMEM((H * tq, 1), F32),
                        pltpu.VMEM((H * tq, KV_LORA), F32)],
        compiler_params=_cparams("parallel", "arbitrary"),
        name="mla_causal",
    )(qcat, kcat)


def _attn_full_kernel(q_ref, kp_ref, kn_ref, o_ref, *, S):
    q = q_ref[0].reshape(MLA_HEADS * S, 256)
    kp = kp_ref[0]
    kn = kn_ref[0]
    s1 = lax.dot_general(q, kp, _NT, preferred_element_type=F32) * MLA_SCALE
    s2 = lax.dot_general(q, kn, _NT, preferred_element_type=F32) * MLA_SCALE
    m = jnp.maximum(jnp.max(s1, axis=-1, keepdims=True), jnp.max(s2, axis=-1, keepdims=True))
    p1 = jnp.exp(s1 - m)
    p2 = jnp.exp(s2 - m)
    l = jnp.sum(p1, axis=-1, keepdims=True) + jnp.sum(p2, axis=-1, keepdims=True)
    o = (jnp.dot(p1.astype(BF), kp[:, :KV_LORA], preferred_element_type=F32)
         + jnp.dot(p2.astype(BF), kn[:, :KV_LORA], preferred_element_type=F32)) / l
    for h in range(MLA_HEADS):
        o_ref[0, :, h * KV_LORA:(h + 1) * KV_LORA] = o[h * S:(h + 1) * S].astype(BF)


def _attn_full(qcat, kpast, knew):
    B, H, S, _ = qcat.shape
    P = kpast.shape[1]
    return pl.pallas_call(
        functools.partial(_attn_full_kernel, S=S),
        grid=(B,),
        in_specs=[
            pl.BlockSpec((1, H, S, 256), lambda b: (b, 0, 0, 0)),
            pl.BlockSpec((1, P, 256), lambda b: (b, 0, 0)),
            pl.BlockSpec((1, S, 256), lambda b: (b, 0, 0)),
        ],
        out_specs=pl.BlockSpec((1, S, H * KV_LORA), lambda b: (b, 0, 0)),
        out_shape=jax.ShapeDtypeStruct((B, S, H * KV_LORA), BF),
        compiler_params=_cparams("parallel"),
        name="mla_full",
    )(qcat, kpast, knew)


def _out_kernel(x_ref, yrg_ref, yrw_ref, ol_ref, wuv_ref, wout_ref, o_ref):
    ymla = jnp.dot(ol_ref[...], wuv_ref[...], preferred_element_type=F32)
    ycat = jnp.concatenate([yrg_ref[...], yrw_ref[...], ymla.astype(BF)], axis=-1)
    o_ref[...] = x_ref[...] + jnp.dot(ycat, wout_ref[...], preferred_element_type=F32)


def _out_proj(x2d, yrg, yrw, olat, wuv_bd, wout):
    T = x2d.shape[0]
    tm = min(ROW_TILE, T)
    row = lambda i: (i, 0)
    const = lambda i: (0, 0)
    return pl.pallas_call(
        _out_kernel,
        grid=(T // tm,),
        in_specs=[
            pl.BlockSpec((tm, D_MODEL), row),
            pl.BlockSpec((tm, RG_WIDTH), row),
            pl.BlockSpec((tm, RW_WIDTH), row),
            pl.BlockSpec((tm, MLA_HEADS * KV_LORA), row),
            pl.BlockSpec((MLA_HEADS * KV_LORA, MLA_HEADS * V_HEAD), const),
            pl.BlockSpec((D_MODEL, D_MODEL), const),
        ],
        out_specs=pl.BlockSpec((tm, D_MODEL), row),
        out_shape=jax.ShapeDtypeStruct((T, D_MODEL), F32),
        compiler_params=_cparams("parallel"),
        name="out_proj",
    )(x2d, yrg, yrw, olat, wuv_bd, wout)


def _mem_kernel(x_ref, g_ref, wq_ref, mk_ref, mv_ref, wo_ref, o_ref):
    x = x_ref[...]
    hn = _rms(x, g_ref[...]).astype(BF)
    q = jnp.dot(hn, wq_ref[...], preferred_element_type=F32).astype(BF)
    mk = mk_ref[0]
    mv = mv_ref[0]
    outs = []
    for h in range(MEM_HEADS):
        sl = slice(h * MEM_HEAD, (h + 1) * MEM_HEAD)
        s = lax.dot_general(q[:, sl], mk[:, sl], _NT, preferred_element_type=F32) * MEM_SCALE
        e = jnp.exp(s - jnp.max(s, axis=-1, keepdims=True))
        pr = e / jnp.sum(e, axis=-1, keepdims=True)
        outs.append(jnp.dot(pr.astype(BF), mv[:, sl], preferred_element_type=F32).astype(BF))
    o = jnp.concatenate(outs, axis=-1)
    o_ref[...] = x + jnp.dot(o, wo_ref[...], preferred_element_type=F32)


def _mem_attn(x2d, B, S, g, wq, mk, mv, wo):
    T = B * S
    tm = min(ROW_TILE, S)
    nst = S // tm
    row = lambda i: (i, 0)
    const = lambda i: (0, 0)
    return pl.pallas_call(
        _mem_kernel,
        grid=(T // tm,),
        in_specs=[
            pl.BlockSpec((tm, D_MODEL), row),
            pl.BlockSpec((1, D_MODEL), const),
            pl.BlockSpec((D_MODEL, D_MODEL), const),
            pl.BlockSpec((1, N_MEM, D_MODEL), lambda i: (i // nst, 0, 0)),
            pl.BlockSpec((1, N_MEM, D_MODEL), lambda i: (i // nst, 0, 0)),
            pl.BlockSpec((D_MODEL, D_MODEL), const),
        ],
        out_specs=pl.BlockSpec((tm, D_MODEL), row),
        out_shape=jax.ShapeDtypeStruct((T, D_MODEL), F32),
        compiler_params=_cparams("parallel"),
        name="mem_attn",
    )(x2d, g, wq, mk, mv, wo)


def _ffn_kernel(x_ref, g_ref, w1_ref, w2_ref, gf_ref, o_ref, xn_ref, acc_ref, *, final):
    j = pl.program_id(1)

    @pl.when(j == 0)
    def _():
        xn_ref[...] = _rms(x_ref[...], g_ref[...]).astype(BF)
        acc_ref[...] = jnp.zeros(acc_ref.shape, F32)

    h = jnp.dot(xn_ref[...], w1_ref[...], preferred_element_type=F32)
    h = jnp.square(jnp.maximum(h, 0.0)).astype(BF)
    acc_ref[...] += jnp.dot(h, w2_ref[...], preferred_element_type=F32)

    @pl.when(j == pl.num_programs(1) - 1)
    def _():
        y = x_ref[...] + acc_ref[...]
        if final:
            y = _rms(y, gf_ref[...])
        o_ref[...] = y


def _ffn(x2d, g, w1, w2, gf, final):
    T = x2d.shape[0]
    tm = min(ROW_TILE, T)
    return pl.pallas_call(
        functools.partial(_ffn_kernel, final=final),
        grid=(T // tm, D_FF // FF_TILE),
        in_specs=[
            pl.BlockSpec((tm, D_MODEL), lambda i, j: (i, 0)),
            pl.BlockSpec((1, D_MODEL), lambda i, j: (0, 0)),
            pl.BlockSpec((D_MODEL, FF_TILE), lambda i, j: (0, j)),
            pl.BlockSpec((FF_TILE, D_MODEL), lambda i, j: (j, 0)),
            pl.BlockSpec((1, D_MODEL), lambda i, j: (0, 0)),
        ],
        out_specs=pl.BlockSpec((tm, D_MODEL), lambda i, j: (i, 0)),
        out_shape=jax.ShapeDtypeStruct((T, D_MODEL), F32),
        scratch_shapes=[pltpu.VMEM((tm, D_MODEL), BF), pltpu.VMEM((tm, D_MODEL), F32)],
        compiler_params=_cparams("parallel", "arbitrary"),
        name="ffn",
    )(x2d, g, w1, w2, gf)


def _memproj_kernel(m_ref, g_ref, w_ref, k_ref, v_ref, kb_ref, vb_ref):
    mn = _rms(m_ref[...], g_ref[...]).astype(BF)
    kv = jnp.dot(mn, w_ref[...], preferred_element_type=F32)
    k = kv[:, :D_MODEL]
    v = kv[:, D_MODEL:]
    k_ref[...] = k
    v_ref[...] = v
    kb_ref[...] = k.astype(BF)
    vb_ref[...] = v.astype(BF)


def _mem_project(mem2d, g, wkv):
    T = mem2d.shape[0]
    tm = min(ROW_TILE, T)
    row = lambda i: (i, 0)
    const = lambda i: (0, 0)
    return pl.pallas_call(
        _memproj_kernel,
        grid=(T // tm,),
        in_specs=[
            pl.BlockSpec((tm, D_MODEL), row),
            pl.BlockSpec((1, D_MODEL), const),
            pl.BlockSpec((D_MODEL, 2 * D_MODEL), const),
        ],
        out_specs=[pl.BlockSpec((tm, D_MODEL), row)] * 4,
        out_shape=[jax.ShapeDtypeStruct((T, D_MODEL), F32)] * 2
                  + [jax.ShapeDtypeStruct((T, D_MODEL), BF)] * 2,
        compiler_params=_cparams("parallel"),
        name="mem_project",
    )(mem2d, g, wkv)


def _block_diag(blocks):
    G, m, n = blocks.shape
    eye = jnp.eye(G, dtype=blocks.dtype)
    return jnp.einsum('gmn,gh->gmhn', blocks, eye).reshape(G * m, G * n)


def _prep_weights(l, W):
    r2 = lambda v: v.reshape(1, -1)
    w_in = W['w_in'][l]
    kr_cols = w_in[:, 1792:1824]
    kr_swap = jnp.concatenate([kr_cols[:, 16:], kr_cols[:, :16]], axis=1)
    w_ext = jnp.concatenate([w_in[:, :1792], jnp.tile(kr_cols, (1, 4)), jnp.tile(kr_swap, (1, 4))],
                            axis=1).astype(BF)
    w_uq = W['mla_w_uq'][l]
    rope = w_uq[:, :, QK_NOPE:]
    rope_swap = jnp.concatenate([rope[:, :, 16:], rope[:, :, :16]], axis=2)
    wq_ext = jnp.concatenate([w_uq[:, :, :QK_NOPE].reshape(Q_LORA, -1), rope.reshape(Q_LORA, -1),
                              rope_swap.reshape(Q_LORA, -1)], axis=1).astype(BF)
    wuk_bd = _block_diag(jnp.transpose(W['mla_w_uk'][l], (1, 2, 0))).astype(BF)
    wuv_bd = _block_diag(jnp.transpose(W['mla_w_uv'][l], (1, 0, 2))).astype(BF)
    zpad = lambda top, mat, bot: jnp.concatenate(
        [jnp.zeros((top, RW_WIDTH), F32), mat, jnp.zeros((bot, RW_WIDTH), F32)], axis=0)
    return dict(
        norm_mix=r2(W['norm_mix'][l]), w_ext=w_ext, gq=r2(W['mla_q_norm'][l]),
        gkv=r2(W['mla_kv_norm'][l]), wq_ext=wq_ext, wuk_bd=wuk_bd, wuv_bd=wuv_bd,
        cw=W['rg_conv_w'][l], cb=r2(W['rg_conv_b'][l]),
        wa_bd=_block_diag(W['rg_wa'][l]).astype(BF), ba=r2(W['rg_ba'][l]),
        wx_bd=_block_diag(W['rg_wx'][l]).astype(BF), bx=r2(W['rg_bx'][l]),
        lam=r2(W['rg_lambda'][l]),
        mu=r2(W['rw_mu'][l]), w0=r2(W['rw_w0'][l]), w2p=zpad(0, W['rw_w2'][l], 96),
        a0=r2(W['rw_a0'][l]), a2p=zpad(32, W['rw_a2'][l], 64), g2p=zpad(64, W['rw_g2'][l], 0),
        k_k=r2(W['rw_k_k'][l]), k_a=r2(W['rw_k_a'][l]), r_k=r2(W['rw_r_k'][l]),
        gn_w=r2(W['rw_gn_w'][l]), gn_b=r2(W['rw_gn_b'][l]),
        w_out=W['w_out'][l].astype(BF),
        norm_mem=r2(W['norm_mem'][l]), wq=W['mem_w_q'][l].reshape(D_MODEL, D_MODEL).astype(BF),
        wo=W['mem_w_o'][l].reshape(D_MODEL, D_MODEL).astype(BF),
        norm_ffn=r2(W['norm_ffn'][l]), w1=W['ffn_w1'][l].astype(BF), w2=W['ffn_w2'][l].astype(BF),
        norm_mem_kv=r2(W['norm_mem_kv'][l]),
        wkv=jnp.concatenate([W['mem_w_k'][l].reshape(D_MODEL, D_MODEL),
                             W['mem_w_v'][l].reshape(D_MODEL, D_MODEL)], axis=1).astype(BF),
    )


def _rope_tables(pos, rows):
    half = QK_ROPE // 2
    inv = ROPE_BASE ** (-jnp.arange(half, dtype=F32) / half)
    ang = pos.astype(F32)[:, None] * inv
    cos, sin = jnp.cos(ang), jnp.sin(ang)
    cosq = jnp.tile(jnp.concatenate([cos, cos], axis=1), (1, MLA_HEADS))
    sinq = jnp.tile(jnp.concatenate([-sin, sin], axis=1), (1, MLA_HEADS))
    reps = max(1, rows // pos.shape[0])
    return jnp.tile(cosq, (reps, 1)), jnp.tile(sinq, (reps, 1))


def _layer(x2d, B, S, P, tabs, rg_buf, rg_h, rw_shift, rw_state, past, mem_k, mem_v, gf, final):
    cosq, sinq = tabs
    pa, prw, ckv, kr, kcat, qcat = _in_proj(x2d, B, S, P['norm_mix'], P['w_ext'], P['gq'], P['gkv'],
                                            P['wq_ext'], P['wuk_bd'], cosq, sinq)
    buf8 = jnp.concatenate([jnp.zeros((B, 5, RG_WIDTH), F32), rg_buf], axis=1)
    yrg, buf_new, h_new = _rglru(pa.reshape(B, S, 512), buf8, rg_h.reshape(B, 1, RG_WIDTH),
                                 P['cw'], P['cb'], P['wa_bd'], P['ba'], P['wx_bd'], P['bx'], P['lam'])
    eye_h = jnp.eye(RW_HEADS, dtype=F32)
    state_bd = jnp.einsum('bhij,hg->bhigj', rw_state, eye_h).reshape(B, RW_WIDTH, RW_WIDTH)
    ones_bd = _block_diag(jnp.ones((RW_HEADS, RW_HEAD, RW_HEAD), F32)).astype(BF)
    yrw, shift_new, state_new = _rwkv(prw.reshape(B, S, RW_IN), rw_shift.reshape(B, 1, RW_IN), state_bd,
                                      P['mu'], P['w0'], P['w2p'], P['a0'], P['a2p'], P['g2p'],
                                      P['k_k'], P['k_a'], P['r_k'], P['gn_w'], P['gn_b'], ones_bd)
    wkv_new = jnp.einsum('bhigj,hg->bhij',
                         state_new.reshape(B, RW_HEADS, RW_HEAD, RW_HEADS, RW_HEAD), eye_h)
    kcat3 = kcat.reshape(B, S, 256)
    if past is None:
        olat = _attn_causal(qcat, kcat3)
    else:
        past_ckv, past_kr = past
        kpast = jnp.concatenate([past_ckv, jnp.tile(past_kr, (1, 1, 4))], axis=-1).astype(BF)
        olat = _attn_full(qcat, kpast, kcat3)
    x1 = _out_proj(x2d, yrg.reshape(B * S, RG_WIDTH), yrw.reshape(B * S, RW_WIDTH),
                   olat.reshape(B * S, MLA_HEADS * KV_LORA), P['wuv_bd'], P['w_out'])
    x2 = _mem_attn(x1, B, S, P['norm_mem'], P['wq'], mem_k, mem_v, P['wo'])
    x3 = _ffn(x2, P['norm_ffn'], P['w1'], P['w2'], gf, final)
    new_state = (buf_new[:, 5:8], h_new.reshape(B, RG_WIDTH), shift_new.reshape(B, RW_IN), wkv_new,
                 ckv.reshape(B, S, KV_LORA), kr.reshape(B, S, QK_ROPE))
    return x3, new_state


def kernel(x_prompt, x_sample, state_rg_conv, state_rg_h, state_rw_shift, state_rw_wkv, cache_mla_ckv, cache_mla_krope, cache_mem_k, cache_mem_v, mem_prompt, norm_mix, w_in, rg_conv_w, rg_conv_b, rg_wa, rg_ba, rg_wx, rg_bx, rg_lambda, rw_mu, rw_w0, rw_w2, rw_a0, rw_a2, rw_g2, rw_k_k, rw_k_a, rw_r_k, rw_gn_w, rw_gn_b, mla_q_norm, mla_kv_norm, mla_w_uq, mla_w_uk, mla_w_uv, w_out, norm_mem, norm_mem_kv, mem_w_q, mem_w_k, mem_w_v, mem_w_o, norm_ffn, ffn_w1, ffn_w2, norm_final):
    W = dict(norm_mix=norm_mix, w_in=w_in, rg_conv_w=rg_conv_w, rg_conv_b=rg_conv_b, rg_wa=rg_wa,
             rg_ba=rg_ba, rg_wx=rg_wx, rg_bx=rg_bx, rg_lambda=rg_lambda, rw_mu=rw_mu, rw_w0=rw_w0,
             rw_w2=rw_w2, rw_a0=rw_a0, rw_a2=rw_a2, rw_g2=rw_g2, rw_k_k=rw_k_k, rw_k_a=rw_k_a,
             rw_r_k=rw_r_k.reshape(rw_r_k.shape[0], RW_WIDTH), rw_gn_w=rw_gn_w, rw_gn_b=rw_gn_b,
             mla_q_norm=mla_q_norm, mla_kv_norm=mla_kv_norm, mla_w_uq=mla_w_uq, mla_w_uk=mla_w_uk,
             mla_w_uv=mla_w_uv, w_out=w_out, norm_mem=norm_mem, norm_mem_kv=norm_mem_kv,
             mem_w_q=mem_w_q, mem_w_k=mem_w_k, mem_w_v=mem_w_v, mem_w_o=mem_w_o, norm_ffn=norm_ffn,
             ffn_w1=ffn_w1, ffn_w2=ffn_w2)
    depth = norm_mix.shape[0]
    Bp, Sp, _ = x_prompt.shape
    Bs, Ss, _ = x_sample.shape
    past_len = cache_mla_ckv.shape[2]
    tabs_p = _rope_tables(jnp.arange(Sp, dtype=jnp.int32), min(ROW_TILE, Sp))
    tabs_s = _rope_tables(past_len + jnp.arange(Ss, dtype=jnp.int32), min(ROW_TILE, Ss))
    gf = norm_final.reshape(1, D_MODEL)

    xp = x_prompt.reshape(Bp * Sp, D_MODEL)
    xs = x_sample.reshape(Bs * Ss, D_MODEL)
    z = lambda *s: jnp.zeros(s, F32)
    p_states, s_states = [], []
    for l in range(depth):
        P = _prep_weights(l, W)
        final = l == depth - 1
        mk, mv, mkb, mvb = _mem_project(mem_prompt.reshape(Bp * N_MEM, D_MODEL), P['norm_mem_kv'], P['wkv'])
        xp, st_p = _layer(xp, Bp, Sp, P, tabs_p, z(Bp, 3, RG_WIDTH), z(Bp, RG_WIDTH), z(Bp, RW_IN),
                          z(Bp, RW_HEADS, RW_HEAD, RW_HEAD), None,
                          mkb.reshape(Bp, N_MEM, D_MODEL), mvb.reshape(Bp, N_MEM, D_MODEL), gf, final)
        xs, st_s = _layer(xs, Bs, Ss, P, tabs_s, state_rg_conv[l], state_rg_h[l], state_rw_shift[l],
                          state_rw_wkv[l], (cache_mla_ckv[l], cache_mla_krope[l]),
                          cache_mem_k[l].reshape(Bs, N_MEM, D_MODEL).astype(BF),
                          cache_mem_v[l].reshape(Bs, N_MEM, D_MODEL).astype(BF), gf, final)
        p_states.append(st_p + (mk.reshape(Bp, N_MEM, MEM_HEADS, MEM_HEAD),
                                mv.reshape(Bp, N_MEM, MEM_HEADS, MEM_HEAD)))
        s_states.append(st_s)

    sp = [jnp.stack(t) for t in zip(*p_states)]
    ss = [jnp.stack(t) for t in zip(*s_states)]
    return (xp.reshape(Bp, Sp, D_MODEL), xs.reshape(Bs, Ss, D_MODEL),
            sp[0], sp[1], sp[2], sp[3], sp[4], sp[5], sp[6], sp[7],
            ss[0], ss[1], ss[2], ss[3], ss[4], ss[5])
```

```python
import functools
import math

import jax
import jax.numpy as jnp
from jax import lax
from jax.experimental import pallas as pl
from jax.experimental.pallas import tpu as pltpu

BF = jnp.bfloat16
F32 = jnp.float32

D_MODEL = 1024
RG_WIDTH = 256
RG_BLOCKS = 4
CONV_W = 4
LRU_C = 8.0
RW_HEADS = 4
RW_HEAD = 64
RW_WIDTH = 256
RW_IN = 896
GN_EPS = 64e-5
L2_EPS = 1e-12
MLA_HEADS = 8
QK_NOPE = 64
QK_ROPE = 32
V_HEAD = 64
Q_LORA = 256
KV_LORA = 128
MLA_SCALE = (QK_NOPE + QK_ROPE) ** -0.5
SCALE_LOG2E = MLA_SCALE * math.log2(math.e)
ROPE_BASE = 10000.0
CHUNK = 64
N_MEM = 256
MEM_HEADS = 4
MEM_HEAD = 256
MEM_SCALE = MEM_HEAD ** -0.5
D_FF = 4096
EPS = 1e-6

LANES = 128
VMEM_LIMIT = 52 * 1024 * 1024

ROW_TILE = 512
RG_TILE = 512
RW_TILE = 256
RW_CHUNK = 64
RW_PASSES_GRAM = 1
RW_PASSES_INV = 1
RW_PASSES_LOCAL = 1
RW_PASSES_STATE = 1
ATT_TILE = 512
FF_TILE = 1024


def _cparams(*sem):
    return pltpu.CompilerParams(dimension_semantics=sem, vmem_limit_bytes=VMEM_LIMIT)


def _rms(x, g):
    return x * lax.rsqrt(jnp.mean(x * x, axis=-1, keepdims=True) + EPS) * g


def _dot(a, b):
    return jnp.dot(a.astype(BF), b.astype(BF), preferred_element_type=F32)


def _dot_nt(a, b):
    return lax.dot_general(a.astype(BF), b.astype(BF), (((1,), (1,)), ((), ())),
                           preferred_element_type=F32)


def _split(x):
    hi = x.astype(BF)
    lo = (x - hi.astype(F32)).astype(BF)
    return hi, lo


def _mm3(a, b, dims=(((1,), (0,)), ((), ())), passes=3):
    dg = functools.partial(lax.dot_general, dimension_numbers=dims, preferred_element_type=F32)
    if passes == 1:
        return dg(a.astype(BF), b.astype(BF))
    ah, al = _split(a)
    bh, bl = _split(b)
    return dg(ah, bh) + (dg(ah, bl) + dg(al, bh))


_NT = (((1,), (1,)), ((), ()))
_TN = (((0,), (0,)), ((), ()))


def _softplus(x):
    return jnp.maximum(x, 0.0) + jnp.log1p(jnp.exp(-jnp.abs(x)))


def _sigmoid(x):
    return 1.0 / (1.0 + jnp.exp(-x))


def _gelu_tanh(x):
    c = math.sqrt(2.0 / math.pi)
    return 0.5 * x * (1.0 + jnp.tanh(c * (x + 0.044715 * (x * x * x))))


def _shift_rows(x, d):
    return pltpu.roll(x, d, 0)


def _in_kernel(x_ref, g_ref, w_ref, gq_ref, gkv_ref, wq_ref, wuk_ref, cos_ref, sin_ref,
               pa_ref, prw_ref, ckv_ref, kr_ref, kcat_ref, qcat_ref):
    xn = _rms(x_ref[...], g_ref[...]).astype(BF)
    p = jnp.dot(xn, w_ref[...], preferred_element_type=F32)
    pa_ref[...] = p[:, :512]
    prw_ref[...] = p[:, 512:1408]
    cq = p[:, 1408:1664]
    ckv = p[:, 1664:1792]
    ka = p[:, 1792:1920]
    kb = p[:, 1920:2048]
    cos = cos_ref[...]
    sin = sin_ref[...]
    ckvn = _rms(ckv, gkv_ref[...])
    krt = ka * cos[:, :LANES] + kb * sin[:, :LANES]
    ckv_ref[...] = ckvn
    kr_ref[...] = krt[:, :QK_ROPE]
    kcat_ref[...] = jnp.concatenate([ckvn, krt], axis=-1).astype(BF)

    cqn = _rms(cq, gq_ref[...]).astype(BF)
    qq = jnp.dot(cqn, wq_ref[...], preferred_element_type=F32)
    rr = qq[:, 512:768] * cos + qq[:, 768:1024] * sin
    qlat = jnp.dot(qq[:, :512].astype(BF), wuk_ref[...], preferred_element_type=F32)
    lane_grp = lax.broadcasted_iota(jnp.int32, (1, LANES), 1) // QK_ROPE
    for h in range(MLA_HEADS):
        g = h // 4
        rpart = jnp.where(lane_grp == (h % 4), rr[:, g * LANES:(g + 1) * LANES], 0.0)
        qcat_ref[0, h, :, :LANES] = qlat[:, h * LANES:(h + 1) * LANES].astype(BF)
        qcat_ref[0, h, :, LANES:] = rpart.astype(BF)


def _in_proj(x2d, B, S, g, w_ext, gq, gkv, wq_ext, wuk_bd, cosq, sinq):
    T = B * S
    tm = min(ROW_TILE, S)
    nst = S // tm
    ntab = cosq.shape[0] // tm
    row = lambda i: (i, 0)
    const = lambda i: (0, 0)
    outs = pl.pallas_call(
        _in_kernel,
        grid=(T // tm,),
        in_specs=[
            pl.BlockSpec((tm, D_MODEL), row),
            pl.BlockSpec((1, D_MODEL), const),
            pl.BlockSpec((D_MODEL, 2048), const),
            pl.BlockSpec((1, Q_LORA), const),
            pl.BlockSpec((1, KV_LORA), const),
            pl.BlockSpec((Q_LORA, 1024), const),
            pl.BlockSpec((512, 1024), const),
            pl.BlockSpec((tm, 256), lambda i: (i % ntab, 0)),
            pl.BlockSpec((tm, 256), lambda i: (i % ntab, 0)),
        ],
        out_specs=[
            pl.BlockSpec((tm, 512), row),
            pl.BlockSpec((tm, RW_IN), row),
            pl.BlockSpec((tm, KV_LORA), row),
            pl.BlockSpec((tm, QK_ROPE), row),
            pl.BlockSpec((tm, 256), row),
            pl.BlockSpec((1, MLA_HEADS, tm, 256), lambda i: (i // nst, 0, i % nst, 0)),
        ],
        out_shape=[
            jax.ShapeDtypeStruct((T, 512), F32),
            jax.ShapeDtypeStruct((T, RW_IN), F32),
            jax.ShapeDtypeStruct((T, KV_LORA), F32),
            jax.ShapeDtypeStruct((T, QK_ROPE), F32),
            jax.ShapeDtypeStruct((T, 256), BF),
            jax.ShapeDtypeStruct((B, MLA_HEADS, S, 256), BF),
        ],
        compiler_params=_cparams("parallel"),
        name="in_proj",
    )(x2d, g, w_ext, gq, gkv, wq_ext, wuk_bd, cosq, sinq)
    return outs


def _rglru_kernel(pa_ref, buf0_ref, h0_ref, cw_ref, cb_ref, wa_ref, ba_ref, wx_ref, bx_ref,
                  lam_ref, y_ref, buf_ref, hout_ref, ext_ref, h_ref, *, R):
    t = pl.program_id(1)

    @pl.when(t == 0)
    def _():
        ext_ref[0:8, :] = buf0_ref[0]
        h_ref[...] = h0_ref[0]

    pa = pa_ref[0]
    xr = pa[:, :RG_WIDTH]
    gate = pa[:, RG_WIDTH:]
    ext_ref[8:8 + R, :] = xr
    cw = cw_ref[...]
    conv = (cb_ref[...] + cw[3:4] * xr + cw[2:3] * ext_ref[7:7 + R, :]
            + cw[1:2] * ext_ref[6:6 + R, :] + cw[0:1] * ext_ref[5:5 + R, :])
    hist = ext_ref[R:R + 8, :]
    ext_ref[0:8, :] = hist
    buf_ref[0] = hist

    ra = _sigmoid(_dot(conv, wa_ref[...]) + ba_ref[...])
    ia = _sigmoid(_dot(conv, wx_ref[...]) + bx_ref[...])
    log_a = (-LRU_C) * ra * _softplus(-lam_ref[...])
    a = jnp.exp(log_a)
    b = jnp.sqrt(-jnp.tanh(log_a) * (a * a + 1.0)) * (ia * conv)

    rows = lax.broadcasted_iota(jnp.int32, (R, RG_WIDTH), 0)
    d = 1
    while d < R:
        m = rows >= d
        b = jnp.where(m, a * _shift_rows(b, d), 0.0) + b
        a = jnp.where(m, a * _shift_rows(a, d), a)
        d *= 2
    h = a * h_ref[...] + b
    hl = h[R - 1:R, :]
    h_ref[...] = hl
    hout_ref[0] = hl
    y_ref[0] = (_gelu_tanh(gate) * h).astype(BF)


def _rglru(pa, buf8, h0, cw, cb, wa_bd, ba, wx_bd, bx, lam):
    B, S, _ = pa.shape
    R = min(RG_TILE, S)
    vec = lambda b, t: (0, 0)
    return pl.pallas_call(
        functools.partial(_rglru_kernel, R=R),
        grid=(B, S // R),
        in_specs=[
            pl.BlockSpec((1, R, 512), lambda b, t: (b, t, 0)),
            pl.BlockSpec((1, 8, RG_WIDTH), lambda b, t: (b, 0, 0)),
            pl.BlockSpec((1, 1, RG_WIDTH), lambda b, t: (b, 0, 0)),
            pl.BlockSpec((CONV_W, RG_WIDTH), vec),
            pl.BlockSpec((1, RG_WIDTH), vec),
            pl.BlockSpec((RG_WIDTH, RG_WIDTH), vec),
            pl.BlockSpec((1, RG_WIDTH), vec),
            pl.BlockSpec((RG_WIDTH, RG_WIDTH), vec),
            pl.BlockSpec((1, RG_WIDTH), vec),
            pl.BlockSpec((1, RG_WIDTH), vec),
        ],
        out_specs=[
            pl.BlockSpec((1, R, RG_WIDTH), lambda b, t: (b, t, 0)),
            pl.BlockSpec((1, 8, RG_WIDTH), lambda b, t: (b, 0, 0)),
            pl.BlockSpec((1, 1, RG_WIDTH), lambda b, t: (b, 0, 0)),
        ],
        out_shape=[
            jax.ShapeDtypeStruct((B, S, RG_WIDTH), BF),
            jax.ShapeDtypeStruct((B, 8, RG_WIDTH), F32),
            jax.ShapeDtypeStruct((B, 1, RG_WIDTH), F32),
        ],
        scratch_shapes=[pltpu.VMEM((R + 8, RG_WIDTH), F32), pltpu.VMEM((1, RG_WIDTH), F32)],
        compiler_params=_cparams("parallel", "arbitrary"),
        name="rglru",
    )(pa, buf8, h0, cw, cb, wa_bd, ba, wx_bd, bx, lam)


def _rwkv_kernel(p_ref, shift0_ref, state0_ref, mu_ref, w0_ref, w2_ref, a0_ref, a2_ref, g2_ref,
                 kk_ref, ka_ref, rk_ref, gnw_ref, gnb_ref, ones_ref,
                 y_ref, shift_ref, state_ref, prev_ref, s_ref, *, R, C):
    t = pl.program_id(1)
    W = RW_WIDTH
    HC = RW_HEADS * C

    @pl.when(t == 0)
    def _():
        prev_ref[...] = shift0_ref[0]
        s_ref[...] = state0_ref[0]

    p = p_ref[0]
    rows = lax.broadcasted_iota(jnp.int32, (R, RW_IN), 0)
    prev = jnp.where(rows == 0, prev_ref[...], _shift_rows(p, 1))
    last = p[R - 1:R, :]
    prev_ref[...] = last
    shift_ref[0] = last
    xs = p + mu_ref[...] * (prev - p)
    r = xs[:, 0:W]
    k = xs[:, W:2 * W]
    v = xs[:, 2 * W:3 * W]
    x4 = xs[:, 3 * W:]
    ones_bd = ones_ref[...]

    def segsum(x):
        hi = x.astype(BF)
        r1 = x - hi.astype(F32)
        mid = r1.astype(BF)
        lo = (r1 - mid.astype(F32)).astype(BF)
        dg = functools.partial(jnp.dot, preferred_element_type=F32)
        return dg(hi, ones_bd) + (dg(mid, ones_bd) + dg(lo, ones_bd))

    log_w = -_softplus(-(w0_ref[...] + _mm3(jnp.tanh(x4), w2_ref[...]))) - 0.5
    logw = -jnp.exp(log_w)
    a = _sigmoid(a0_ref[...] + _mm3(x4, a2_ref[...]))
    g = _mm3(_sigmoid(x4), g2_ref[...])
    kk = k * kk_ref[...]
    kk = kk * lax.rsqrt(segsum(kk * kk) + L2_EPS)
    k = k * (1.0 + (a - 1.0) * ka_ref[...])
    bonus = segsum(r * k * rk_ref[...]) * v
    kka = kk * a

    lane_head = lax.broadcasted_iota(jnp.int32, (1, W), 1) // RW_HEAD
    hmask = [lane_head == h for h in range(RW_HEADS)]

    def stack(x):
        return jnp.concatenate([jnp.where(hmask[h], x, 0.0) for h in range(RW_HEADS)], axis=0)

    ri = lax.broadcasted_iota(jnp.int32, (HC, HC), 0) % C
    ci = lax.broadcasted_iota(jnp.int32, (HC, HC), 1) % C
    strict = ri > ci
    incl = ri >= ci
    eye_hc = lax.broadcasted_iota(jnp.int32, (HC, HC), 0) == lax.broadcasted_iota(jnp.int32, (HC, HC), 1)
    eye_w = lax.broadcasted_iota(jnp.int32, (W, W), 0) == lax.broadcasted_iota(jnp.int32, (W, W), 1)
    crow = lax.broadcasted_iota(jnp.int32, (C, W), 0)

    mm_gram = functools.partial(_mm3, passes=RW_PASSES_GRAM)
    mm_inv = functools.partial(_mm3, passes=RW_PASSES_INV)
    mm_loc = functools.partial(_mm3, passes=RW_PASSES_LOCAL)
    mm_state = functools.partial(_mm3, passes=RW_PASSES_STATE)
    ys = []
    for c in range(R // C):
        sl = slice(c * C, (c + 1) * C)
        lw = logw[sl]
        cum = lw
        d = 1
        while d < C:
            cum = cum + jnp.where(crow >= d, _shift_rows(cum, d), 0.0)
            d *= 2
        cl = cum[C - 1:C, :]
        w_inc = jnp.exp(cum)
        w_inv = jnp.exp(-cum)
        w_prev = jnp.exp(cum - lw)
        w_rem = jnp.exp(cl - cum)
        w_end = jnp.exp(cl)
        a_s = stack(w_prev * kk[sl])
        b_s = stack(kka[sl] * w_inv)
        k_s = stack(k[sl] * w_inv)
        r_s = stack(r[sl] * w_inc)
        v_s = stack(v[sl])
        kt_s = stack(k[sl] * w_rem)
        bt_s = stack(kka[sl] * w_rem)

        bk = jnp.concatenate([b_s, k_s], axis=0)
        g_a = mm_gram(a_s, bk, _NT)
        g_r = mm_gram(r_s, bk, _NT)
        n = jnp.where(strict, -g_a[:, :HC], 0.0)
        l_k = jnp.where(strict, g_a[:, HC:], 0.0)
        m_b = jnp.where(incl, g_r[:, :HC], 0.0)
        m_k = jnp.where(incl, g_r[:, HC:], 0.0)

        tinv = jnp.where(eye_hc, 1.0, 0.0) + n
        pw = n
        d = 2
        while d < C:
            pw = mm_inv(pw, pw)
            tinv = tinv + mm_inv(tinv, pw)
            d *= 2

        x1 = mm_loc(l_k, v_s)
        tu = mm_loc(tinv, jnp.concatenate([x1, a_s], axis=1))
        u = tu[:, :W]
        at = tu[:, W:]
        mb = mm_loc(m_b, tu)
        y_loc = mm_loc(m_k, v_s) - mb[:, :W]
        r_t = r_s - mb[:, W:]
        p_m = jnp.where(eye_w, w_end, 0.0) - mm_loc(at, bt_s, _TN)
        q_m = mm_loc(v_s, kt_s, _TN) - mm_loc(u, bt_s, _TN)

        s0 = s_ref[...]
        y_st = mm_state(r_t, s0, _NT) + y_loc
        yc = y_st[0:C]
        for h in range(1, RW_HEADS):
            yc = yc + y_st[h * C:(h + 1) * C]
        ys.append(yc)
        s_ref[...] = mm_state(s0, p_m) + q_m

    y = jnp.concatenate(ys, axis=0) if len(ys) > 1 else ys[0]
    state_ref[0] = s_ref[...]
    inv_n = 1.0 / RW_HEAD
    mean = segsum(y) * inv_n
    yc = y - mean
    var = segsum(yc * yc) * inv_n
    yn = yc * lax.rsqrt(var + GN_EPS) * gnw_ref[...] + gnb_ref[...]
    y_ref[0] = ((yn + bonus) * g).astype(BF)


def _rwkv(prw, shift0, state_bd, mu, w0, w2p, a0, a2p, g2p, k_k, k_a, r_k, gn_w, gn_b, ones_bd):
    B, S, _ = prw.shape
    R = min(RW_TILE, S)
    C = min(RW_CHUNK, S)
    vec = lambda b, t: (0, 0)
    wv = pl.BlockSpec((1, RW_WIDTH), vec)
    lora = pl.BlockSpec((LANES, RW_WIDTH), vec)
    return pl.pallas_call(
        functools.partial(_rwkv_kernel, R=R, C=C),
        grid=(B, S // R),
        in_specs=[
            pl.BlockSpec((1, R, RW_IN), lambda b, t: (b, t, 0)),
            pl.BlockSpec((1, 1, RW_IN), lambda b, t: (b, 0, 0)),
            pl.BlockSpec((1, RW_WIDTH, RW_WIDTH), lambda b, t: (b, 0, 0)),
            pl.BlockSpec((1, RW_IN), vec),
            wv, lora, wv, lora, lora, wv, wv, wv, wv, wv,
            pl.BlockSpec((RW_WIDTH, RW_WIDTH), vec),
        ],
        out_specs=[
            pl.BlockSpec((1, R, RW_WIDTH), lambda b, t: (b, t, 0)),
            pl.BlockSpec((1, 1, RW_IN), lambda b, t: (b, 0, 0)),
            pl.BlockSpec((1, RW_WIDTH, RW_WIDTH), lambda b, t: (b, 0, 0)),
        ],
        out_shape=[
            jax.ShapeDtypeStruct((B, S, RW_WIDTH), BF),
            jax.ShapeDtypeStruct((B, 1, RW_IN), F32),
            jax.ShapeDtypeStruct((B, RW_WIDTH, RW_WIDTH), F32),
        ],
        scratch_shapes=[pltpu.VMEM((1, RW_IN), F32), pltpu.VMEM((RW_WIDTH, RW_WIDTH), F32)],
        compiler_params=_cparams("parallel", "arbitrary"),
        name="rwkv7",
    )(prw, shift0, state_bd, mu, w0, w2p, a0, a2p, g2p, k_k, k_a, r_k, gn_w, gn_b, ones_bd)


def _attn_causal_kernel(q_ref, k_ref, o_ref, m_ref, l_ref, acc_ref, *, tq):
    qi = pl.program_id(1)
    q = q_ref[0].reshape(MLA_HEADS * tq, 256)
    m_ref[...] = jnp.full(m_ref.shape, -jnp.inf, F32)
    l_ref[...] = jnp.zeros(l_ref.shape, F32)
    acc_ref[...] = jnp.zeros(acc_ref.shape, F32)

    def step(kblk, masked):
        s = lax.dot_general(kblk, q, _NT, preferred_element_type=F32)
        if masked:
            kc = lax.broadcasted_iota(jnp.int32, s.shape, 0) // CHUNK
            qc = (lax.broadcasted_iota(jnp.int32, s.shape, 1) % tq) // CHUNK
            s = jnp.where(kc <= qc, s, -jnp.inf)
        m_prev = m_ref[...]
        m_new = jnp.maximum(m_prev, jnp.max(s, axis=0, keepdims=True))
        alpha = jnp.exp2((m_prev - m_new) * SCALE_LOG2E)
        p = jnp.exp2((s - m_new) * SCALE_LOG2E)
        l_ref[...] = alpha * l_ref[...] + jnp.sum(p, axis=0, keepdims=True)
        pv = lax.dot_general(kblk[:, :KV_LORA], p.astype(BF), _TN, preferred_element_type=F32)
        acc_ref[...] = alpha * acc_ref[...] + pv
        m_ref[...] = m_new

    def body(ki, carry):
        step(k_ref[0, pl.ds(pl.multiple_of(ki * tq, tq), tq), :], False)
        return carry

    lax.fori_loop(0, qi, body, 0)
    step(k_ref[0, pl.ds(pl.multiple_of(qi * tq, tq), tq), :], True)
    o = acc_ref[...] / l_ref[...]
    for h in range(MLA_HEADS):
        o_ref[0, :, h * KV_LORA:(h + 1) * KV_LORA] = o[:, h * tq:(h + 1) * tq].T.astype(BF)


def _attn_causal(qcat, kcat):
    B, H, S, _ = qcat.shape
    tq = min(ATT_TILE, S)
    return pl.pallas_call(
        functools.partial(_attn_causal_kernel, tq=tq),
        grid=(B, S // tq),
        in_specs=[
            pl.BlockSpec((1, H, tq, 256), lambda b, i: (b, 0, i, 0)),
            pl.BlockSpec((1, S, 256), lambda b, i: (b, 0, 0)),
        ],
        out_specs=pl.BlockSpec((1, tq, H * KV_LORA), lambda b, i: (b, i, 0)),
        out_shape=jax.ShapeDtypeStruct((B, S, H * KV_LORA), BF),
        scratch_shapes=[pltpu.VMEM((1, H * tq), F32), pltpu.VMEM((1, H * tq), F32),
                        pltpu.VMEM((KV_LORA, H * tq), F32)],
        compiler_params=_cparams("parallel", "arbitrary"),
        name="mla_causal",
    )(qcat, kcat)


def _attn_full_kernel(q_ref, kp_ref, kn_ref, o_ref, *, S):
    q = q_ref[0].reshape(MLA_HEADS * S, 256)
    kp = kp_ref[0]
    kn = kn_ref[0]
    s1 = lax.dot_general(q, kp, _NT, preferred_element_type=F32) * MLA_SCALE
    s2 = lax.dot_general(q, kn, _NT, preferred_element_type=F32) * MLA_SCALE
    m = jnp.maximum(jnp.max(s1, axis=-1, keepdims=True), jnp.max(s2, axis=-1, keepdims=True))
    p1 = jnp.exp(s1 - m)
    p2 = jnp.exp(s2 - m)
    l = jnp.sum(p1, axis=-1, keepdims=True) + jnp.sum(p2, axis=-1, keepdims=True)
    o = (jnp.dot(p1.astype(BF), kp[:, :KV_LORA], preferred_element_type=F32)
         + jnp.dot(p2.astype(BF), kn[:, :KV_LORA], preferred_element_type=F32)) / l
    for h in range(MLA_HEADS):
        o_ref[0, :, h * KV_LORA:(h + 1) * KV_LORA] = o[h * S:(h + 1) * S].astype(BF)


def _attn_full(qcat, kpast, knew):
    B, H, S, _ = qcat.shape
    P = kpast.shape[1]
    return pl.pallas_call(
        functools.partial(_attn_full_kernel, S=S),
        grid=(B,),
        in_specs=[
            pl.BlockSpec((1, H, S, 256), lambda b: (b, 0, 0, 0)),
            pl.BlockSpec((1, P, 256), lambda b: (b, 0, 0)),
            pl.BlockSpec((1, S, 256), lambda b: (b, 0, 0)),
        ],
        out_specs=pl.BlockSpec((1, S, H * KV_LORA), lambda b: (b, 0, 0)),
        out_shape=jax.ShapeDtypeStruct((B, S, H * KV_LORA), BF),
        compiler_params=_cparams("parallel"),
        name="mla_full",
    )(qcat, kpast, knew)


def _out_kernel(x_ref, yrg_ref, yrw_ref, ol_ref, wuv_ref, wout_ref, o_ref):
    ymla = jnp.dot(ol_ref[...], wuv_ref[...], preferred_element_type=F32)
    ycat = jnp.concatenate([yrg_ref[...], yrw_ref[...], ymla.astype(BF)], axis=-1)
    o_ref[...] = x_ref[...] + jnp.dot(ycat, wout_ref[...], preferred_element_type=F32)


def _out_proj(x2d, yrg, yrw, olat, wuv_bd, wout):
    T = x2d.shape[0]
    tm = min(ROW_TILE, T)
    row = lambda i: (i, 0)
    const = lambda i: (0, 0)
    return pl.pallas_call(
        _out_kernel,
        grid=(T // tm,),
        in_specs=[
            pl.BlockSpec((tm, D_MODEL), row),
            pl.BlockSpec((tm, RG_WIDTH), row),
            pl.BlockSpec((tm, RW_WIDTH), row),
            pl.BlockSpec((tm, MLA_HEADS * KV_LORA), row),
            pl.BlockSpec((MLA_HEADS * KV_LORA, MLA_HEADS * V_HEAD), const),
            pl.BlockSpec((D_MODEL, D_MODEL), const),
        ],
        out_specs=pl.BlockSpec((tm, D_MODEL), row),
        out_shape=jax.ShapeDtypeStruct((T, D_MODEL), F32),
        compiler_params=_cparams("parallel"),
        name="out_proj",
    )(x2d, yrg, yrw, olat, wuv_bd, wout)


def _mem_kernel(x_ref, g_ref, wq_ref, mk_ref, mv_ref, wo_ref, o_ref):
    x = x_ref[...]
    hn = _rms(x, g_ref[...]).astype(BF)
    q = jnp.dot(hn, wq_ref[...], preferred_element_type=F32).astype(BF)
    mk = mk_ref[0]
    mv = mv_ref[0]
    outs = []
    for h in range(MEM_HEADS):
        sl = slice(h * MEM_HEAD, (h + 1) * MEM_HEAD)
        s = lax.dot_general(q[:, sl], mk[:, sl], _NT, preferred_element_type=F32) * MEM_SCALE
        e = jnp.exp(s - jnp.max(s, axis=-1, keepdims=True))
        pr = e / jnp.sum(e, axis=-1, keepdims=True)
        outs.append(jnp.dot(pr.astype(BF), mv[:, sl], preferred_element_type=F32).astype(BF))
    o = jnp.concatenate(outs, axis=-1)
    o_ref[...] = x + jnp.dot(o, wo_ref[...], preferred_element_type=F32)


def _mem_attn(x2d, B, S, g, wq, mk, mv, wo):
    T = B * S
    tm = min(ROW_TILE, S)
    nst = S // tm
    row = lambda i: (i, 0)
    const = lambda i: (0, 0)
    return pl.pallas_call(
        _mem_kernel,
        grid=(T // tm,),
        in_specs=[
            pl.BlockSpec((tm, D_MODEL), row),
            pl.BlockSpec((1, D_MODEL), const),
            pl.BlockSpec((D_MODEL, D_MODEL), const),
            pl.BlockSpec((1, N_MEM, D_MODEL), lambda i: (i // nst, 0, 0)),
            pl.BlockSpec((1, N_MEM, D_MODEL), lambda i: (i // nst, 0, 0)),
            pl.BlockSpec((D_MODEL, D_MODEL), const),
        ],
        out_specs=pl.BlockSpec((tm, D_MODEL), row),
        out_shape=jax.ShapeDtypeStruct((T, D_MODEL), F32),
        compiler_params=_cparams("parallel"),
        name="mem_attn",
    )(x2d, g, wq, mk, mv, wo)


def _ffn_kernel(x_ref, g_ref, w1_ref, w2_ref, gf_ref, o_ref, xn_ref, acc_ref, *, final):
    j = pl.program_id(1)

    @pl.when(j == 0)
    def _():
        xn_ref[...] = _rms(x_ref[...], g_ref[...]).astype(BF)
        acc_ref[...] = jnp.zeros(acc_ref.shape, F32)

    h = jnp.dot(xn_ref[...], w1_ref[...], preferred_element_type=F32)
    h = jnp.square(jnp.maximum(h, 0.0)).astype(BF)
    acc_ref[...] += jnp.dot(h, w2_ref[...], preferred_element_type=F32)

    @pl.when(j == pl.num_programs(1) - 1)
    def _():
        y = x_ref[...] + acc_ref[...]
        if final:
            y = _rms(y, gf_ref[...])
        o_ref[...] = y


def _ffn(x2d, g, w1, w2, gf, final):
    T = x2d.shape[0]
    tm = min(ROW_TILE, T)
    return pl.pallas_call(
        functools.partial(_ffn_kernel, final=final),
        grid=(T // tm, D_FF // FF_TILE),
        in_specs=[
            pl.BlockSpec((tm, D_MODEL), lambda i, j: (i, 0)),
            pl.BlockSpec((1, D_MODEL), lambda i, j: (0, 0)),
            pl.BlockSpec((D_MODEL, FF_TILE), lambda i, j: (0, j)),
            pl.BlockSpec((FF_TILE, D_MODEL), lambda i, j: (j, 0)),
            pl.BlockSpec((1, D_MODEL), lambda i, j: (0, 0)),
        ],
        out_specs=pl.BlockSpec((tm, D_MODEL), lambda i, j: (i, 0)),
        out_shape=jax.ShapeDtypeStruct((T, D_MODEL), F32),
        scratch_shapes=[pltpu.VMEM((tm, D_MODEL), BF), pltpu.VMEM((tm, D_MODEL), F32)],
        compiler_params=_cparams("parallel", "arbitrary"),
        name="ffn",
    )(x2d, g, w1, w2, gf)


def _memproj_kernel(m_ref, g_ref, w_ref, k_ref, v_ref, kb_ref, vb_ref):
    mn = _rms(m_ref[...], g_ref[...]).astype(BF)
    kv = jnp.dot(mn, w_ref[...], preferred_element_type=F32)
    k = kv[:, :D_MODEL]
    v = kv[:, D_MODEL:]
    k_ref[...] = k
    v_ref[...] = v
    kb_ref[...] = k.astype(BF)
    vb_ref[...] = v.astype(BF)


def _mem_project(mem2d, g, wkv):
    T = mem2d.shape[0]
    tm = min(ROW_TILE, T)
    row = lambda i: (i, 0)
    const = lambda i: (0, 0)
    return pl.pallas_call(
        _memproj_kernel,
        grid=(T // tm,),
        in_specs=[
            pl.BlockSpec((tm, D_MODEL), row),
            pl.BlockSpec((1, D_MODEL), const),
            pl.BlockSpec((D_MODEL, 2 * D_MODEL), const),
        ],
        out_specs=[pl.BlockSpec((tm, D_MODEL), row)] * 4,
        out_shape=[jax.ShapeDtypeStruct((T, D_MODEL), F32)] * 2
                  + [jax.ShapeDtypeStruct((T, D_MODEL), BF)] * 2,
        compiler_params=_cparams("parallel"),
        name="mem_project",
    )(mem2d, g, wkv)


def _block_diag(blocks):
    G, m, n = blocks.shape
    eye = jnp.eye(G, dtype=blocks.dtype)
    return jnp.einsum('gmn,gh->gmhn', blocks, eye).reshape(G * m, G * n)


def _prep_weights(l, W):
    r2 = lambda v: v.reshape(1, -1)
    w_in = W['w_in'][l]
    kr_cols = w_in[:, 1792:1824]
    kr_swap = jnp.concatenate([kr_cols[:, 16:], kr_cols[:, :16]], axis=1)
    w_ext = jnp.concatenate([w_in[:, :1792], jnp.tile(kr_cols, (1, 4)), jnp.tile(kr_swap, (1, 4))],
                            axis=1).astype(BF)
    w_uq = W['mla_w_uq'][l]
    rope = w_uq[:, :, QK_NOPE:]
    rope_swap = jnp.concatenate([rope[:, :, 16:], rope[:, :, :16]], axis=2)
    wq_ext = jnp.concatenate([w_uq[:, :, :QK_NOPE].reshape(Q_LORA, -1), rope.reshape(Q_LORA, -1),
                              rope_swap.reshape(Q_LORA, -1)], axis=1).astype(BF)
    wuk_bd = _block_diag(jnp.transpose(W['mla_w_uk'][l], (1, 2, 0))).astype(BF)
    wuv_bd = _block_diag(jnp.transpose(W['mla_w_uv'][l], (1, 0, 2))).astype(BF)
    zpad = lambda top, mat, bot: jnp.concatenate(
        [jnp.zeros((top, RW_WIDTH), F32), mat, jnp.zeros((bot, RW_WIDTH), F32)], axis=0)
    return dict(
        norm_mix=r2(W['norm_mix'][l]), w_ext=w_ext, gq=r2(W['mla_q_norm'][l]),
        gkv=r2(W['mla_kv_norm'][l]), wq_ext=wq_ext, wuk_bd=wuk_bd, wuv_bd=wuv_bd,
        cw=W['rg_conv_w'][l], cb=r2(W['rg_conv_b'][l]),
        wa_bd=_block_diag(W['rg_wa'][l]).astype(BF), ba=r2(W['rg_ba'][l]),
        wx_bd=_block_diag(W['rg_wx'][l]).astype(BF), bx=r2(W['rg_bx'][l]),
        lam=r2(W['rg_lambda'][l]),
        mu=r2(W['rw_mu'][l]), w0=r2(W['rw_w0'][l]), w2p=zpad(0, W['rw_w2'][l], 96),
        a0=r2(W['rw_a0'][l]), a2p=zpad(32, W['rw_a2'][l], 64), g2p=zpad(64, W['rw_g2'][l], 0),
        k_k=r2(W['rw_k_k'][l]), k_a=r2(W['rw_k_a'][l]), r_k=r2(W['rw_r_k'][l]),
        gn_w=r2(W['rw_gn_w'][l]), gn_b=r2(W['rw_gn_b'][l]),
        w_out=W['w_out'][l].astype(BF),
        norm_mem=r2(W['norm_mem'][l]), wq=W['mem_w_q'][l].reshape(D_MODEL, D_MODEL).astype(BF),
        wo=W['mem_w_o'][l].reshape(D_MODEL, D_MODEL).astype(BF),
        norm_ffn=r2(W['norm_ffn'][l]), w1=W['ffn_w1'][l].astype(BF), w2=W['ffn_w2'][l].astype(BF),
        norm_mem_kv=r2(W['norm_mem_kv'][l]),
        wkv=jnp.concatenate([W['mem_w_k'][l].reshape(D_MODEL, D_MODEL),
                             W['mem_w_v'][l].reshape(D_MODEL, D_MODEL)], axis=1).astype(BF),
    )


def _rope_tables(pos, rows):
    half = QK_ROPE // 2
    inv = ROPE_BASE ** (-jnp.arange(half, dtype=F32) / half)
    ang = pos.astype(F32)[:, None] * inv
    cos, sin = jnp.cos(ang), jnp.sin(ang)
    cosq = jnp.tile(jnp.concatenate([cos, cos], axis=1), (1, MLA_HEADS))
    sinq = jnp.tile(jnp.concatenate([-sin, sin], axis=1), (1, MLA_HEADS))
    reps = max(1, rows // pos.shape[0])
    return jnp.tile(cosq, (reps, 1)), jnp.tile(sinq, (reps, 1))


def _layer(x2d, B, S, P, tabs, rg_buf, rg_h, rw_shift, rw_state, past, mem_k, mem_v, gf, final):
    cosq, sinq = tabs
    pa, prw, ckv, kr, kcat, qcat = _in_proj(x2d, B, S, P['norm_mix'], P['w_ext'], P['gq'], P['gkv'],
                                            P['wq_ext'], P['wuk_bd'], cosq, sinq)
    buf8 = jnp.concatenate([jnp.zeros((B, 5, RG_WIDTH), F32), rg_buf], axis=1)
    yrg, buf_new, h_new = _rglru(pa.reshape(B, S, 512), buf8, rg_h.reshape(B, 1, RG_WIDTH),
                                 P['cw'], P['cb'], P['wa_bd'], P['ba'], P['wx_bd'], P['bx'], P['lam'])
    eye_h = jnp.eye(RW_HEADS, dtype=F32)
    state_bd = jnp.einsum('bhij,hg->bhigj', rw_state, eye_h).reshape(B, RW_WIDTH, RW_WIDTH)
    ones_bd = _block_diag(jnp.ones((RW_HEADS, RW_HEAD, RW_HEAD), F32)).astype(BF)
    yrw, shift_new, state_new = _rwkv(prw.reshape(B, S, RW_IN), rw_shift.reshape(B, 1, RW_IN), state_bd,
                                      P['mu'], P['w0'], P['w2p'], P['a0'], P['a2p'], P['g2p'],
                                      P['k_k'], P['k_a'], P['r_k'], P['gn_w'], P['gn_b'], ones_bd)
    wkv_new = jnp.einsum('bhigj,hg->bhij',
                         state_new.reshape(B, RW_HEADS, RW_HEAD, RW_HEADS, RW_HEAD), eye_h)
    kcat3 = kcat.reshape(B, S, 256)
    if past is None:
        olat = _attn_causal(qcat, kcat3)
    else:
        past_ckv, past_kr = past
        kpast = jnp.concatenate([past_ckv, jnp.tile(past_kr, (1, 1, 4))], axis=-1).astype(BF)
        olat = _attn_full(qcat, kpast, kcat3)
    x1 = _out_proj(x2d, yrg.reshape(B * S, RG_WIDTH), yrw.reshape(B * S, RW_WIDTH),
                   olat.reshape(B * S, MLA_HEADS * KV_LORA), P['wuv_bd'], P['w_out'])
    x2 = _mem_attn(x1, B, S, P['norm_mem'], P['wq'], mem_k, mem_v, P['wo'])
    x3 = _ffn(x2, P['norm_ffn'], P['w1'], P['w2'], gf, final)
    new_state = (buf_new[:, 5:8], h_new.reshape(B, RG_WIDTH), shift_new.reshape(B, RW_IN), wkv_new,
                 ckv.reshape(B, S, KV_LORA), kr.reshape(B, S, QK_ROPE))
    return x3, new_state


def kernel(x_prompt, x_sample, state_rg_conv, state_rg_h, state_rw_shift, state_rw_wkv, cache_mla_ckv, cache_mla_krope, cache_mem_k, cache_mem_v, mem_prompt, norm_mix, w_in, rg_conv_w, rg_conv_b, rg_wa, rg_ba, rg_wx, rg_bx, rg_lambda, rw_mu, rw_w0, rw_w2, rw_a0, rw_a2, rw_g2, rw_k_k, rw_k_a, rw_r_k, rw_gn_w, rw_gn_b, mla_q_norm, mla_kv_norm, mla_w_uq, mla_w_uk, mla_w_uv, w_out, norm_mem, norm_mem_kv, mem_w_q, mem_w_k, mem_w_v, mem_w_o, norm_ffn, ffn_w1, ffn_w2, norm_final):
    W = dict(norm_mix=norm_mix, w_in=w_in, rg_conv_w=rg_conv_w, rg_conv_b=rg_conv_b, rg_wa=rg_wa,
             rg_ba=rg_ba, rg_wx=rg_wx, rg_bx=rg_bx, rg_lambda=rg_lambda, rw_mu=rw_mu, rw_w0=rw_w0,
             rw_w2=rw_w2, rw_a0=rw_a0, rw_a2=rw_a2, rw_g2=rw_g2, rw_k_k=rw_k_k, rw_k_a=rw_k_a,
             rw_r_k=rw_r_k.reshape(rw_r_k.shape[0], RW_WIDTH), rw_gn_w=rw_gn_w, rw_gn_b=rw_gn_b,
             mla_q_norm=mla_q_norm, mla_kv_norm=mla_kv_norm, mla_w_uq=mla_w_uq, mla_w_uk=mla_w_uk,
             mla_w_uv=mla_w_uv, w_out=w_out, norm_mem=norm_mem, norm_mem_kv=norm_mem_kv,
             mem_w_q=mem_w_q, mem_w_k=mem_w_k, mem_w_v=mem_w_v, mem_w_o=mem_w_o, norm_ffn=norm_ffn,
             ffn_w1=ffn_w1, ffn_w2=ffn_w2)
    depth = norm_mix.shape[0]
    Bp, Sp, _ = x_prompt.shape
    Bs, Ss, _ = x_sample.shape
    past_len = cache_mla_ckv.shape[2]
    tabs_p = _rope_tables(jnp.arange(Sp, dtype=jnp.int32), min(ROW_TILE, Sp))
    tabs_s = _rope_tables(past_len + jnp.arange(Ss, dtype=jnp.int32), min(ROW_TILE, Ss))
    gf = norm_final.reshape(1, D_MODEL)

    xp = x_prompt.reshape(Bp * Sp, D_MODEL)
    xs = x_sample.reshape(Bs * Ss, D_MODEL)
    z = lambda *s: jnp.zeros(s, F32)
    p_states, s_states = [], []
    for l in range(depth):
        P = _prep_weights(l, W)
        final = l == depth - 1
        mk, mv, mkb, mvb = _mem_project(mem_prompt.reshape(Bp * N_MEM, D_MODEL), P['norm_mem_kv'], P['wkv'])
        xp, st_p = _layer(xp, Bp, Sp, P, tabs_p, z(Bp, 3, RG_WIDTH), z(Bp, RG_WIDTH), z(Bp, RW_IN),
                          z(Bp, RW_HEADS, RW_HEAD, RW_HEAD), None,
                          mkb.reshape(Bp, N_MEM, D_MODEL), mvb.reshape(Bp, N_MEM, D_MODEL), gf, final)
        xs, st_s = _layer(xs, Bs, Ss, P, tabs_s, state_rg_conv[l], state_rg_h[l], state_rw_shift[l],
                          state_rw_wkv[l], (cache_mla_ckv[l], cache_mla_krope[l]),
                          cache_mem_k[l].reshape(Bs, N_MEM, D_MODEL).astype(BF),
                          cache_mem_v[l].reshape(Bs, N_MEM, D_MODEL).astype(BF), gf, final)
        p_states.append(st_p + (mk.reshape(Bp, N_MEM, MEM_HEADS, MEM_HEAD),
                                mv.reshape(Bp, N_MEM, MEM_HEADS, MEM_HEAD)))
        s_states.append(st_s)

    sp = [jnp.stack(t) for t in zip(*p_states)]
    ss = [jnp.stack(t) for t in zip(*s_states)]
    return (xp.reshape(Bp, Sp, D_MODEL), xs.reshape(Bs, Ss, D_MODEL),
            sp[0], sp[1], sp[2], sp[3], sp[4], sp[5], sp[6], sp[7],
            ss[0], ss[1], ss[2], ss[3], ss[4], ss[5])
```

```python
import functools
import math

import jax
import jax.numpy as jnp
from jax import lax
from jax.experimental import pallas as pl
from jax.experimental.pallas import tpu as pltpu

BF = jnp.bfloat16
F32 = jnp.float32

D_MODEL = 1024
RG_WIDTH = 256
RG_BLOCKS = 4
CONV_W = 4
LRU_C = 8.0
RW_HEADS = 4
RW_HEAD = 64
RW_WIDTH = 256
RW_IN = 896
GN_EPS = 64e-5
L2_EPS = 1e-12
MLA_HEADS = 8
QK_NOPE = 64
QK_ROPE = 32
V_HEAD = 64
Q_LORA = 256
KV_LORA = 128
MLA_SCALE = (QK_NOPE + QK_ROPE) ** -0.5
SCALE_LOG2E = MLA_SCALE * math.log2(math.e)
ROPE_BASE = 10000.0
CHUNK = 64
N_MEM = 256
MEM_HEADS = 4
MEM_HEAD = 256
MEM_SCALE = MEM_HEAD ** -0.5
D_FF = 4096
EPS = 1e-6

LANES = 128
VMEM_LIMIT = 52 * 1024 * 1024

ROW_TILE = 512
RG_TILE = 512
RW_TILE = 512
RW_CHUNK = 64
RW_PASSES_GRAM = 1
RW_PASSES_INV = 1
RW_PASSES_LOCAL = 1
RW_PASSES_STATE = 1
ATT_TILE = 512
ROPE_LANE0 = 8
PV_ROWS = 144
FFN_ROW_TILE = 512
FF_TILE = 2048


def _cparams(*sem):
    return pltpu.CompilerParams(dimension_semantics=sem, vmem_limit_bytes=VMEM_LIMIT)


def _rms(x, g):
    return x * lax.rsqrt(jnp.mean(x * x, axis=-1, keepdims=True) + EPS) * g


def _dot(a, b):
    return jnp.dot(a.astype(BF), b.astype(BF), preferred_element_type=F32)


def _dot_nt(a, b):
    return lax.dot_general(a.astype(BF), b.astype(BF), (((1,), (1,)), ((), ())),
                           preferred_element_type=F32)


def _split(x):
    hi = x.astype(BF)
    lo = (x - hi.astype(F32)).astype(BF)
    return hi, lo


def _mm3(a, b, dims=(((1,), (0,)), ((), ())), passes=3):
    dg = functools.partial(lax.dot_general, dimension_numbers=dims, preferred_element_type=F32)
    if passes == 1:
        return dg(a.astype(BF), b.astype(BF))
    ah, al = _split(a)
    bh, bl = _split(b)
    return dg(ah, bh) + (dg(ah, bl) + dg(al, bh))


_NT = (((1,), (1,)), ((), ()))
_TN = (((0,), (0,)), ((), ()))


def _softplus(x):
    return jnp.maximum(x, 0.0) + jnp.log1p(jnp.exp(-jnp.abs(x)))


def _sigmoid(x):
    return 1.0 / (1.0 + jnp.exp(-x))


def _gelu_tanh(x):
    c = math.sqrt(2.0 / math.pi)
    return 0.5 * x * (1.0 + jnp.tanh(c * (x + 0.044715 * (x * x * x))))


def _shift_rows(x, d):
    return pltpu.roll(x, d, 0)


def _in_kernel(x_ref, g_ref, w_ref, gq_ref, gkv_ref, wq_ref, wuk_ref, cos_ref, sin_ref,
               pa_ref, prw_ref, ckv_ref, kr_ref, kcat_ref, qcat_ref):
    xn = _rms(x_ref[...], g_ref[...]).astype(BF)
    p = jnp.dot(xn, w_ref[...], preferred_element_type=F32)
    pa_ref[...] = p[:, :512]
    prw_ref[...] = p[:, 512:1408]
    cq = p[:, 1408:1664]
    ckv = p[:, 1664:1792]
    ka = p[:, 1792:1920]
    kb = p[:, 1920:2048]
    cos = cos_ref[...]
    sin = sin_ref[...]
    ckvn = _rms(ckv, gkv_ref[...])
    krt = ka * cos[:, :LANES] + kb * sin[:, :LANES]
    ckv_ref[...] = ckvn
    kr_ref[...] = krt[:, :QK_ROPE]
    lane = lax.broadcasted_iota(jnp.int32, (1, LANES), 1)
    rope_lanes = (lane >= ROPE_LANE0) & (lane < ROPE_LANE0 + QK_ROPE)
    kext = jnp.where(lane == 0, 1.0, jnp.where(rope_lanes, pltpu.roll(krt, ROPE_LANE0, 1), 0.0))
    kcat_ref[...] = jnp.concatenate([ckvn, kext], axis=-1).astype(BF)

    cqn = _rms(cq, gq_ref[...]).astype(BF)
    qq = jnp.dot(cqn, wq_ref[...], preferred_element_type=F32)
    rr = qq[:, 512:768] * cos + qq[:, 768:1024] * sin
    qlat = jnp.dot(qq[:, :512].astype(BF), wuk_ref[...], preferred_element_type=F32)
    for h in range(MLA_HEADS):
        g = h // 4
        shift = (ROPE_LANE0 - QK_ROPE * (h % 4)) % LANES
        rpart = jnp.where(rope_lanes, pltpu.roll(rr[:, g * LANES:(g + 1) * LANES], shift, 1), 0.0)
        qcat_ref[0, h, :, :LANES] = (qlat[:, h * LANES:(h + 1) * LANES] * SCALE_LOG2E).astype(BF)
        qcat_ref[0, h, :, LANES:] = (rpart * SCALE_LOG2E).astype(BF)


def _in_proj(x2d, B, S, g, w_ext, gq, gkv, wq_ext, wuk_bd, cosq, sinq):
    T = B * S
    tm = min(ROW_TILE, S)
    nst = S // tm
    ntab = cosq.shape[0] // tm
    row = lambda i: (i, 0)
    const = lambda i: (0, 0)
    outs = pl.pallas_call(
        _in_kernel,
        grid=(T // tm,),
        in_specs=[
            pl.BlockSpec((tm, D_MODEL), row),
            pl.BlockSpec((1, D_MODEL), const),
            pl.BlockSpec((D_MODEL, 2048), const),
            pl.BlockSpec((1, Q_LORA), const),
            pl.BlockSpec((1, KV_LORA), const),
            pl.BlockSpec((Q_LORA, 1024), const),
            pl.BlockSpec((512, 1024), const),
            pl.BlockSpec((tm, 256), lambda i: (i % ntab, 0)),
            pl.BlockSpec((tm, 256), lambda i: (i % ntab, 0)),
        ],
        out_specs=[
            pl.BlockSpec((tm, 512), row),
            pl.BlockSpec((tm, RW_IN), row),
            pl.BlockSpec((tm, KV_LORA), row),
            pl.BlockSpec((tm, QK_ROPE), row),
            pl.BlockSpec((tm, 256), row),
            pl.BlockSpec((1, MLA_HEADS, tm, 256), lambda i: (i // nst, 0, i % nst, 0)),
        ],
        out_shape=[
            jax.ShapeDtypeStruct((T, 512), F32),
            jax.ShapeDtypeStruct((T, RW_IN), F32),
            jax.ShapeDtypeStruct((T, KV_LORA), F32),
            jax.ShapeDtypeStruct((T, QK_ROPE), F32),
            jax.ShapeDtypeStruct((T, 256), BF),
            jax.ShapeDtypeStruct((B, MLA_HEADS, S, 256), BF),
        ],
        compiler_params=_cparams("parallel"),
        name="in_proj",
    )(x2d, g, w_ext, gq, gkv, wq_ext, wuk_bd, cosq, sinq)
    return outs


def _rglru_kernel(pa_ref, buf0_ref, h0_ref, cw_ref, cb_ref, wa_ref, ba_ref, wx_ref, bx_ref,
                  lam_ref, y_ref, buf_ref, hout_ref, ext_ref, h_ref, *, R):
    t = pl.program_id(1)

    @pl.when(t == 0)
    def _():
        ext_ref[0:8, :] = buf0_ref[0]
        h_ref[...] = h0_ref[0]

    pa = pa_ref[0]
    xr = pa[:, :RG_WIDTH]
    gate = pa[:, RG_WIDTH:]
    ext_ref[8:8 + R, :] = xr
    cw = cw_ref[...]
    conv = (cb_ref[...] + cw[3:4] * xr + cw[2:3] * ext_ref[7:7 + R, :]
            + cw[1:2] * ext_ref[6:6 + R, :] + cw[0:1] * ext_ref[5:5 + R, :])
    hist = ext_ref[R:R + 8, :]
    ext_ref[0:8, :] = hist
    buf_ref[0] = hist

    ra = _sigmoid(_dot(conv, wa_ref[...]) + ba_ref[...])
    ia = _sigmoid(_dot(conv, wx_ref[...]) + bx_ref[...])
    log_a = (-LRU_C) * ra * _softplus(-lam_ref[...])
    a = jnp.exp(log_a)
    b = jnp.sqrt(-jnp.tanh(log_a) * (a * a + 1.0)) * (ia * conv)

    rows = lax.broadcasted_iota(jnp.int32, (R, RG_WIDTH), 0)
    d = 1
    while d < R:
        m = rows >= d
        b = jnp.where(m, a * _shift_rows(b, d), 0.0) + b
        a = jnp.where(m, a * _shift_rows(a, d), a)
        d *= 2
    h = a * h_ref[...] + b
    hl = h[R - 1:R, :]
    h_ref[...] = hl
    hout_ref[0] = hl
    y_ref[0] = (_gelu_tanh(gate) * h).astype(BF)


def _rglru(pa, buf8, h0, cw, cb, wa_bd, ba, wx_bd, bx, lam):
    B, S, _ = pa.shape
    R = min(RG_TILE, S)
    vec = lambda b, t: (0, 0)
    return pl.pallas_call(
        functools.partial(_rglru_kernel, R=R),
        grid=(B, S // R),
        in_specs=[
            pl.BlockSpec((1, R, 512), lambda b, t: (b, t, 0)),
            pl.BlockSpec((1, 8, RG_WIDTH), lambda b, t: (b, 0, 0)),
            pl.BlockSpec((1, 1, RG_WIDTH), lambda b, t: (b, 0, 0)),
            pl.BlockSpec((CONV_W, RG_WIDTH), vec),
            pl.BlockSpec((1, RG_WIDTH), vec),
            pl.BlockSpec((RG_WIDTH, RG_WIDTH), vec),
            pl.BlockSpec((1, RG_WIDTH), vec),
            pl.BlockSpec((RG_WIDTH, RG_WIDTH), vec),
            pl.BlockSpec((1, RG_WIDTH), vec),
            pl.BlockSpec((1, RG_WIDTH), vec),
        ],
        out_specs=[
            pl.BlockSpec((1, R, RG_WIDTH), lambda b, t: (b, t, 0)),
            pl.BlockSpec((1, 8, RG_WIDTH), lambda b, t: (b, 0, 0)),
            pl.BlockSpec((1, 1, RG_WIDTH), lambda b, t: (b, 0, 0)),
        ],
        out_shape=[
            jax.ShapeDtypeStruct((B, S, RG_WIDTH), BF),
            jax.ShapeDtypeStruct((B, 8, RG_WIDTH), F32),
            jax.ShapeDtypeStruct((B, 1, RG_WIDTH), F32),
        ],
        scratch_shapes=[pltpu.VMEM((R + 8, RG_WIDTH), F32), pltpu.VMEM((1, RG_WIDTH), F32)],
        compiler_params=_cparams("parallel", "arbitrary"),
        name="rglru",
    )(pa, buf8, h0, cw, cb, wa_bd, ba, wx_bd, bx, lam)


def _rwkv_kernel(p_ref, shift0_ref, state0_ref, mu_ref, w0_ref, w2_ref, a0_ref, a2_ref, g2_ref,
                 kk_ref, ka_ref, rk_ref, gnw_ref, gnb_ref, ones_ref,
                 y_ref, shift_ref, state_ref, prev_ref, s_ref, *, R, C):
    t = pl.program_id(1)
    W = RW_WIDTH
    HC = RW_HEADS * C

    @pl.when(t == 0)
    def _():
        prev_ref[...] = shift0_ref[0]
        s_ref[...] = state0_ref[0]

    p = p_ref[0]
    rows = lax.broadcasted_iota(jnp.int32, (R, RW_IN), 0)
    prev = jnp.where(rows == 0, prev_ref[...], _shift_rows(p, 1))
    last = p[R - 1:R, :]
    prev_ref[...] = last
    shift_ref[0] = last
    xs = p + mu_ref[...] * (prev - p)
    r = xs[:, 0:W]
    k = xs[:, W:2 * W]
    v = xs[:, 2 * W:3 * W]
    x4 = xs[:, 3 * W:]
    ones_bd = ones_ref[...]

    def segsum(x):
        hi = x.astype(BF)
        r1 = x - hi.astype(F32)
        mid = r1.astype(BF)
        lo = (r1 - mid.astype(F32)).astype(BF)
        dg = functools.partial(jnp.dot, preferred_element_type=F32)
        return dg(hi, ones_bd) + (dg(mid, ones_bd) + dg(lo, ones_bd))

    log_w = -_softplus(-(w0_ref[...] + _mm3(jnp.tanh(x4), w2_ref[...]))) - 0.5
    logw = -jnp.exp(log_w)
    a = _sigmoid(a0_ref[...] + _mm3(x4, a2_ref[...]))
    g = _mm3(_sigmoid(x4), g2_ref[...])
    kk = k * kk_ref[...]
    kk = kk * lax.rsqrt(segsum(kk * kk) + L2_EPS)
    k = k * (1.0 + (a - 1.0) * ka_ref[...])
    bonus = segsum(r * k * rk_ref[...]) * v
    kka = kk * a

    lane_head = lax.broadcasted_iota(jnp.int32, (1, W), 1) // RW_HEAD
    hmask = [lane_head == h for h in range(RW_HEADS)]

    def stack(x):
        return jnp.concatenate([jnp.where(hmask[h], x, 0.0) for h in range(RW_HEADS)], axis=0)

    ri = lax.broadcasted_iota(jnp.int32, (HC, HC), 0) % C
    ci = lax.broadcasted_iota(jnp.int32, (HC, HC), 1) % C
    strict = ri > ci
    incl = ri >= ci
    eye_hc = lax.broadcasted_iota(jnp.int32, (HC, HC), 0) == lax.broadcasted_iota(jnp.int32, (HC, HC), 1)
    eye_w = lax.broadcasted_iota(jnp.int32, (W, W), 0) == lax.broadcasted_iota(jnp.int32, (W, W), 1)
    crow = lax.broadcasted_iota(jnp.int32, (R, W), 0) % C

    mm_gram = functools.partial(_mm3, passes=RW_PASSES_GRAM)
    mm_inv = functools.partial(_mm3, passes=RW_PASSES_INV)
    mm_loc = functools.partial(_mm3, passes=RW_PASSES_LOCAL)
    mm_state = functools.partial(_mm3, passes=RW_PASSES_STATE)

    cum = logw
    d = 1
    while d < C:
        cum = cum + jnp.where(crow >= d, _shift_rows(cum, d), 0.0)
        d *= 2
    nch = R // C
    chunks = range(nch)
    sls = [slice(c * C, (c + 1) * C) for c in chunks]
    ends = [cum[(c + 1) * C - 1:(c + 1) * C, :] for c in chunks]
    cl = jnp.concatenate([jnp.broadcast_to(e, (C, W)) for e in ends], axis=0) if nch > 1 \
        else jnp.broadcast_to(ends[0], (C, W))
    w_inc = jnp.exp(cum)
    w_inv = jnp.exp(-cum)
    w_rem = jnp.exp(cl - cum)
    t_a = jnp.exp(cum - logw) * kk
    t_b = kka * w_inv
    t_k = k * w_inv
    t_r = r * w_inc
    t_kt = k * w_rem
    t_bt = kka * w_rem

    a_s = [stack(t_a[sl]) for sl in sls]
    r_s = [stack(t_r[sl]) for sl in sls]
    v_s = [stack(v[sl]) for sl in sls]
    kt_s = [stack(t_kt[sl]) for sl in sls]
    bt_s = [stack(t_bt[sl]) for sl in sls]
    bk = [jnp.concatenate([stack(t_b[sl]), stack(t_k[sl])], axis=0) for sl in sls]
    g_a = [mm_gram(a_s[c], bk[c], _NT) for c in chunks]
    g_r = [mm_gram(r_s[c], bk[c], _NT) for c in chunks]
    n = [jnp.where(strict, -g[:, :HC], 0.0) for g in g_a]
    l_k = [jnp.where(strict, g[:, HC:], 0.0) for g in g_a]
    m_b = [jnp.where(incl, g[:, :HC], 0.0) for g in g_r]
    m_k = [jnp.where(incl, g[:, HC:], 0.0) for g in g_r]

    tinv = [jnp.where(eye_hc, 1.0, 0.0) + x for x in n]
    pw = n
    d = 2
    while d < C:
        pw = [mm_inv(x, x) for x in pw]
        tinv = [tinv[c] + mm_inv(tinv[c], pw[c]) for c in chunks]
        d *= 2

    x1 = [mm_loc(l_k[c], v_s[c]) for c in chunks]
    tu = [mm_loc(tinv[c], jnp.concatenate([x1[c], a_s[c]], axis=1)) for c in chunks]
    mb = [mm_loc(m_b[c], tu[c]) for c in chunks]
    y_loc = [mm_loc(m_k[c], v_s[c]) - mb[c][:, :W] for c in chunks]
    r_t = [r_s[c] - mb[c][:, W:] for c in chunks]
    p_m = [jnp.where(eye_w, jnp.exp(ends[c]), 0.0) - mm_loc(tu[c][:, W:], bt_s[c], _TN) for c in chunks]
    q_m = [mm_loc(v_s[c], kt_s[c], _TN) - mm_loc(tu[c][:, :W], bt_s[c], _TN) for c in chunks]

    st = s_ref[...]
    ys = []
    for c in chunks:
        y_st = mm_state(r_t[c], st, _NT) + y_loc[c]
        yc = y_st[0:C]
        for h in range(1, RW_HEADS):
            yc = yc + y_st[h * C:(h + 1) * C]
        ys.append(yc)
        st = mm_state(st, p_m[c]) + q_m[c]
    s_ref[...] = st

    y = jnp.concatenate(ys, axis=0) if len(ys) > 1 else ys[0]
    state_ref[0] = s_ref[...]
    inv_n = 1.0 / RW_HEAD
    mean = segsum(y) * inv_n
    yc = y - mean
    var = segsum(yc * yc) * inv_n
    yn = yc * lax.rsqrt(var + GN_EPS) * gnw_ref[...] + gnb_ref[...]
    y_ref[0] = ((yn + bonus) * g).astype(BF)


def _rwkv(prw, shift0, state_bd, mu, w0, w2p, a0, a2p, g2p, k_k, k_a, r_k, gn_w, gn_b, ones_bd):
    B, S, _ = prw.shape
    R = min(RW_TILE, S)
    C = min(RW_CHUNK, S)
    vec = lambda b, t: (0, 0)
    wv = pl.BlockSpec((1, RW_WIDTH), vec)
    lora = pl.BlockSpec((LANES, RW_WIDTH), vec)
    return pl.pallas_call(
        functools.partial(_rwkv_kernel, R=R, C=C),
        grid=(B, S // R),
        in_specs=[
            pl.BlockSpec((1, R, RW_IN), lambda b, t: (b, t, 0)),
            pl.BlockSpec((1, 1, RW_IN), lambda b, t: (b, 0, 0)),
            pl.BlockSpec((1, RW_WIDTH, RW_WIDTH), lambda b, t: (b, 0, 0)),
            pl.BlockSpec((1, RW_IN), vec),
            wv, lora, wv, lora, lora, wv, wv, wv, wv, wv,
            pl.BlockSpec((RW_WIDTH, RW_WIDTH), vec),
        ],
        out_specs=[
            pl.BlockSpec((1, R, RW_WIDTH), lambda b, t: (b, t, 0)),
            pl.BlockSpec((1, 1, RW_IN), lambda b, t: (b, 0, 0)),
            pl.BlockSpec((1, RW_WIDTH, RW_WIDTH), lambda b, t: (b, 0, 0)),
        ],
        out_shape=[
            jax.ShapeDtypeStruct((B, S, RW_WIDTH), BF),
            jax.ShapeDtypeStruct((B, 1, RW_IN), F32),
            jax.ShapeDtypeStruct((B, RW_WIDTH, RW_WIDTH), F32),
        ],
        scratch_shapes=[pltpu.VMEM((1, RW_IN), F32), pltpu.VMEM((RW_WIDTH, RW_WIDTH), F32)],
        compiler_params=_cparams("parallel", "arbitrary"),
        name="rwkv7",
    )(prw, shift0, state_bd, mu, w0, w2p, a0, a2p, g2p, k_k, k_a, r_k, gn_w, gn_b, ones_bd)


def _attn_causal_kernel(q_ref, k_ref, o_ref, m_ref, acc_ref, *, tq):
    qi = pl.program_id(1)
    q = q_ref[0].reshape(MLA_HEADS * tq, 256)
    m_ref[...] = jnp.full(m_ref.shape, -jnp.inf, F32)
    acc_ref[...] = jnp.zeros(acc_ref.shape, F32)

    def step(kblk, masked):
        s = lax.dot_general(kblk, q, _NT, preferred_element_type=F32)
        if masked:
            kc = lax.broadcasted_iota(jnp.int32, s.shape, 0) // CHUNK
            qc = (lax.broadcasted_iota(jnp.int32, s.shape, 1) % tq) // CHUNK
            s = jnp.where(kc <= qc, s, -jnp.inf)
        m_prev = m_ref[...]
        m_new = jnp.maximum(m_prev, jnp.max(s, axis=0, keepdims=True))
        alpha = jnp.exp2(m_prev - m_new)
        p = jnp.exp2(s - m_new).astype(BF)
        pv = lax.dot_general(kblk[:, :PV_ROWS], p, _TN, preferred_element_type=F32)
        acc_ref[...] = alpha * acc_ref[...] + pv
        m_ref[...] = m_new

    def body(ki, carry):
        step(k_ref[0, pl.ds(pl.multiple_of(ki * tq, tq), tq), :], False)
        return carry

    lax.fori_loop(0, qi, body, 0)
    step(k_ref[0, pl.ds(pl.multiple_of(qi * tq, tq), tq), :], True)
    o = acc_ref[0:KV_LORA, :] / acc_ref[KV_LORA:KV_LORA + 1, :]
    for h in range(MLA_HEADS):
        o_ref[0, :, h * KV_LORA:(h + 1) * KV_LORA] = o[:, h * tq:(h + 1) * tq].T.astype(BF)


def _attn_causal(qcat, kcat):
    B, H, S, _ = qcat.shape
    tq = min(ATT_TILE, S)
    return pl.pallas_call(
        functools.partial(_attn_causal_kernel, tq=tq),
        grid=(B, S // tq),
        in_specs=[
            pl.BlockSpec((1, H, tq, 256), lambda b, i: (b, 0, i, 0)),
            pl.BlockSpec((1, S, 256), lambda b, i: (b, 0, 0)),
        ],
        out_specs=pl.BlockSpec((1, tq, H * KV_LORA), lambda b, i: (b, i, 0)),
        out_shape=jax.ShapeDtypeStruct((B, S, H * KV_LORA), BF),
        scratch_shapes=[pltpu.VMEM((1, H * tq), F32), pltpu.VMEM((PV_ROWS, H * tq), F32)],
        compiler_params=_cparams("parallel", "arbitrary"),
        name="mla_causal",
    )(qcat, kcat)


def _attn_full_kernel(q_ref, kp_ref, kn_ref, o_ref, *, S):
    q = q_ref[0].reshape(MLA_HEADS * S, 256)
    kp = kp_ref[0]
    kn = kn_ref[0]
    s1 = lax.dot_general(q, kp, _NT, preferred_element_type=F32)
    s2 = lax.dot_general(q, kn, _NT, preferred_element_type=F32)
    m = jnp.maximum(jnp.max(s1, axis=-1, keepdims=True), jnp.max(s2, axis=-1, keepdims=True))
    p1 = jnp.exp2(s1 - m)
    p2 = jnp.exp2(s2 - m)
    l = jnp.sum(p1, axis=-1, keepdims=True) + jnp.sum(p2, axis=-1, keepdims=True)
    o = (jnp.dot(p1.astype(BF), kp[:, :KV_LORA], preferred_element_type=F32)
         + jnp.dot(p2.astype(BF), kn[:, :KV_LORA], preferred_element_type=F32)) / l
    for h in range(MLA_HEADS):
        o_ref[0, :, h * KV_LORA:(h + 1) * KV_LORA] = o[h * S:(h + 1) * S].astype(BF)


def _attn_full(qcat, kpast, knew):
    B, H, S, _ = qcat.shape
    P = kpast.shape[1]
    return pl.pallas_call(
        functools.partial(_attn_full_kernel, S=S),
        grid=(B,),
        in_specs=[
            pl.BlockSpec((1, H, S, 256), lambda b: (b, 0, 0, 0)),
            pl.BlockSpec((1, P, 256), lambda b: (b, 0, 0)),
            pl.BlockSpec((1, S, 256), lambda b: (b, 0, 0)),
        ],
        out_specs=pl.BlockSpec((1, S, H * KV_LORA), lambda b: (b, 0, 0)),
        out_shape=jax.ShapeDtypeStruct((B, S, H * KV_LORA), BF),
        compiler_params=_cparams("parallel"),
        name="mla_full",
    )(qcat, kpast, knew)


def _out_kernel(x_ref, yrg_ref, yrw_ref, ol_ref, wuv_ref, wout_ref, o_ref):
    ymla = jnp.dot(ol_ref[...], wuv_ref[...], preferred_element_type=F32)
    ycat = jnp.concatenate([yrg_ref[...], yrw_ref[...], ymla.astype(BF)], axis=-1)
    o_ref[...] = x_ref[...] + jnp.dot(ycat, wout_ref[...], preferred_element_type=F32)


def _out_proj(x2d, yrg, yrw, olat, wuv_bd, wout):
    T = x2d.shape[0]
    tm = min(ROW_TILE, T)
    row = lambda i: (i, 0)
    const = lambda i: (0, 0)
    return pl.pallas_call(
        _out_kernel,
        grid=(T // tm,),
        in_specs=[
            pl.BlockSpec((tm, D_MODEL), row),
            pl.BlockSpec((tm, RG_WIDTH), row),
            pl.BlockSpec((tm, RW_WIDTH), row),
            pl.BlockSpec((tm, MLA_HEADS * KV_LORA), row),
            pl.BlockSpec((MLA_HEADS * KV_LORA, MLA_HEADS * V_HEAD), const),
            pl.BlockSpec((D_MODEL, D_MODEL), const),
        ],
        out_specs=pl.BlockSpec((tm, D_MODEL), row),
        out_shape=jax.ShapeDtypeStruct((T, D_MODEL), F32),
        compiler_params=_cparams("parallel"),
        name="out_proj",
    )(x2d, yrg, yrw, olat, wuv_bd, wout)


def _mem_kernel(x_ref, g_ref, wq_ref, mk_ref, mv_ref, wo_ref, o_ref):
    x = x_ref[...]
    hn = _rms(x, g_ref[...]).astype(BF)
    q = jnp.dot(hn, wq_ref[...], preferred_element_type=F32).astype(BF)
    mk = mk_ref[0]
    mv = mv_ref[0]
    outs = []
    for h in range(MEM_HEADS):
        sl = slice(h * MEM_HEAD, (h + 1) * MEM_HEAD)
        s = lax.dot_general(q[:, sl], mk[:, sl], _NT, preferred_element_type=F32) * MEM_SCALE
        e = jnp.exp(s - jnp.max(s, axis=-1, keepdims=True))
        pr = e / jnp.sum(e, axis=-1, keepdims=True)
        outs.append(jnp.dot(pr.astype(BF), mv[:, sl], preferred_element_type=F32).astype(BF))
    o = jnp.concatenate(outs, axis=-1)
    o_ref[...] = x + jnp.dot(o, wo_ref[...], preferred_element_type=F32)


def _mem_attn(x2d, B, S, g, wq, mk, mv, wo):
    T = B * S
    tm = min(ROW_TILE, S)
    nst = S // tm
    row = lambda i: (i, 0)
    const = lambda i: (0, 0)
    return pl.pallas_call(
        _mem_kernel,
        grid=(T // tm,),
        in_specs=[
            pl.BlockSpec((tm, D_MODEL), row),
            pl.BlockSpec((1, D_MODEL), const),
            pl.BlockSpec((D_MODEL, D_MODEL), const),
            pl.BlockSpec((1, N_MEM, D_MODEL), lambda i: (i // nst, 0, 0)),
            pl.BlockSpec((1, N_MEM, D_MODEL), lambda i: (i // nst, 0, 0)),
            pl.BlockSpec((D_MODEL, D_MODEL), const),
        ],
        out_specs=pl.BlockSpec((tm, D_MODEL), row),
        out_shape=jax.ShapeDtypeStruct((T, D_MODEL), F32),
        compiler_params=_cparams("parallel"),
        name="mem_attn",
    )(x2d, g, wq, mk, mv, wo)


def _ffn_kernel(x_ref, g_ref, w1_ref, w2_ref, gf_ref, o_ref, xn_ref, acc_ref, *, final):
    j = pl.program_id(1)

    @pl.when(j == 0)
    def _():
        xn_ref[...] = _rms(x_ref[...], g_ref[...]).astype(BF)
        acc_ref[...] = jnp.zeros(acc_ref.shape, F32)

    h = jnp.dot(xn_ref[...], w1_ref[...], preferred_element_type=F32)
    h = jnp.square(jnp.maximum(h, 0.0)).astype(BF)
    acc_ref[...] += jnp.dot(h, w2_ref[...], preferred_element_type=F32)

    @pl.when(j == pl.num_programs(1) - 1)
    def _():
        y = x_ref[...] + acc_ref[...]
        if final:
            y = _rms(y, gf_ref[...])
        o_ref[...] = y


def _ffn(x2d, g, w1, w2, gf, final):
    T = x2d.shape[0]
    tm = min(FFN_ROW_TILE, T)
    return pl.pallas_call(
        functools.partial(_ffn_kernel, final=final),
        grid=(T // tm, D_FF // FF_TILE),
        in_specs=[
            pl.BlockSpec((tm, D_MODEL), lambda i, j: (i, 0)),
            pl.BlockSpec((1, D_MODEL), lambda i, j: (0, 0)),
            pl.BlockSpec((D_MODEL, FF_TILE), lambda i, j: (0, j)),
            pl.BlockSpec((FF_TILE, D_MODEL), lambda i, j: (j, 0)),
            pl.BlockSpec((1, D_MODEL), lambda i, j: (0, 0)),
        ],
        out_specs=pl.BlockSpec((tm, D_MODEL), lambda i, j: (i, 0)),
        out_shape=jax.ShapeDtypeStruct((T, D_MODEL), F32),
        scratch_shapes=[pltpu.VMEM((tm, D_MODEL), BF), pltpu.VMEM((tm, D_MODEL), F32)],
        compiler_params=_cparams("parallel", "arbitrary"),
        name="ffn",
    )(x2d, g, w1, w2, gf)


def _memproj_kernel(m_ref, g_ref, w_ref, k_ref, v_ref, kb_ref, vb_ref):
    mn = _rms(m_ref[...], g_ref[...]).astype(BF)
    kv = jnp.dot(mn, w_ref[...], preferred_element_type=F32)
    k = kv[:, :D_MODEL]
    v = kv[:, D_MODEL:]
    k_ref[...] = k
    v_ref[...] = v
    kb_ref[...] = k.astype(BF)
    vb_ref[...] = v.astype(BF)


def _mem_project(mem2d, g, wkv):
    T = mem2d.shape[0]
    tm = min(ROW_TILE, T)
    row = lambda i: (i, 0)
    const = lambda i: (0, 0)
    return pl.pallas_call(
        _memproj_kernel,
        grid=(T // tm,),
        in_specs=[
            pl.BlockSpec((tm, D_MODEL), row),
            pl.BlockSpec((1, D_MODEL), const),
            pl.BlockSpec((D_MODEL, 2 * D_MODEL), const),
        ],
        out_specs=[pl.BlockSpec((tm, D_MODEL), row)] * 4,
        out_shape=[jax.ShapeDtypeStruct((T, D_MODEL), F32)] * 2
                  + [jax.ShapeDtypeStruct((T, D_MODEL), BF)] * 2,
        compiler_params=_cparams("parallel"),
        name="mem_project",
    )(mem2d, g, wkv)


def _block_diag(blocks):
    G, m, n = blocks.shape
    eye = jnp.eye(G, dtype=blocks.dtype)
    return jnp.einsum('gmn,gh->gmhn', blocks, eye).reshape(G * m, G * n)


def _prep_weights(l, W):
    r2 = lambda v: v.reshape(1, -1)
    w_in = W['w_in'][l]
    kr_cols = w_in[:, 1792:1824]
    kr_swap = jnp.concatenate([kr_cols[:, 16:], kr_cols[:, :16]], axis=1)
    w_ext = jnp.concatenate([w_in[:, :1792], jnp.tile(kr_cols, (1, 4)), jnp.tile(kr_swap, (1, 4))],
                            axis=1).astype(BF)
    w_uq = W['mla_w_uq'][l]
    rope = w_uq[:, :, QK_NOPE:]
    rope_swap = jnp.concatenate([rope[:, :, 16:], rope[:, :, :16]], axis=2)
    wq_ext = jnp.concatenate([w_uq[:, :, :QK_NOPE].reshape(Q_LORA, -1), rope.reshape(Q_LORA, -1),
                              rope_swap.reshape(Q_LORA, -1)], axis=1).astype(BF)
    wuk_bd = _block_diag(jnp.transpose(W['mla_w_uk'][l], (1, 2, 0))).astype(BF)
    wuv_bd = _block_diag(jnp.transpose(W['mla_w_uv'][l], (1, 0, 2))).astype(BF)
    zpad = lambda top, mat, bot: jnp.concatenate(
        [jnp.zeros((top, RW_WIDTH), F32), mat, jnp.zeros((bot, RW_WIDTH), F32)], axis=0)
    return dict(
        norm_mix=r2(W['norm_mix'][l]), w_ext=w_ext, gq=r2(W['mla_q_norm'][l]),
        gkv=r2(W['mla_kv_norm'][l]), wq_ext=wq_ext, wuk_bd=wuk_bd, wuv_bd=wuv_bd,
        cw=W['rg_conv_w'][l], cb=r2(W['rg_conv_b'][l]),
        wa_bd=_block_diag(W['rg_wa'][l]).astype(BF), ba=r2(W['rg_ba'][l]),
        wx_bd=_block_diag(W['rg_wx'][l]).astype(BF), bx=r2(W['rg_bx'][l]),
        lam=r2(W['rg_lambda'][l]),
        mu=r2(W['rw_mu'][l]), w0=r2(W['rw_w0'][l]), w2p=zpad(0, W['rw_w2'][l], 96),
        a0=r2(W['rw_a0'][l]), a2p=zpad(32, W['rw_a2'][l], 64), g2p=zpad(64, W['rw_g2'][l], 0),
        k_k=r2(W['rw_k_k'][l]), k_a=r2(W['rw_k_a'][l]), r_k=r2(W['rw_r_k'][l]),
        gn_w=r2(W['rw_gn_w'][l]), gn_b=r2(W['rw_gn_b'][l]),
        w_out=W['w_out'][l].astype(BF),
        norm_mem=r2(W['norm_mem'][l]), wq=W['mem_w_q'][l].reshape(D_MODEL, D_MODEL).astype(BF),
        wo=W['mem_w_o'][l].reshape(D_MODEL, D_MODEL).astype(BF),
        norm_ffn=r2(W['norm_ffn'][l]), w1=W['ffn_w1'][l].astype(BF), w2=W['ffn_w2'][l].astype(BF),
        norm_mem_kv=r2(W['norm_mem_kv'][l]),
        wkv=jnp.concatenate([W['mem_w_k'][l].reshape(D_MODEL, D_MODEL),
                             W['mem_w_v'][l].reshape(D_MODEL, D_MODEL)], axis=1).astype(BF),
    )


def _rope_tables(pos, rows):
    half = QK_ROPE // 2
    inv = ROPE_BASE ** (-jnp.arange(half, dtype=F32) / half)
    ang = pos.astype(F32)[:, None] * inv
    cos, sin = jnp.cos(ang), jnp.sin(ang)
    cosq = jnp.tile(jnp.concatenate([cos, cos], axis=1), (1, MLA_HEADS))
    sinq = jnp.tile(jnp.concatenate([-sin, sin], axis=1), (1, MLA_HEADS))
    reps = max(1, rows // pos.shape[0])
    return jnp.tile(cosq, (reps, 1)), jnp.tile(sinq, (reps, 1))


def _layer(x2d, B, S, P, tabs, rg_buf, rg_h, rw_shift, rw_state, past, mem_k, mem_v, gf, final):
    cosq, sinq = tabs
    pa, prw, ckv, kr, kcat, qcat = _in_proj(x2d, B, S, P['norm_mix'], P['w_ext'], P['gq'], P['gkv'],
                                            P['wq_ext'], P['wuk_bd'], cosq, sinq)
    buf8 = jnp.concatenate([jnp.zeros((B, 5, RG_WIDTH), F32), rg_buf], axis=1)
    yrg, buf_new, h_new = _rglru(pa.reshape(B, S, 512), buf8, rg_h.reshape(B, 1, RG_WIDTH),
                                 P['cw'], P['cb'], P['wa_bd'], P['ba'], P['wx_bd'], P['bx'], P['lam'])
    eye_h = jnp.eye(RW_HEADS, dtype=F32)
    state_bd = jnp.einsum('bhij,hg->bhigj', rw_state, eye_h).reshape(B, RW_WIDTH, RW_WIDTH)
    ones_bd = _block_diag(jnp.ones((RW_HEADS, RW_HEAD, RW_HEAD), F32)).astype(BF)
    yrw, shift_new, state_new = _rwkv(prw.reshape(B, S, RW_IN), rw_shift.reshape(B, 1, RW_IN), state_bd,
                                      P['mu'], P['w0'], P['w2p'], P['a0'], P['a2p'], P['g2p'],
                                      P['k_k'], P['k_a'], P['r_k'], P['gn_w'], P['gn_b'], ones_bd)
    wkv_new = jnp.einsum('bhigj,hg->bhij',
                         state_new.reshape(B, RW_HEADS, RW_HEAD, RW_HEADS, RW_HEAD), eye_h)
    kcat3 = kcat.reshape(B, S, 256)
    if past is None:
        olat = _attn_causal(qcat, kcat3)
    else:
        past_ckv, past_kr = past
        pshape = past_kr.shape[:2]
        kpast = jnp.concatenate(
            [past_ckv, jnp.ones(pshape + (1,), F32), jnp.zeros(pshape + (ROPE_LANE0 - 1,), F32),
             past_kr, jnp.zeros(pshape + (LANES - ROPE_LANE0 - QK_ROPE,), F32)], axis=-1).astype(BF)
        olat = _attn_full(qcat, kpast, kcat3)
    x1 = _out_proj(x2d, yrg.reshape(B * S, RG_WIDTH), yrw.reshape(B * S, RW_WIDTH),
                   olat.reshape(B * S, MLA_HEADS * KV_LORA), P['wuv_bd'], P['w_out'])
    x2 = _mem_attn(x1, B, S, P['norm_mem'], P['wq'], mem_k, mem_v, P['wo'])
    x3 = _ffn(x2, P['norm_ffn'], P['w1'], P['w2'], gf, final)
    new_state = (buf_new[:, 5:8], h_new.reshape(B, RG_WIDTH), shift_new.reshape(B, RW_IN), wkv_new,
                 ckv.reshape(B, S, KV_LORA), kr.reshape(B, S, QK_ROPE))
    return x3, new_state


def kernel(x_prompt, x_sample, state_rg_conv, state_rg_h, state_rw_shift, state_rw_wkv, cache_mla_ckv, cache_mla_krope, cache_mem_k, cache_mem_v, mem_prompt, norm_mix, w_in, rg_conv_w, rg_conv_b, rg_wa, rg_ba, rg_wx, rg_bx, rg_lambda, rw_mu, rw_w0, rw_w2, rw_a0, rw_a2, rw_g2, rw_k_k, rw_k_a, rw_r_k, rw_gn_w, rw_gn_b, mla_q_norm, mla_kv_norm, mla_w_uq, mla_w_uk, mla_w_uv, w_out, norm_mem, norm_mem_kv, mem_w_q, mem_w_k, mem_w_v, mem_w_o, norm_ffn, ffn_w1, ffn_w2, norm_final):
    W = dict(norm_mix=norm_mix, w_in=w_in, rg_conv_w=rg_conv_w, rg_conv_b=rg_conv_b, rg_wa=rg_wa,
             rg_ba=rg_ba, rg_wx=rg_wx, rg_bx=rg_bx, rg_lambda=rg_lambda, rw_mu=rw_mu, rw_w0=rw_w0,
             rw_w2=rw_w2, rw_a0=rw_a0, rw_a2=rw_a2, rw_g2=rw_g2, rw_k_k=rw_k_k, rw_k_a=rw_k_a,
             rw_r_k=rw_r_k.reshape(rw_r_k.shape[0], RW_WIDTH), rw_gn_w=rw_gn_w, rw_gn_b=rw_gn_b,
             mla_q_norm=mla_q_norm, mla_kv_norm=mla_kv_norm, mla_w_uq=mla_w_uq, mla_w_uk=mla_w_uk,
             mla_w_uv=mla_w_uv, w_out=w_out, norm_mem=norm_mem, norm_mem_kv=norm_mem_kv,
             mem_w_q=mem_w_q, mem_w_k=mem_w_k, mem_w_v=mem_w_v, mem_w_o=mem_w_o, norm_ffn=norm_ffn,
             ffn_w1=ffn_w1, ffn_w2=ffn_w2)
    depth = norm_mix.shape[0]
    Bp, Sp, _ = x_prompt.shape
    Bs, Ss, _ = x_sample.shape
    past_len = cache_mla_ckv.shape[2]
    tabs_p = _rope_tables(jnp.arange(Sp, dtype=jnp.int32), min(ROW_TILE, Sp))
    tabs_s = _rope_tables(past_len + jnp.arange(Ss, dtype=jnp.int32), min(ROW_TILE, Ss))
    gf = norm_final.reshape(1, D_MODEL)

    xp = x_prompt.reshape(Bp * Sp, D_MODEL)
    xs = x_sample.reshape(Bs * Ss, D_MODEL)
    z = lambda *s: jnp.zeros(s, F32)
    p_states, s_states = [], []
    for l in range(depth):
        P = _prep_weights(l, W)
        final = l == depth - 1
        mk, mv, mkb, mvb = _mem_project(mem_prompt.reshape(Bp * N_MEM, D_MODEL), P['norm_mem_kv'], P['wkv'])
        xp, st_p = _layer(xp, Bp, Sp, P, tabs_p, z(Bp, 3, RG_WIDTH), z(Bp, RG_WIDTH), z(Bp, RW_IN),
                          z(Bp, RW_HEADS, RW_HEAD, RW_HEAD), None,
                          mkb.reshape(Bp, N_MEM, D_MODEL), mvb.reshape(Bp, N_MEM, D_MODEL), gf, final)
        xs, st_s = _layer(xs, Bs, Ss, P, tabs_s, state_rg_conv[l], state_rg_h[l], state_rw_shift[l],
                          state_rw_wkv[l], (cache_mla_ckv[l], cache_mla_krope[l]),
                          cache_mem_k[l].reshape(Bs, N_MEM, D_MODEL).astype(BF),
                          cache_mem_v[l].reshape(Bs, N_MEM, D_MODEL).astype(BF), gf, final)
        p_states.append(st_p + (mk.reshape(Bp, N_MEM, MEM_HEADS, MEM_HEAD),
                                mv.reshape(Bp, N_MEM, MEM_HEADS, MEM_HEAD)))
        s_states.append(st_s)

    sp = [jnp.stack(t) for t in zip(*p_states)]
    ss = [jnp.stack(t) for t in zip(*s_states)]
    return (xp.reshape(Bp, Sp, D_MODEL), xs.reshape(Bs, Ss, D_MODEL),
            sp[0], sp[1], sp[2], sp[3], sp[4], sp[5], sp[6], sp[7],
            ss[0], ss[1], ss[2], ss[3], ss[4], ss[5])
```

```python
import functools
import math

import jax
import jax.numpy as jnp
from jax import lax
from jax.experimental import pallas as pl
from jax.experimental.pallas import tpu as pltpu

BF = jnp.bfloat16
F32 = jnp.float32

D_MODEL = 1024
RG_WIDTH = 256
RG_BLOCKS = 4
CONV_W = 4
LRU_C = 8.0
RW_HEADS = 4
RW_HEAD = 64
RW_WIDTH = 256
RW_IN = 896
GN_EPS = 64e-5
L2_EPS = 1e-12
MLA_HEADS = 8
QK_NOPE = 64
QK_ROPE = 32
V_HEAD = 64
Q_LORA = 256
KV_LORA = 128
MLA_SCALE = (QK_NOPE + QK_ROPE) ** -0.5
SCALE_LOG2E = MLA_SCALE * math.log2(math.e)
ROPE_BASE = 10000.0
CHUNK = 64
N_MEM = 256
MEM_HEADS = 4
MEM_HEAD = 256
MEM_SCALE = MEM_HEAD ** -0.5
D_FF = 4096
EPS = 1e-6

LANES = 128
VMEM_LIMIT = 52 * 1024 * 1024

ROW_TILE = 512
RG_TILE = 512
RW_TILE = 512
RW_CHUNK = 64
RW_PASSES_GRAM = 1
RW_PASSES_INV = 1
RW_PASSES_LOCAL = 1
RW_PASSES_STATE = 1
ATT_TILE = 512
IN_SPLIT = 2
ROPE_LANE0 = 8
PV_ROWS = 144
FFN_ROW_TILE = 512
FF_TILE = 2048


def _cparams(*sem):
    return pltpu.CompilerParams(dimension_semantics=sem, vmem_limit_bytes=VMEM_LIMIT)


def _rms(x, g):
    return x * lax.rsqrt(jnp.mean(x * x, axis=-1, keepdims=True) + EPS) * g


def _dot(a, b):
    return jnp.dot(a.astype(BF), b.astype(BF), preferred_element_type=F32)


def _dot_nt(a, b):
    return lax.dot_general(a.astype(BF), b.astype(BF), (((1,), (1,)), ((), ())),
                           preferred_element_type=F32)


def _split(x):
    hi = x.astype(BF)
    lo = (x - hi.astype(F32)).astype(BF)
    return hi, lo


def _mm3(a, b, dims=(((1,), (0,)), ((), ())), passes=3):
    dg = functools.partial(lax.dot_general, dimension_numbers=dims, preferred_element_type=F32)
    if passes == 1:
        return dg(a.astype(BF), b.astype(BF))
    ah, al = _split(a)
    bh, bl = _split(b)
    return dg(ah, bh) + (dg(ah, bl) + dg(al, bh))


_NT = (((1,), (1,)), ((), ()))
_TN = (((0,), (0,)), ((), ()))


def _softplus(x):
    return jnp.maximum(x, 0.0) + jnp.log1p(jnp.exp(-jnp.abs(x)))


def _sigmoid(x):
    return 1.0 / (1.0 + jnp.exp(-x))


def _gelu_tanh(x):
    c = math.sqrt(2.0 / math.pi)
    return 0.5 * x * (1.0 + jnp.tanh(c * (x + 0.044715 * (x * x * x))))


def _shift_rows(x, d):
    return pltpu.roll(x, d, 0)


def _in_kernel(x_ref, g_ref, w_ref, gq_ref, gkv_ref, wq_ref, cos_ref, sin_ref,
               pa_ref, prw_ref, ckv_ref, kr_ref, kcat_ref, qcat_ref, *, tm):
    lane = lax.broadcasted_iota(jnp.int32, (1, LANES), 1)
    rope_lanes = (lane >= ROPE_LANE0) & (lane < ROPE_LANE0 + QK_ROPE)

    def rows(sl):
        xn = _rms(x_ref[sl, :], g_ref[...]).astype(BF)
        p = jnp.dot(xn, w_ref[...], preferred_element_type=F32)
        pa_ref[sl, :] = p[:, :512]
        prw_ref[sl, :] = p[:, 512:1408]
        cq = p[:, 1408:1664]
        ckv = p[:, 1664:1792]
        ka = p[:, 1792:1920]
        kb = p[:, 1920:2048]
        cos = cos_ref[sl, :]
        sin = sin_ref[sl, :]
        ckvn = _rms(ckv, gkv_ref[...])
        krt = ka * cos[:, :LANES] + kb * sin[:, :LANES]
        ckv_ref[sl, :] = ckvn
        kr_ref[sl, :] = krt[:, :QK_ROPE]
        kext = jnp.where(lane == 0, 1.0,
                         jnp.where(rope_lanes, pltpu.roll(krt, ROPE_LANE0, 1), 0.0))
        kcat_ref[sl, :] = jnp.concatenate([ckvn, kext], axis=-1).astype(BF)

        cqn = _rms(cq, gq_ref[...]).astype(BF)
        qq = jnp.dot(cqn, wq_ref[...], preferred_element_type=F32)
        qlat = qq[:, :1024]
        rr = qq[:, 1024:1280] * cos + qq[:, 1280:1536] * sin
        for h in range(MLA_HEADS):
            g = h // 4
            shift = (ROPE_LANE0 - QK_ROPE * (h % 4)) % LANES
            rpart = jnp.where(rope_lanes,
                              pltpu.roll(rr[:, g * LANES:(g + 1) * LANES], shift, 1), 0.0)
            qcat_ref[0, h, sl, :LANES] = (qlat[:, h * LANES:(h + 1) * LANES] * SCALE_LOG2E).astype(BF)
            qcat_ref[0, h, sl, LANES:] = (rpart * SCALE_LOG2E).astype(BF)

    step = tm // IN_SPLIT if tm % (16 * IN_SPLIT) == 0 else tm
    for r0 in range(0, tm, step):
        rows(slice(r0, r0 + step))


def _in_proj(x2d, B, S, g, w_ext, gq, gkv, wq_ext, cosq, sinq):
    T = B * S
    tm = min(ROW_TILE, S)
    nst = S // tm
    ntab = cosq.shape[0] // tm
    row = lambda i: (i, 0)
    const = lambda i: (0, 0)
    outs = pl.pallas_call(
        functools.partial(_in_kernel, tm=tm),
        grid=(T // tm,),
        in_specs=[
            pl.BlockSpec((tm, D_MODEL), row),
            pl.BlockSpec((1, D_MODEL), const),
            pl.BlockSpec((D_MODEL, 2048), const),
            pl.BlockSpec((1, Q_LORA), const),
            pl.BlockSpec((1, KV_LORA), const),
            pl.BlockSpec((Q_LORA, 1536), const),
            pl.BlockSpec((tm, 256), lambda i: (i % ntab, 0)),
            pl.BlockSpec((tm, 256), lambda i: (i % ntab, 0)),
        ],
        out_specs=[
            pl.BlockSpec((tm, 512), row),
            pl.BlockSpec((tm, RW_IN), row),
            pl.BlockSpec((tm, KV_LORA), row),
            pl.BlockSpec((tm, QK_ROPE), row),
            pl.BlockSpec((tm, 256), row),
            pl.BlockSpec((1, MLA_HEADS, tm, 256), lambda i: (i // nst, 0, i % nst, 0)),
        ],
        out_shape=[
            jax.ShapeDtypeStruct((T, 512), F32),
            jax.ShapeDtypeStruct((T, RW_IN), F32),
            jax.ShapeDtypeStruct((T, KV_LORA), F32),
            jax.ShapeDtypeStruct((T, QK_ROPE), F32),
            jax.ShapeDtypeStruct((T, 256), BF),
            jax.ShapeDtypeStruct((B, MLA_HEADS, S, 256), BF),
        ],
        compiler_params=_cparams("parallel"),
        name="in_proj",
    )(x2d, g, w_ext, gq, gkv, wq_ext, cosq, sinq)
    return outs


def _rglru_kernel(pa_ref, buf0_ref, h0_ref, cw_ref, cb_ref, wa_ref, ba_ref, wx_ref, bx_ref,
                  lam_ref, y_ref, buf_ref, hout_ref, ext_ref, h_ref, *, R):
    t = pl.program_id(1)

    @pl.when(t == 0)
    def _():
        ext_ref[0:8, :] = buf0_ref[0]
        h_ref[...] = h0_ref[0]

    pa = pa_ref[0]
    xr = pa[:, :RG_WIDTH]
    gate = pa[:, RG_WIDTH:]
    ext_ref[8:8 + R, :] = xr
    cw = cw_ref[...]
    conv = (cb_ref[...] + cw[3:4] * xr + cw[2:3] * ext_ref[7:7 + R, :]
            + cw[1:2] * ext_ref[6:6 + R, :] + cw[0:1] * ext_ref[5:5 + R, :])
    hist = ext_ref[R:R + 8, :]
    ext_ref[0:8, :] = hist
    buf_ref[0] = hist

    ra = _sigmoid(_dot(conv, wa_ref[...]) + ba_ref[...])
    ia = _sigmoid(_dot(conv, wx_ref[...]) + bx_ref[...])
    log_a = (-LRU_C) * ra * _softplus(-lam_ref[...])
    a = jnp.exp(log_a)
    b = jnp.sqrt(-jnp.tanh(log_a) * (a * a + 1.0)) * (ia * conv)

    rows = lax.broadcasted_iota(jnp.int32, (R, RG_WIDTH), 0)
    d = 1
    while d < R:
        m = rows >= d
        b = jnp.where(m, a * _shift_rows(b, d), 0.0) + b
        a = jnp.where(m, a * _shift_rows(a, d), a)
        d *= 2
    h = a * h_ref[...] + b
    hl = h[R - 1:R, :]
    h_ref[...] = hl
    hout_ref[0] = hl
    y_ref[0] = (_gelu_tanh(gate) * h).astype(BF)


def _rglru(pa, buf8, h0, cw, cb, wa_bd, ba, wx_bd, bx, lam):
    B, S, _ = pa.shape
    R = min(RG_TILE, S)
    vec = lambda b, t: (0, 0)
    return pl.pallas_call(
        functools.partial(_rglru_kernel, R=R),
        grid=(B, S // R),
        in_specs=[
            pl.BlockSpec((1, R, 512), lambda b, t: (b, t, 0)),
            pl.BlockSpec((1, 8, RG_WIDTH), lambda b, t: (b, 0, 0)),
            pl.BlockSpec((1, 1, RG_WIDTH), lambda b, t: (b, 0, 0)),
            pl.BlockSpec((CONV_W, RG_WIDTH), vec),
            pl.BlockSpec((1, RG_WIDTH), vec),
            pl.BlockSpec((RG_WIDTH, RG_WIDTH), vec),
            pl.BlockSpec((1, RG_WIDTH), vec),
            pl.BlockSpec((RG_WIDTH, RG_WIDTH), vec),
            pl.BlockSpec((1, RG_WIDTH), vec),
            pl.BlockSpec((1, RG_WIDTH), vec),
        ],
        out_specs=[
            pl.BlockSpec((1, R, RG_WIDTH), lambda b, t: (b, t, 0)),
            pl.BlockSpec((1, 8, RG_WIDTH), lambda b, t: (b, 0, 0)),
            pl.BlockSpec((1, 1, RG_WIDTH), lambda b, t: (b, 0, 0)),
        ],
        out_shape=[
            jax.ShapeDtypeStruct((B, S, RG_WIDTH), BF),
            jax.ShapeDtypeStruct((B, 8, RG_WIDTH), F32),
            jax.ShapeDtypeStruct((B, 1, RG_WIDTH), F32),
        ],
        scratch_shapes=[pltpu.VMEM((R + 8, RG_WIDTH), F32), pltpu.VMEM((1, RG_WIDTH), F32)],
        compiler_params=_cparams("parallel", "arbitrary"),
        name="rglru",
    )(pa, buf8, h0, cw, cb, wa_bd, ba, wx_bd, bx, lam)


def _rwkv_kernel(p_ref, shift0_ref, state0_ref, mu_ref, w0_ref, w2_ref, a0_ref, a2_ref, g2_ref,
                 kk_ref, ka_ref, rk_ref, gnw_ref, gnb_ref, ones_ref,
                 y_ref, shift_ref, state_ref, prev_ref, s_ref, *, R, C):
    t = pl.program_id(1)
    W = RW_WIDTH
    HC = RW_HEADS * C

    @pl.when(t == 0)
    def _():
        prev_ref[...] = shift0_ref[0]
        s_ref[...] = state0_ref[0]

    p = p_ref[0]
    rows = lax.broadcasted_iota(jnp.int32, (R, RW_IN), 0)
    prev = jnp.where(rows == 0, prev_ref[...], _shift_rows(p, 1))
    last = p[R - 1:R, :]
    prev_ref[...] = last
    shift_ref[0] = last
    xs = p + mu_ref[...] * (prev - p)
    r = xs[:, 0:W]
    k = xs[:, W:2 * W]
    v = xs[:, 2 * W:3 * W]
    x4 = xs[:, 3 * W:]
    ones_bd = ones_ref[...]

    def segsum(x):
        hi, lo = _split(x)
        dg = functools.partial(jnp.dot, preferred_element_type=F32)
        return dg(hi, ones_bd) + dg(lo, ones_bd)

    log_w = -_softplus(-(w0_ref[...] + _mm3(jnp.tanh(x4), w2_ref[...]))) - 0.5
    logw = -jnp.exp(log_w)
    a = _sigmoid(a0_ref[...] + _mm3(x4, a2_ref[...]))
    g = _mm3(_sigmoid(x4), g2_ref[...])
    kk = k * kk_ref[...]
    kk = kk * lax.rsqrt(segsum(kk * kk) + L2_EPS)
    k = k * (1.0 + (a - 1.0) * ka_ref[...])
    bonus = segsum(r * k * rk_ref[...]) * v
    kka = kk * a

    lane_head = lax.broadcasted_iota(jnp.int32, (1, W), 1) // RW_HEAD
    hmask = [lane_head == h for h in range(RW_HEADS)]

    def stack(x):
        return jnp.concatenate([jnp.where(hmask[h], x, 0.0) for h in range(RW_HEADS)], axis=0)

    ri = lax.broadcasted_iota(jnp.int32, (HC, HC), 0) % C
    ci = lax.broadcasted_iota(jnp.int32, (HC, HC), 1) % C
    strict = ri > ci
    incl = ri >= ci
    eye_hc = lax.broadcasted_iota(jnp.int32, (HC, HC), 0) == lax.broadcasted_iota(jnp.int32, (HC, HC), 1)
    eye_w = lax.broadcasted_iota(jnp.int32, (W, W), 0) == lax.broadcasted_iota(jnp.int32, (W, W), 1)
    crow = lax.broadcasted_iota(jnp.int32, (R, W), 0) % C

    mm_gram = functools.partial(_mm3, passes=RW_PASSES_GRAM)
    mm_inv = functools.partial(_mm3, passes=RW_PASSES_INV)
    mm_loc = functools.partial(_mm3, passes=RW_PASSES_LOCAL)
    mm_state = functools.partial(_mm3, passes=RW_PASSES_STATE)

    cum = logw
    d = 1
    while d < C:
        cum = cum + jnp.where(crow >= d, _shift_rows(cum, d), 0.0)
        d *= 2
    nch = R // C
    chunks = range(nch)
    sls = [slice(c * C, (c + 1) * C) for c in chunks]
    ends = [cum[(c + 1) * C - 1:(c + 1) * C, :] for c in chunks]
    cl = jnp.concatenate([jnp.broadcast_to(e, (C, W)) for e in ends], axis=0) if nch > 1 \
        else jnp.broadcast_to(ends[0], (C, W))
    w_inc = jnp.exp(cum)
    w_inv = jnp.exp(-cum)
    w_rem = jnp.exp(cl - cum)
    t_a = jnp.exp(cum - logw) * kk
    t_b = kka * w_inv
    t_k = k * w_inv
    t_r = r * w_inc
    t_kt = k * w_rem
    t_bt = kka * w_rem

    a_s = [stack(t_a[sl]) for sl in sls]
    r_s = [stack(t_r[sl]) for sl in sls]
    v_s = [stack(v[sl]) for sl in sls]
    kt_s = [stack(t_kt[sl]) for sl in sls]
    bt_s = [stack(t_bt[sl]) for sl in sls]
    bk = [jnp.concatenate([stack(t_b[sl]), stack(t_k[sl])], axis=0) for sl in sls]
    g_a = [mm_gram(a_s[c], bk[c], _NT) for c in chunks]
    g_r = [mm_gram(r_s[c], bk[c], _NT) for c in chunks]
    n = [jnp.where(strict, -g[:, :HC], 0.0) for g in g_a]
    l_k = [jnp.where(strict, g[:, HC:], 0.0) for g in g_a]
    m_b = [jnp.where(incl, g[:, :HC], 0.0) for g in g_r]
    m_k = [jnp.where(incl, g[:, HC:], 0.0) for g in g_r]

    tinv = [jnp.where(eye_hc, 1.0, 0.0) + x for x in n]
    pw = n
    d = 2
    while d < C:
        pw = [mm_inv(x, x) for x in pw]
        tinv = [tinv[c] + mm_inv(tinv[c], pw[c]) for c in chunks]
        d *= 2

    x1 = [mm_loc(l_k[c], v_s[c]) for c in chunks]
    tu = [mm_loc(tinv[c], jnp.concatenate([x1[c], a_s[c]], axis=1)) for c in chunks]
    mb = [mm_loc(m_b[c], tu[c]) for c in chunks]
    y_loc = [mm_loc(m_k[c], v_s[c]) - mb[c][:, :W] for c in chunks]
    r_t = [r_s[c] - mb[c][:, W:] for c in chunks]
    p_m = [jnp.where(eye_w, jnp.exp(ends[c]), 0.0) - mm_loc(tu[c][:, W:], bt_s[c], _TN) for c in chunks]
    q_m = [mm_loc(v_s[c], kt_s[c], _TN) - mm_loc(tu[c][:, :W], bt_s[c], _TN) for c in chunks]

    st = s_ref[...]
    ys = []
    for c in chunks:
        y_st = mm_state(r_t[c], st, _NT) + y_loc[c]
        yc = y_st[0:C]
        for h in range(1, RW_HEADS):
            yc = yc + y_st[h * C:(h + 1) * C]
        ys.append(yc)
        st = mm_state(st, p_m[c]) + q_m[c]
    s_ref[...] = st

    y = jnp.concatenate(ys, axis=0) if len(ys) > 1 else ys[0]
    state_ref[0] = s_ref[...]
    inv_n = 1.0 / RW_HEAD
    mean = segsum(y) * inv_n
    yc = y - mean
    var = segsum(yc * yc) * inv_n
    yn = yc * lax.rsqrt(var + GN_EPS) * gnw_ref[...] + gnb_ref[...]
    y_ref[0] = ((yn + bonus) * g).astype(BF)


def _rwkv(prw, shift0, state_bd, mu, w0, w2p, a0, a2p, g2p, k_k, k_a, r_k, gn_w, gn_b, ones_bd):
    B, S, _ = prw.shape
    R = min(RW_TILE, S)
    C = min(RW_CHUNK, S)
    vec = lambda b, t: (0, 0)
    wv = pl.BlockSpec((1, RW_WIDTH), vec)
    lora = pl.BlockSpec((LANES, RW_WIDTH), vec)
    return pl.pallas_call(
        functools.partial(_rwkv_kernel, R=R, C=C),
        grid=(B, S // R),
        in_specs=[
            pl.BlockSpec((1, R, RW_IN), lambda b, t: (b, t, 0)),
            pl.BlockSpec((1, 1, RW_IN), lambda b, t: (b, 0, 0)),
            pl.BlockSpec((1, RW_WIDTH, RW_WIDTH), lambda b, t: (b, 0, 0)),
            pl.BlockSpec((1, RW_IN), vec),
            wv, lora, wv, lora, lora, wv, wv, wv, wv, wv,
            pl.BlockSpec((RW_WIDTH, RW_WIDTH), vec),
        ],
        out_specs=[
            pl.BlockSpec((1, R, RW_WIDTH), lambda b, t: (b, t, 0)),
            pl.BlockSpec((1, 1, RW_IN), lambda b, t: (b, 0, 0)),
            pl.BlockSpec((1, RW_WIDTH, RW_WIDTH), lambda b, t: (b, 0, 0)),
        ],
        out_shape=[
            jax.ShapeDtypeStruct((B, S, RW_WIDTH), BF),
            jax.ShapeDtypeStruct((B, 1, RW_IN), F32),
            jax.ShapeDtypeStruct((B, RW_WIDTH, RW_WIDTH), F32),
        ],
        scratch_shapes=[pltpu.VMEM((1, RW_IN), F32), pltpu.VMEM((RW_WIDTH, RW_WIDTH), F32)],
        compiler_params=_cparams("parallel", "arbitrary"),
        name="rwkv7",
    )(prw, shift0, state_bd, mu, w0, w2p, a0, a2p, g2p, k_k, k_a, r_k, gn_w, gn_b, ones_bd)


def _attn_causal_kernel(q_ref, k_ref, wuv_ref, o_ref, m_ref, acc_ref, *, tq):
    qi = pl.program_id(1)
    q = q_ref[0].reshape(MLA_HEADS * tq, 256)
    m_ref[...] = jnp.full(m_ref.shape, -jnp.inf, F32)
    acc_ref[...] = jnp.zeros(acc_ref.shape, F32)

    def step(kblk, masked):
        s = lax.dot_general(kblk, q, _NT, preferred_element_type=F32)
        if masked:
            kc = lax.broadcasted_iota(jnp.int32, s.shape, 0) // CHUNK
            qc = (lax.broadcasted_iota(jnp.int32, s.shape, 1) % tq) // CHUNK
            s = jnp.where(kc <= qc, s, -jnp.inf)
        m_prev = m_ref[...]
        m_new = jnp.maximum(m_prev, jnp.max(s, axis=0, keepdims=True))
        alpha = jnp.exp2(m_prev - m_new)
        p = jnp.exp2(s - m_new).astype(BF)
        pv = lax.dot_general(kblk[:, :PV_ROWS], p, _TN, preferred_element_type=F32)
        acc_ref[...] = alpha * acc_ref[...] + pv
        m_ref[...] = m_new

    def body(ki, carry):
        step(k_ref[0, pl.ds(pl.multiple_of(ki * tq, tq), tq), :], False)
        return carry

    lax.fori_loop(0, qi, body, 0)
    step(k_ref[0, pl.ds(pl.multiple_of(qi * tq, tq), tq), :], True)
    o = (acc_ref[0:KV_LORA, :] / acc_ref[KV_LORA:KV_LORA + 1, :]).astype(BF)
    for hp in range(MLA_HEADS // 2):
        yt = [jnp.dot(wuv_ref[h], o[:, h * tq:(h + 1) * tq], preferred_element_type=F32)
              for h in (2 * hp, 2 * hp + 1)]
        o_ref[0, :, hp * LANES:(hp + 1) * LANES] = jnp.concatenate(yt, axis=0).T.astype(BF)


def _attn_causal(qcat, kcat, wuv_t):
    B, H, S, _ = qcat.shape
    tq = min(ATT_TILE, S)
    return pl.pallas_call(
        functools.partial(_attn_causal_kernel, tq=tq),
        grid=(B, S // tq),
        in_specs=[
            pl.BlockSpec((1, H, tq, 256), lambda b, i: (b, 0, i, 0)),
            pl.BlockSpec((1, S, 256), lambda b, i: (b, 0, 0)),
            pl.BlockSpec((H, V_HEAD, KV_LORA), lambda b, i: (0, 0, 0)),
        ],
        out_specs=pl.BlockSpec((1, tq, H * V_HEAD), lambda b, i: (b, i, 0)),
        out_shape=jax.ShapeDtypeStruct((B, S, H * V_HEAD), BF),
        scratch_shapes=[pltpu.VMEM((1, H * tq), F32), pltpu.VMEM((PV_ROWS, H * tq), F32)],
        compiler_params=_cparams("parallel", "arbitrary"),
        name="mla_causal",
    )(qcat, kcat, wuv_t)


def _attn_full_kernel(q_ref, ckv_ref, kr_ref, kn_ref, wuv_ref, o_ref, *, S):
    q = q_ref[0].reshape(MLA_HEADS * S, 256)
    ckv = ckv_ref[0].astype(BF)
    kr = kr_ref[0].astype(BF)
    kn = kn_ref[0]
    sel = (lax.broadcasted_iota(jnp.int32, (LANES, QK_ROPE), 0)
           == lax.broadcasted_iota(jnp.int32, (LANES, QK_ROPE), 1) + ROPE_LANE0)
    q_rope = jnp.dot(q[:, LANES:], jnp.where(sel, 1.0, 0.0).astype(BF),
                     preferred_element_type=F32).astype(BF)
    s1 = (lax.dot_general(q[:, :LANES], ckv, _NT, preferred_element_type=F32)
          + lax.dot_general(q_rope, kr, _NT, preferred_element_type=F32))
    s2 = lax.dot_general(q, kn, _NT, preferred_element_type=F32)
    m = jnp.maximum(jnp.max(s1, axis=-1, keepdims=True), jnp.max(s2, axis=-1, keepdims=True))
    p1 = jnp.exp2(s1 - m)
    p2 = jnp.exp2(s2 - m)
    l = jnp.sum(p1, axis=-1, keepdims=True) + jnp.sum(p2, axis=-1, keepdims=True)
    o = ((jnp.dot(p1.astype(BF), ckv, preferred_element_type=F32)
          + jnp.dot(p2.astype(BF), kn[:, :KV_LORA], preferred_element_type=F32)) / l).astype(BF)
    y = [jnp.dot(o[h * S:(h + 1) * S], wuv_ref[h], preferred_element_type=F32)
         for h in range(MLA_HEADS)]
    o_ref[0] = jnp.concatenate(y, axis=-1).astype(BF)


def _attn_full(qcat, past_ckv, past_kr, knew, wuv_h):
    B, H, S, _ = qcat.shape
    P = past_ckv.shape[1]
    return pl.pallas_call(
        functools.partial(_attn_full_kernel, S=S),
        grid=(B,),
        in_specs=[
            pl.BlockSpec((1, H, S, 256), lambda b: (b, 0, 0, 0)),
            pl.BlockSpec((1, P, KV_LORA), lambda b: (b, 0, 0)),
            pl.BlockSpec((1, P, QK_ROPE), lambda b: (b, 0, 0)),
            pl.BlockSpec((1, S, 256), lambda b: (b, 0, 0)),
            pl.BlockSpec((H, KV_LORA, V_HEAD), lambda b: (0, 0, 0)),
        ],
        out_specs=pl.BlockSpec((1, S, H * V_HEAD), lambda b: (b, 0, 0)),
        out_shape=jax.ShapeDtypeStruct((B, S, H * V_HEAD), BF),
        compiler_params=_cparams("parallel"),
        name="mla_full",
    )(qcat, past_ckv, past_kr, knew, wuv_h)


def _out_kernel(x_ref, yrg_ref, yrw_ref, ymla_ref, wout_ref, o_ref):
    ycat = jnp.concatenate([yrg_ref[...], yrw_ref[...], ymla_ref[...]], axis=-1)
    o_ref[...] = x_ref[...] + jnp.dot(ycat, wout_ref[...], preferred_element_type=F32)


def _out_proj(x2d, yrg, yrw, ymla, wout):
    T = x2d.shape[0]
    tm = min(ROW_TILE, T)
    row = lambda i: (i, 0)
    const = lambda i: (0, 0)
    return pl.pallas_call(
        _out_kernel,
        grid=(T // tm,),
        in_specs=[
            pl.BlockSpec((tm, D_MODEL), row),
            pl.BlockSpec((tm, RG_WIDTH), row),
            pl.BlockSpec((tm, RW_WIDTH), row),
            pl.BlockSpec((tm, MLA_HEADS * V_HEAD), row),
            pl.BlockSpec((D_MODEL, D_MODEL), const),
        ],
        out_specs=pl.BlockSpec((tm, D_MODEL), row),
        out_shape=jax.ShapeDtypeStruct((T, D_MODEL), F32),
        compiler_params=_cparams("parallel"),
        name="out_proj",
    )(x2d, yrg, yrw, ymla, wout)


def _mem_kernel(x_ref, g_ref, wq_ref, mk_ref, mv_ref, wo_ref, o_ref):
    x = x_ref[...]
    hn = _rms(x, g_ref[...]).astype(BF)
    q = jnp.dot(hn, wq_ref[...], preferred_element_type=F32).astype(BF)
    mk = mk_ref[0]
    mv = mv_ref[0]
    outs = []
    for h in range(MEM_HEADS):
        sl = slice(h * MEM_HEAD, (h + 1) * MEM_HEAD)
        s = lax.dot_general(q[:, sl], mk[:, sl], _NT, preferred_element_type=F32) * MEM_SCALE
        e = jnp.exp(s - jnp.max(s, axis=-1, keepdims=True))
        pr = e / jnp.sum(e, axis=-1, keepdims=True)
        outs.append(jnp.dot(pr.astype(BF), mv[:, sl], preferred_element_type=F32).astype(BF))
    o = jnp.concatenate(outs, axis=-1)
    o_ref[...] = x + jnp.dot(o, wo_ref[...], preferred_element_type=F32)


def _mem_attn(x2d, B, S, g, wq, mk, mv, wo):
    T = B * S
    tm = min(ROW_TILE, S)
    nst = S // tm
    row = lambda i: (i, 0)
    const = lambda i: (0, 0)
    return pl.pallas_call(
        _mem_kernel,
        grid=(T // tm,),
        in_specs=[
            pl.BlockSpec((tm, D_MODEL), row),
            pl.BlockSpec((1, D_MODEL), const),
            pl.BlockSpec((D_MODEL, D_MODEL), const),
            pl.BlockSpec((1, N_MEM, D_MODEL), lambda i: (i // nst, 0, 0)),
            pl.BlockSpec((1, N_MEM, D_MODEL), lambda i: (i // nst, 0, 0)),
            pl.BlockSpec((D_MODEL, D_MODEL), const),
        ],
        out_specs=pl.BlockSpec((tm, D_MODEL), row),
        out_shape=jax.ShapeDtypeStruct((T, D_MODEL), F32),
        compiler_params=_cparams("parallel"),
        name="mem_attn",
    )(x2d, g, wq, mk, mv, wo)


def _ffn_kernel(x_ref, g_ref, w1_ref, w2_ref, gf_ref, o_ref, *, final):
    x = x_ref[...]
    xn = _rms(x, g_ref[...]).astype(BF)
    acc = x
    for c in range(D_FF // FF_TILE):
        h = jnp.dot(xn, w1_ref[:, c * FF_TILE:(c + 1) * FF_TILE], preferred_element_type=F32)
        h = jnp.square(jnp.maximum(h, 0.0)).astype(BF)
        acc = acc + jnp.dot(h, w2_ref[c * FF_TILE:(c + 1) * FF_TILE, :], preferred_element_type=F32)
    if final:
        acc = _rms(acc, gf_ref[...])
    o_ref[...] = acc


def _ffn(x2d, g, w1, w2, gf, final):
    T = x2d.shape[0]
    tm = min(FFN_ROW_TILE, T)
    const = lambda i: (0, 0)
    return pl.pallas_call(
        functools.partial(_ffn_kernel, final=final),
        grid=(T // tm,),
        in_specs=[
            pl.BlockSpec((tm, D_MODEL), lambda i: (i, 0)),
            pl.BlockSpec((1, D_MODEL), const),
            pl.BlockSpec((D_MODEL, D_FF), const, pipeline_mode=pl.Buffered(1)),
            pl.BlockSpec((D_FF, D_MODEL), const, pipeline_mode=pl.Buffered(1)),
            pl.BlockSpec((1, D_MODEL), const),
        ],
        out_specs=pl.BlockSpec((tm, D_MODEL), lambda i: (i, 0)),
        out_shape=jax.ShapeDtypeStruct((T, D_MODEL), F32),
        compiler_params=_cparams("parallel"),
        name="ffn",
    )(x2d, g, w1, w2, gf)


def _memproj_kernel(m_ref, g_ref, w_ref, k_ref, v_ref, kb_ref, vb_ref):
    mn = _rms(m_ref[...], g_ref[...]).astype(BF)
    kv = jnp.dot(mn, w_ref[...], preferred_element_type=F32)
    k = kv[:, :D_MODEL]
    v = kv[:, D_MODEL:]
    k_ref[...] = k
    v_ref[...] = v
    kb_ref[...] = k.astype(BF)
    vb_ref[...] = v.astype(BF)


def _mem_project(mem2d, g, wkv):
    T = mem2d.shape[0]
    tm = min(ROW_TILE, T)
    row = lambda i: (i, 0)
    const = lambda i: (0, 0)
    return pl.pallas_call(
        _memproj_kernel,
        grid=(T // tm,),
        in_specs=[
            pl.BlockSpec((tm, D_MODEL), row),
            pl.BlockSpec((1, D_MODEL), const),
            pl.BlockSpec((D_MODEL, 2 * D_MODEL), const),
        ],
        out_specs=[pl.BlockSpec((tm, D_MODEL), row)] * 4,
        out_shape=[jax.ShapeDtypeStruct((T, D_MODEL), F32)] * 2
                  + [jax.ShapeDtypeStruct((T, D_MODEL), BF)] * 2,
        compiler_params=_cparams("parallel"),
        name="mem_project",
    )(mem2d, g, wkv)


def _fold_kernel(a_ref, b_ref, o_ref):
    o_ref[0] = _mm3(a_ref[0], b_ref[0])


def _fold_heads(a, b):
    G, m, k = a.shape
    n = b.shape[2]
    return pl.pallas_call(
        _fold_kernel,
        grid=(G,),
        in_specs=[pl.BlockSpec((1, m, k), lambda g: (g, 0, 0)),
                  pl.BlockSpec((1, k, n), lambda g: (g, 0, 0))],
        out_specs=pl.BlockSpec((1, m, n), lambda g: (g, 0, 0)),
        out_shape=jax.ShapeDtypeStruct((G, m, n), F32),
        compiler_params=_cparams("parallel"),
        name="fold_heads",
    )(a, b)


def _block_diag(blocks):
    G, m, n = blocks.shape
    eye = jnp.eye(G, dtype=blocks.dtype)
    return jnp.einsum('gmn,gh->gmhn', blocks, eye).reshape(G * m, G * n)


def _prep_weights(l, W):
    r2 = lambda v: v.reshape(1, -1)
    w_in = W['w_in'][l]
    kr_cols = w_in[:, 1792:1824]
    kr_swap = jnp.concatenate([kr_cols[:, 16:], kr_cols[:, :16]], axis=1)
    w_ext = jnp.concatenate([w_in[:, :1792], jnp.tile(kr_cols, (1, 4)), jnp.tile(kr_swap, (1, 4))],
                            axis=1).astype(BF)
    w_uq = W['mla_w_uq'][l]
    rope = w_uq[:, :, QK_NOPE:]
    rope_swap = jnp.concatenate([rope[:, :, 16:], rope[:, :, :16]], axis=2)
    w_fold = _fold_heads(jnp.transpose(w_uq[:, :, :QK_NOPE], (1, 0, 2)),
                         jnp.transpose(W['mla_w_uk'][l], (1, 2, 0)))
    wq_ext = jnp.concatenate([jnp.transpose(w_fold, (1, 0, 2)).reshape(Q_LORA, -1),
                              rope.reshape(Q_LORA, -1), rope_swap.reshape(Q_LORA, -1)],
                             axis=1).astype(BF)
    w_uv = W['mla_w_uv'][l]
    wuv_t = jnp.transpose(w_uv, (1, 2, 0)).astype(BF)
    wuv_h = jnp.transpose(w_uv, (1, 0, 2)).astype(BF)
    zpad = lambda top, mat, bot: jnp.concatenate(
        [jnp.zeros((top, RW_WIDTH), F32), mat, jnp.zeros((bot, RW_WIDTH), F32)], axis=0)
    return dict(
        norm_mix=r2(W['norm_mix'][l]), w_ext=w_ext, gq=r2(W['mla_q_norm'][l]),
        gkv=r2(W['mla_kv_norm'][l]), wq_ext=wq_ext, wuv_t=wuv_t, wuv_h=wuv_h,
        cw=W['rg_conv_w'][l], cb=r2(W['rg_conv_b'][l]),
        wa_bd=_block_diag(W['rg_wa'][l]).astype(BF), ba=r2(W['rg_ba'][l]),
        wx_bd=_block_diag(W['rg_wx'][l]).astype(BF), bx=r2(W['rg_bx'][l]),
        lam=r2(W['rg_lambda'][l]),
        mu=r2(W['rw_mu'][l]), w0=r2(W['rw_w0'][l]), w2p=zpad(0, W['rw_w2'][l], 96),
        a0=r2(W['rw_a0'][l]), a2p=zpad(32, W['rw_a2'][l], 64), g2p=zpad(64, W['rw_g2'][l], 0),
        k_k=r2(W['rw_k_k'][l]), k_a=r2(W['rw_k_a'][l]), r_k=r2(W['rw_r_k'][l]),
        gn_w=r2(W['rw_gn_w'][l]), gn_b=r2(W['rw_gn_b'][l]),
        w_out=W['w_out'][l].astype(BF),
        norm_mem=r2(W['norm_mem'][l]), wq=W['mem_w_q'][l].reshape(D_MODEL, D_MODEL).astype(BF),
        wo=W['mem_w_o'][l].reshape(D_MODEL, D_MODEL).astype(BF),
        norm_ffn=r2(W['norm_ffn'][l]), w1=W['ffn_w1'][l].astype(BF), w2=W['ffn_w2'][l].astype(BF),
        norm_mem_kv=r2(W['norm_mem_kv'][l]),
        wkv=jnp.concatenate([W['mem_w_k'][l].reshape(D_MODEL, D_MODEL),
                             W['mem_w_v'][l].reshape(D_MODEL, D_MODEL)], axis=1).astype(BF),
    )


def _rope_tables(pos, rows):
    half = QK_ROPE // 2
    inv = ROPE_BASE ** (-jnp.arange(half, dtype=F32) / half)
    ang = pos.astype(F32)[:, None] * inv
    cos, sin = jnp.cos(ang), jnp.sin(ang)
    cosq = jnp.tile(jnp.concatenate([cos, cos], axis=1), (1, MLA_HEADS))
    sinq = jnp.tile(jnp.concatenate([-sin, sin], axis=1), (1, MLA_HEADS))
    reps = max(1, rows // pos.shape[0])
    return jnp.tile(cosq, (reps, 1)), jnp.tile(sinq, (reps, 1))


def _layer(x2d, B, S, P, tabs, rg_buf, rg_h, rw_shift, rw_state, past, mem_k, mem_v, gf, final):
    cosq, sinq = tabs
    pa, prw, ckv, kr, kcat, qcat = _in_proj(x2d, B, S, P['norm_mix'], P['w_ext'], P['gq'], P['gkv'],
                                            P['wq_ext'], cosq, sinq)
    buf8 = jnp.concatenate([jnp.zeros((B, 5, RG_WIDTH), F32), rg_buf], axis=1)
    yrg, buf_new, h_new = _rglru(pa.reshape(B, S, 512), buf8, rg_h.reshape(B, 1, RG_WIDTH),
                                 P['cw'], P['cb'], P['wa_bd'], P['ba'], P['wx_bd'], P['bx'], P['lam'])
    eye_h = jnp.eye(RW_HEADS, dtype=F32)
    state_bd = jnp.einsum('bhij,hg->bhigj', rw_state, eye_h).reshape(B, RW_WIDTH, RW_WIDTH)
    ones_bd = _block_diag(jnp.ones((RW_HEADS, RW_HEAD, RW_HEAD), F32)).astype(BF)
    yrw, shift_new, state_new = _rwkv(prw.reshape(B, S, RW_IN), rw_shift.reshape(B, 1, RW_IN), state_bd,
                                      P['mu'], P['w0'], P['w2p'], P['a0'], P['a2p'], P['g2p'],
                                      P['k_k'], P['k_a'], P['r_k'], P['gn_w'], P['gn_b'], ones_bd)
    wkv_new = jnp.einsum('bhigj,hg->bhij',
                         state_new.reshape(B, RW_HEADS, RW_HEAD, RW_HEADS, RW_HEAD), eye_h)
    kcat3 = kcat.reshape(B, S, 256)
    if past is None:
        ymla = _attn_causal(qcat, kcat3, P['wuv_t'])
    else:
        ymla = _attn_full(qcat, past[0], past[1], kcat3, P['wuv_h'])
    x1 = _out_proj(x2d, yrg.reshape(B * S, RG_WIDTH), yrw.reshape(B * S, RW_WIDTH),
                   ymla.reshape(B * S, MLA_HEADS * V_HEAD), P['w_out'])
    x2 = _mem_attn(x1, B, S, P['norm_mem'], P['wq'], mem_k, mem_v, P['wo'])
    x3 = _ffn(x2, P['norm_ffn'], P['w1'], P['w2'], gf, final)
    new_state = (buf_new[:, 5:8], h_new.reshape(B, RG_WIDTH), shift_new.reshape(B, RW_IN), wkv_new,
                 ckv.reshape(B, S, KV_LORA), kr.reshape(B, S, QK_ROPE))
    return x3, new_state


def kernel(x_prompt, x_sample, state_rg_conv, state_rg_h, state_rw_shift, state_rw_wkv, cache_mla_ckv, cache_mla_krope, cache_mem_k, cache_mem_v, mem_prompt, norm_mix, w_in, rg_conv_w, rg_conv_b, rg_wa, rg_ba, rg_wx, rg_bx, rg_lambda, rw_mu, rw_w0, rw_w2, rw_a0, rw_a2, rw_g2, rw_k_k, rw_k_a, rw_r_k, rw_gn_w, rw_gn_b, mla_q_norm, mla_kv_norm, mla_w_uq, mla_w_uk, mla_w_uv, w_out, norm_mem, norm_mem_kv, mem_w_q, mem_w_k, mem_w_v, mem_w_o, norm_ffn, ffn_w1, ffn_w2, norm_final):
    W = dict(norm_mix=norm_mix, w_in=w_in, rg_conv_w=rg_conv_w, rg_conv_b=rg_conv_b, rg_wa=rg_wa,
             rg_ba=rg_ba, rg_wx=rg_wx, rg_bx=rg_bx, rg_lambda=rg_lambda, rw_mu=rw_mu, rw_w0=rw_w0,
             rw_w2=rw_w2, rw_a0=rw_a0, rw_a2=rw_a2, rw_g2=rw_g2, rw_k_k=rw_k_k, rw_k_a=rw_k_a,
             rw_r_k=rw_r_k.reshape(rw_r_k.shape[0], RW_WIDTH), rw_gn_w=rw_gn_w, rw_gn_b=rw_gn_b,
             mla_q_norm=mla_q_norm, mla_kv_norm=mla_kv_norm, mla_w_uq=mla_w_uq, mla_w_uk=mla_w_uk,
             mla_w_uv=mla_w_uv, w_out=w_out, norm_mem=norm_mem, norm_mem_kv=norm_mem_kv,
             mem_w_q=mem_w_q, mem_w_k=mem_w_k, mem_w_v=mem_w_v, mem_w_o=mem_w_o, norm_ffn=norm_ffn,
             ffn_w1=ffn_w1, ffn_w2=ffn_w2)
    depth = norm_mix.shape[0]
    Bp, Sp, _ = x_prompt.shape
    Bs, Ss, _ = x_sample.shape
    past_len = cache_mla_ckv.shape[2]
    tabs_p = _rope_tables(jnp.arange(Sp, dtype=jnp.int32), min(ROW_TILE, Sp))
    tabs_s = _rope_tables(past_len + jnp.arange(Ss, dtype=jnp.int32), min(ROW_TILE, Ss))
    gf = norm_final.reshape(1, D_MODEL)

    xp = x_prompt.reshape(Bp * Sp, D_MODEL)
    xs = x_sample.reshape(Bs * Ss, D_MODEL)
    z = lambda *s: jnp.zeros(s, F32)
    p_states, s_states = [], []
    for l in range(depth):
        P = _prep_weights(l, W)
        final = l == depth - 1
        mk, mv, mkb, mvb = _mem_project(mem_prompt.reshape(Bp * N_MEM, D_MODEL), P['norm_mem_kv'], P['wkv'])
        xp, st_p = _layer(xp, Bp, Sp, P, tabs_p, z(Bp, 3, RG_WIDTH), z(Bp, RG_WIDTH), z(Bp, RW_IN),
                          z(Bp, RW_HEADS, RW_HEAD, RW_HEAD), None,
                          mkb.reshape(Bp, N_MEM, D_MODEL), mvb.reshape(Bp, N_MEM, D_MODEL), gf, final)
        xs, st_s = _layer(xs, Bs, Ss, P, tabs_s, state_rg_conv[l], state_rg_h[l], state_rw_shift[l],
                          state_rw_wkv[l], (cache_mla_ckv[l], cache_mla_krope[l]),
                          cache_mem_k[l].reshape(Bs, N_MEM, D_MODEL).astype(BF),
                          cache_mem_v[l].reshape(Bs, N_MEM, D_MODEL).astype(BF), gf, final)
        p_states.append(st_p + (mk.reshape(Bp, N_MEM, MEM_HEADS, MEM_HEAD),
                                mv.reshape(Bp, N_MEM, MEM_HEADS, MEM_HEAD)))
        s_states.append(st_s)

    sp = [jnp.stack(t) for t in zip(*p_states)]
    ss = [jnp.stack(t) for t in zip(*s_states)]
    return (xp.reshape(Bp, Sp, D_MODEL), xs.reshape(Bs, Ss, D_MODEL),
            sp[0], sp[1], sp[2], sp[3], sp[4], sp[5], sp[6], sp[7],
            ss[0], ss[1], ss[2], ss[3], ss[4], ss[5])
```

```python
import functools
import math

import jax
import jax.numpy as jnp
from jax import lax
from jax.experimental import pallas as pl
from jax.experimental.pallas import tpu as pltpu

BF = jnp.bfloat16
F32 = jnp.float32

D_MODEL = 1024
RG_WIDTH = 256
RG_BLOCKS = 4
CONV_W = 4
LRU_C = 8.0
RW_HEADS = 4
RW_HEAD = 64
RW_WIDTH = 256
RW_IN = 896
GN_EPS = 64e-5
L2_EPS = 1e-12
MLA_HEADS = 8
QK_NOPE = 64
QK_ROPE = 32
V_HEAD = 64
Q_LORA = 256
KV_LORA = 128
MLA_SCALE = (QK_NOPE + QK_ROPE) ** -0.5
SCALE_LOG2E = MLA_SCALE * math.log2(math.e)
ROPE_BASE = 10000.0
CHUNK = 64
N_MEM = 256
MEM_HEADS = 4
MEM_HEAD = 256
MEM_SCALE = MEM_HEAD ** -0.5
D_FF = 4096
EPS = 1e-6

LANES = 128
VMEM_LIMIT = 52 * 1024 * 1024

ROW_TILE = 512
RG_TILE = 512
RW_TILE = 512
RW_CHUNK = 64
RW_PASSES_GRAM = 1
RW_PASSES_INV = 1
RW_PASSES_LOCAL = 1
RW_PASSES_STATE = 1
ATT_TILE = 512
IN_SPLIT = 2
ROPE_LANE0 = 8
PV_ROWS = 144
FFN_ROW_TILE = 512
FF_TILE = 2048


def _cparams(*sem):
    return pltpu.CompilerParams(dimension_semantics=sem, vmem_limit_bytes=VMEM_LIMIT)


def _rms(x, g):
    return x * lax.rsqrt(jnp.mean(x * x, axis=-1, keepdims=True) + EPS) * g


def _dot(a, b):
    return jnp.dot(a.astype(BF), b.astype(BF), preferred_element_type=F32)


def _dot_nt(a, b):
    return lax.dot_general(a.astype(BF), b.astype(BF), (((1,), (1,)), ((), ())),
                           preferred_element_type=F32)


def _split(x):
    hi = x.astype(BF)
    lo = (x - hi.astype(F32)).astype(BF)
    return hi, lo


def _mm3(a, b, dims=(((1,), (0,)), ((), ())), passes=3):
    dg = functools.partial(lax.dot_general, dimension_numbers=dims, preferred_element_type=F32)
    if passes == 1:
        return dg(a.astype(BF), b.astype(BF))
    ah, al = _split(a)
    bh, bl = _split(b)
    return dg(ah, bh) + (dg(ah, bl) + dg(al, bh))


_NT = (((1,), (1,)), ((), ()))
_TN = (((0,), (0,)), ((), ()))


def _softplus(x):
    return jnp.maximum(x, 0.0) + jnp.log1p(jnp.exp(-jnp.abs(x)))


def _sigmoid(x):
    return 1.0 / (1.0 + jnp.exp(-x))


def _gelu_tanh(x):
    c = math.sqrt(2.0 / math.pi)
    return 0.5 * x * (1.0 + jnp.tanh(c * (x + 0.044715 * (x * x * x))))


def _shift_rows(x, d):
    return pltpu.roll(x, d, 0)


def _in_kernel(x_ref, g_ref, w_ref, gq_ref, gkv_ref, wq_ref, cos_ref, sin_ref, *rest, tm, n_prev):
    pa_ref, prw_ref, ckv_ref, kr_ref, kcat_ref, qcat_ref = rest[n_prev:]
    lane = lax.broadcasted_iota(jnp.int32, (1, LANES), 1)
    rope_lanes = (lane >= ROPE_LANE0) & (lane < ROPE_LANE0 + QK_ROPE)

    def rows(sl):
        xn = _rms(x_ref[sl, :], g_ref[...]).astype(BF)
        p = jnp.dot(xn, w_ref[...], preferred_element_type=F32)
        pa_ref[sl, :] = p[:, :512]
        prw_ref[sl, :] = p[:, 512:1408]
        cq = p[:, 1408:1664]
        ckv = p[:, 1664:1792]
        ka = p[:, 1792:1920]
        kb = p[:, 1920:2048]
        cos = cos_ref[sl, :]
        sin = sin_ref[sl, :]
        ckvn = _rms(ckv, gkv_ref[...])
        krt = ka * cos[:, :LANES] + kb * sin[:, :LANES]
        ckv_ref[sl, :] = ckvn
        kr_ref[sl, :] = krt[:, :QK_ROPE]
        kext = jnp.where(lane == 0, 1.0,
                         jnp.where(rope_lanes, pltpu.roll(krt, ROPE_LANE0, 1), 0.0))
        kcat_ref[sl, :] = jnp.concatenate([ckvn, kext], axis=-1).astype(BF)

        cqn = _rms(cq, gq_ref[...]).astype(BF)
        qq = jnp.dot(cqn, wq_ref[...], preferred_element_type=F32)
        qlat = qq[:, :1024]
        rr = qq[:, 1024:1280] * cos + qq[:, 1280:1536] * sin
        for h in range(MLA_HEADS):
            g = h // 4
            shift = (ROPE_LANE0 - QK_ROPE * (h % 4)) % LANES
            rpart = jnp.where(rope_lanes,
                              pltpu.roll(rr[:, g * LANES:(g + 1) * LANES], shift, 1), 0.0)
            qcat_ref[0, h, sl, :LANES] = (qlat[:, h * LANES:(h + 1) * LANES] * SCALE_LOG2E).astype(BF)
            qcat_ref[0, h, sl, LANES:] = (rpart * SCALE_LOG2E).astype(BF)

    step = tm // IN_SPLIT if tm % (16 * IN_SPLIT) == 0 else tm
    for r0 in range(0, tm, step):
        rows(slice(r0, r0 + step))


def _in_proj(x2d, B, S, g, w_ext, gq, gkv, wq_ext, cosq, sinq, l, depth, prev):
    T = B * S
    n_prev = 0 if prev is None else len(prev)
    tm = min(ROW_TILE, S)
    nst = S // tm
    ntab = cosq.shape[0] // tm
    row = lambda i: (i, 0)
    const = lambda i: (0, 0)
    outs = pl.pallas_call(
        functools.partial(_in_kernel, tm=tm, n_prev=n_prev),
        grid=(T // tm,),
        in_specs=[
            pl.BlockSpec((tm, D_MODEL), row),
            pl.BlockSpec((1, D_MODEL), const),
            pl.BlockSpec((D_MODEL, 2048), const),
            pl.BlockSpec((1, Q_LORA), const),
            pl.BlockSpec((1, KV_LORA), const),
            pl.BlockSpec((Q_LORA, 1536), const),
            pl.BlockSpec((tm, 256), lambda i: (i % ntab, 0)),
            pl.BlockSpec((tm, 256), lambda i: (i % ntab, 0)),
        ] + [pl.BlockSpec(memory_space=pl.ANY)] * n_prev,
        out_specs=[
            pl.BlockSpec((tm, 512), row),
            pl.BlockSpec((tm, RW_IN), row),
            pl.BlockSpec((None, tm, KV_LORA), lambda i: (l, i, 0)),
            pl.BlockSpec((None, tm, QK_ROPE), lambda i: (l, i, 0)),
            pl.BlockSpec((tm, 256), row),
            pl.BlockSpec((1, MLA_HEADS, tm, 256), lambda i: (i // nst, 0, i % nst, 0)),
        ],
        out_shape=[
            jax.ShapeDtypeStruct((T, 512), F32),
            jax.ShapeDtypeStruct((T, RW_IN), F32),
            jax.ShapeDtypeStruct((depth, T, KV_LORA), F32),
            jax.ShapeDtypeStruct((depth, T, QK_ROPE), F32),
            jax.ShapeDtypeStruct((T, 256), BF),
            jax.ShapeDtypeStruct((B, MLA_HEADS, S, 256), BF),
        ],
        input_output_aliases={8 + j: 2 + j for j in range(n_prev)},
        compiler_params=_cparams("parallel"),
        name="in_proj",
    )(x2d, g, w_ext, gq, gkv, wq_ext, cosq, sinq, *([] if prev is None else prev))
    return outs


def _rglru_kernel(pa_ref, buf0_ref, h0_ref, cw_ref, cb_ref, wa_ref, ba_ref, wx_ref, bx_ref,
                  lam_ref, y_ref, buf_ref, hout_ref, ext_ref, h_ref, *, R):
    t = pl.program_id(1)

    @pl.when(t == 0)
    def _():
        ext_ref[0:8, :] = buf0_ref[0]
        h_ref[...] = h0_ref[0]

    pa = pa_ref[0]
    xr = pa[:, :RG_WIDTH]
    gate = pa[:, RG_WIDTH:]
    ext_ref[8:8 + R, :] = xr
    cw = cw_ref[...]
    conv = (cb_ref[...] + cw[3:4] * xr + cw[2:3] * ext_ref[7:7 + R, :]
            + cw[1:2] * ext_ref[6:6 + R, :] + cw[0:1] * ext_ref[5:5 + R, :])
    hist = ext_ref[R:R + 8, :]
    ext_ref[0:8, :] = hist
    buf_ref[0] = hist

    ra = _sigmoid(_dot(conv, wa_ref[...]) + ba_ref[...])
    ia = _sigmoid(_dot(conv, wx_ref[...]) + bx_ref[...])
    log_a = (-LRU_C) * ra * _softplus(-lam_ref[...])
    a = jnp.exp(log_a)
    b = jnp.sqrt(-jnp.tanh(log_a) * (a * a + 1.0)) * (ia * conv)

    rows = lax.broadcasted_iota(jnp.int32, (R, RG_WIDTH), 0)
    d = 1
    while d < R:
        m = rows >= d
        b = jnp.where(m, a * _shift_rows(b, d), 0.0) + b
        a = jnp.where(m, a * _shift_rows(a, d), a)
        d *= 2
    h = a * h_ref[...] + b
    hl = h[R - 1:R, :]
    h_ref[...] = hl
    hout_ref[0] = hl
    y_ref[0] = (_gelu_tanh(gate) * h).astype(BF)


def _rglru(pa, buf8, h0, cw, cb, wa_bd, ba, wx_bd, bx, lam):
    B, S, _ = pa.shape
    R = min(RG_TILE, S)
    vec = lambda b, t: (0, 0)
    return pl.pallas_call(
        functools.partial(_rglru_kernel, R=R),
        grid=(B, S // R),
        in_specs=[
            pl.BlockSpec((1, R, 512), lambda b, t: (b, t, 0)),
            pl.BlockSpec((1, 8, RG_WIDTH), lambda b, t: (b, 0, 0)),
            pl.BlockSpec((1, 1, RG_WIDTH), lambda b, t: (b, 0, 0)),
            pl.BlockSpec((CONV_W, RG_WIDTH), vec),
            pl.BlockSpec((1, RG_WIDTH), vec),
            pl.BlockSpec((RG_WIDTH, RG_WIDTH), vec),
            pl.BlockSpec((1, RG_WIDTH), vec),
            pl.BlockSpec((RG_WIDTH, RG_WIDTH), vec),
            pl.BlockSpec((1, RG_WIDTH), vec),
            pl.BlockSpec((1, RG_WIDTH), vec),
        ],
        out_specs=[
            pl.BlockSpec((1, R, RG_WIDTH), lambda b, t: (b, t, 0)),
            pl.BlockSpec((1, 8, RG_WIDTH), lambda b, t: (b, 0, 0)),
            pl.BlockSpec((1, 1, RG_WIDTH), lambda b, t: (b, 0, 0)),
        ],
        out_shape=[
            jax.ShapeDtypeStruct((B, S, RG_WIDTH), BF),
            jax.ShapeDtypeStruct((B, 8, RG_WIDTH), F32),
            jax.ShapeDtypeStruct((B, 1, RG_WIDTH), F32),
        ],
        scratch_shapes=[pltpu.VMEM((R + 8, RG_WIDTH), F32), pltpu.VMEM((1, RG_WIDTH), F32)],
        compiler_params=_cparams("parallel", "arbitrary"),
        name="rglru",
    )(pa, buf8, h0, cw, cb, wa_bd, ba, wx_bd, bx, lam)


def _rwkv_kernel(p_ref, shift0_ref, state0_ref, mu_ref, w0_ref, w2_ref, a0_ref, a2_ref, g2_ref,
                 kk_ref, ka_ref, rk_ref, gnw_ref, gnb_ref, ones_ref,
                 y_ref, shift_ref, state_ref, prev_ref, s_ref, *, R, C):
    t = pl.program_id(1)
    W = RW_WIDTH
    HC = RW_HEADS * C

    @pl.when(t == 0)
    def _():
        prev_ref[...] = shift0_ref[0]
        s_ref[...] = state0_ref[0]

    p = p_ref[0]
    rows = lax.broadcasted_iota(jnp.int32, (R, RW_IN), 0)
    prev = jnp.where(rows == 0, prev_ref[...], _shift_rows(p, 1))
    last = p[R - 1:R, :]
    prev_ref[...] = last
    shift_ref[0] = last
    xs = p + mu_ref[...] * (prev - p)
    r = xs[:, 0:W]
    k = xs[:, W:2 * W]
    v = xs[:, 2 * W:3 * W]
    x4 = xs[:, 3 * W:]
    ones_bd = ones_ref[...]

    def segsum(x):
        hi, lo = _split(x)
        dg = functools.partial(jnp.dot, preferred_element_type=F32)
        return dg(hi, ones_bd) + dg(lo, ones_bd)

    log_w = -_softplus(-(w0_ref[...] + _mm3(jnp.tanh(x4), w2_ref[...]))) - 0.5
    logw = -jnp.exp(log_w)
    a = _sigmoid(a0_ref[...] + _mm3(x4, a2_ref[...]))
    g = _mm3(_sigmoid(x4), g2_ref[...])
    kk = k * kk_ref[...]
    kk = kk * lax.rsqrt(segsum(kk * kk) + L2_EPS)
    k = k * (1.0 + (a - 1.0) * ka_ref[...])
    bonus = segsum(r * k * rk_ref[...]) * v
    kka = kk * a

    lane_head = lax.broadcasted_iota(jnp.int32, (1, W), 1) // RW_HEAD
    hmask = [lane_head == h for h in range(RW_HEADS)]

    def stack(x):
        return jnp.concatenate([jnp.where(hmask[h], x, 0.0) for h in range(RW_HEADS)], axis=0)

    ri = lax.broadcasted_iota(jnp.int32, (HC, HC), 0) % C
    ci = lax.broadcasted_iota(jnp.int32, (HC, HC), 1) % C
    strict = ri > ci
    incl = ri >= ci
    eye_hc = lax.broadcasted_iota(jnp.int32, (HC, HC), 0) == lax.broadcasted_iota(jnp.int32, (HC, HC), 1)
    eye_w = lax.broadcasted_iota(jnp.int32, (W, W), 0) == lax.broadcasted_iota(jnp.int32, (W, W), 1)
    crow = lax.broadcasted_iota(jnp.int32, (R, W), 0) % C

    mm_gram = functools.partial(_mm3, passes=RW_PASSES_GRAM)
    mm_inv = functools.partial(_mm3, passes=RW_PASSES_INV)
    mm_loc = functools.partial(_mm3, passes=RW_PASSES_LOCAL)
    mm_state = functools.partial(_mm3, passes=RW_PASSES_STATE)

    cum = logw
    d = 1
    while d < C:
        cum = cum + jnp.where(crow >= d, _shift_rows(cum, d), 0.0)
        d *= 2
    nch = R // C
    chunks = range(nch)
    sls = [slice(c * C, (c + 1) * C) for c in chunks]
    ends = [cum[(c + 1) * C - 1:(c + 1) * C, :] for c in chunks]
    cl = jnp.concatenate([jnp.broadcast_to(e, (C, W)) for e in ends], axis=0) if nch > 1 \
        else jnp.broadcast_to(ends[0], (C, W))
    w_inc = jnp.exp(cum)
    w_inv = jnp.exp(-cum)
    w_rem = jnp.exp(cl - cum)
    t_a = jnp.exp(cum - logw) * kk
    t_b = kka * w_inv
    t_k = k * w_inv
    t_r = r * w_inc
    t_kt = k * w_rem
    t_bt = kka * w_rem

    a_s = [stack(t_a[sl]) for sl in sls]
    r_s = [stack(t_r[sl]) for sl in sls]
    v_s = [stack(v[sl]) for sl in sls]
    kt_s = [stack(t_kt[sl]) for sl in sls]
    bt_s = [stack(t_bt[sl]) for sl in sls]
    bk = [jnp.concatenate([stack(t_b[sl]), stack(t_k[sl])], axis=0) for sl in sls]
    g_a = [mm_gram(a_s[c], bk[c], _NT) for c in chunks]
    g_r = [mm_gram(r_s[c], bk[c], _NT) for c in chunks]
    n = [jnp.where(strict, -g[:, :HC], 0.0) for g in g_a]
    l_k = [jnp.where(strict, g[:, HC:], 0.0) for g in g_a]
    m_b = [jnp.where(incl, g[:, :HC], 0.0) for g in g_r]
    m_k = [jnp.where(incl, g[:, HC:], 0.0) for g in g_r]

    tinv = [jnp.where(eye_hc, 1.0, 0.0) + x for x in n]
    pw = n
    d = 2
    while d < C:
        pw = [mm_inv(x, x) for x in pw]
        tinv = [tinv[c] + mm_inv(tinv[c], pw[c]) for c in chunks]
        d *= 2

    x1 = [mm_loc(l_k[c], v_s[c]) for c in chunks]
    tu = [mm_loc(tinv[c], jnp.concatenate([x1[c], a_s[c]], axis=1)) for c in chunks]
    mb = [mm_loc(m_b[c], tu[c]) for c in chunks]
    y_loc = [mm_loc(m_k[c], v_s[c]) - mb[c][:, :W] for c in chunks]
    r_t = [r_s[c] - mb[c][:, W:] for c in chunks]
    p_m = [jnp.where(eye_w, jnp.exp(ends[c]), 0.0) - mm_loc(tu[c][:, W:], bt_s[c], _TN) for c in chunks]
    q_m = [mm_loc(v_s[c], kt_s[c], _TN) - mm_loc(tu[c][:, :W], bt_s[c], _TN) for c in chunks]

    st = s_ref[...]
    ys = []
    for c in chunks:
        y_st = mm_state(r_t[c], st, _NT) + y_loc[c]
        yc = y_st[0:C]
        for h in range(1, RW_HEADS):
            yc = yc + y_st[h * C:(h + 1) * C]
        ys.append(yc)
        st = mm_state(st, p_m[c]) + q_m[c]
    s_ref[...] = st

    y = jnp.concatenate(ys, axis=0) if len(ys) > 1 else ys[0]
    state_ref[0] = s_ref[...]
    inv_n = 1.0 / RW_HEAD
    mean = segsum(y) * inv_n
    yc = y - mean
    var = segsum(yc * yc) * inv_n
    yn = yc * lax.rsqrt(var + GN_EPS) * gnw_ref[...] + gnb_ref[...]
    y_ref[0] = ((yn + bonus) * g).astype(BF)


def _rwkv(prw, shift0, state_bd, mu, w0, w2p, a0, a2p, g2p, k_k, k_a, r_k, gn_w, gn_b, ones_bd):
    B, S, _ = prw.shape
    R = min(RW_TILE, S)
    C = min(RW_CHUNK, S)
    vec = lambda b, t: (0, 0)
    wv = pl.BlockSpec((1, RW_WIDTH), vec)
    lora = pl.BlockSpec((LANES, RW_WIDTH), vec)
    return pl.pallas_call(
        functools.partial(_rwkv_kernel, R=R, C=C),
        grid=(B, S // R),
        in_specs=[
            pl.BlockSpec((1, R, RW_IN), lambda b, t: (b, t, 0)),
            pl.BlockSpec((1, 1, RW_IN), lambda b, t: (b, 0, 0)),
            pl.BlockSpec((1, RW_WIDTH, RW_WIDTH), lambda b, t: (b, 0, 0)),
            pl.BlockSpec((1, RW_IN), vec),
            wv, lora, wv, lora, lora, wv, wv, wv, wv, wv,
            pl.BlockSpec((RW_WIDTH, RW_WIDTH), vec),
        ],
        out_specs=[
            pl.BlockSpec((1, R, RW_WIDTH), lambda b, t: (b, t, 0)),
            pl.BlockSpec((1, 1, RW_IN), lambda b, t: (b, 0, 0)),
            pl.BlockSpec((1, RW_WIDTH, RW_WIDTH), lambda b, t: (b, 0, 0)),
        ],
        out_shape=[
            jax.ShapeDtypeStruct((B, S, RW_WIDTH), BF),
            jax.ShapeDtypeStruct((B, 1, RW_IN), F32),
            jax.ShapeDtypeStruct((B, RW_WIDTH, RW_WIDTH), F32),
        ],
        scratch_shapes=[pltpu.VMEM((1, RW_IN), F32), pltpu.VMEM((RW_WIDTH, RW_WIDTH), F32)],
        compiler_params=_cparams("parallel", "arbitrary"),
        name="rwkv7",
    )(prw, shift0, state_bd, mu, w0, w2p, a0, a2p, g2p, k_k, k_a, r_k, gn_w, gn_b, ones_bd)


def _attn_causal_kernel(q_ref, k_ref, wuv_ref, o_ref, m_ref, acc_ref, *, tq):
    qi = pl.program_id(1)
    q = q_ref[0].reshape(MLA_HEADS * tq, 256)
    m_ref[...] = jnp.full(m_ref.shape, -jnp.inf, F32)
    acc_ref[...] = jnp.zeros(acc_ref.shape, F32)

    def step(kblk, masked):
        s = lax.dot_general(kblk, q, _NT, preferred_element_type=F32)
        if masked:
            kc = lax.broadcasted_iota(jnp.int32, s.shape, 0) // CHUNK
            qc = (lax.broadcasted_iota(jnp.int32, s.shape, 1) % tq) // CHUNK
            s = jnp.where(kc <= qc, s, -jnp.inf)
        m_prev = m_ref[...]
        m_new = jnp.maximum(m_prev, jnp.max(s, axis=0, keepdims=True))
        alpha = jnp.exp2(m_prev - m_new)
        p = jnp.exp2(s - m_new).astype(BF)
        pv = lax.dot_general(kblk[:, :PV_ROWS], p, _TN, preferred_element_type=F32)
        acc_ref[...] = alpha * acc_ref[...] + pv
        m_ref[...] = m_new

    def body(kj, carry):
        step(k_ref[0, pl.ds(pl.multiple_of(kj * (2 * tq), 2 * tq), 2 * tq), :], False)
        return carry

    lax.fori_loop(0, qi // 2, body, 0)

    @pl.when(qi % 2 == 1)
    def _():
        step(k_ref[0, pl.ds(pl.multiple_of((qi - 1) * tq, tq), tq), :], False)

    step(k_ref[0, pl.ds(pl.multiple_of(qi * tq, tq), tq), :], True)
    o = (acc_ref[0:KV_LORA, :] / acc_ref[KV_LORA:KV_LORA + 1, :]).astype(BF)
    for hp in range(MLA_HEADS // 2):
        yt = [jnp.dot(wuv_ref[h], o[:, h * tq:(h + 1) * tq], preferred_element_type=F32)
              for h in (2 * hp, 2 * hp + 1)]
        o_ref[0, :, hp * LANES:(hp + 1) * LANES] = jnp.concatenate(yt, axis=0).T.astype(BF)


def _attn_causal(qcat, kcat, wuv_t):
    B, H, S, _ = qcat.shape
    tq = min(ATT_TILE, S)
    return pl.pallas_call(
        functools.partial(_attn_causal_kernel, tq=tq),
        grid=(B, S // tq),
        in_specs=[
            pl.BlockSpec((1, H, tq, 256), lambda b, i: (b, 0, i, 0)),
            pl.BlockSpec((1, S, 256), lambda b, i: (b, 0, 0)),
            pl.BlockSpec((H, V_HEAD, KV_LORA), lambda b, i: (0, 0, 0)),
        ],
        out_specs=pl.BlockSpec((1, tq, H * V_HEAD), lambda b, i: (b, i, 0)),
        out_shape=jax.ShapeDtypeStruct((B, S, H * V_HEAD), BF),
        scratch_shapes=[pltpu.VMEM((1, H * tq), F32), pltpu.VMEM((PV_ROWS, H * tq), F32)],
        compiler_params=_cparams("parallel", "arbitrary"),
        name="mla_causal",
    )(qcat, kcat, wuv_t)


def _attn_full_kernel(q_ref, ckv_ref, kr_ref, kn_ref, wuv_ref, o_ref, *, S):
    q = q_ref[0].reshape(MLA_HEADS * S, 256)
    ckv = ckv_ref[0].astype(BF)
    kr = kr_ref[0].astype(BF)
    kn = kn_ref[0]
    sel = (lax.broadcasted_iota(jnp.int32, (LANES, QK_ROPE), 0)
           == lax.broadcasted_iota(jnp.int32, (LANES, QK_ROPE), 1) + ROPE_LANE0)
    q_rope = jnp.dot(q[:, LANES:], jnp.where(sel, 1.0, 0.0).astype(BF),
                     preferred_element_type=F32).astype(BF)
    s1 = (lax.dot_general(q[:, :LANES], ckv, _NT, preferred_element_type=F32)
          + lax.dot_general(q_rope, kr, _NT, preferred_element_type=F32))
    s2 = lax.dot_general(q, kn, _NT, preferred_element_type=F32)
    m = jnp.maximum(jnp.max(s1, axis=-1, keepdims=True), jnp.max(s2, axis=-1, keepdims=True))
    p1 = jnp.exp2(s1 - m)
    p2 = jnp.exp2(s2 - m)
    l = jnp.sum(p1, axis=-1, keepdims=True) + jnp.sum(p2, axis=-1, keepdims=True)
    o = ((jnp.dot(p1.astype(BF), ckv, preferred_element_type=F32)
          + jnp.dot(p2.astype(BF), kn[:, :KV_LORA], preferred_element_type=F32)) / l).astype(BF)
    y = [jnp.dot(o[h * S:(h + 1) * S], wuv_ref[h], preferred_element_type=F32)
         for h in range(MLA_HEADS)]
    o_ref[0] = jnp.concatenate(y, axis=-1).astype(BF)


def _attn_full(qcat, past_ckv, past_kr, knew, wuv_h, l):
    B, H, S, _ = qcat.shape
    P = past_ckv.shape[2]
    return pl.pallas_call(
        functools.partial(_attn_full_kernel, S=S),
        grid=(B,),
        in_specs=[
            pl.BlockSpec((1, H, S, 256), lambda b: (b, 0, 0, 0)),
            pl.BlockSpec((None, 1, P, KV_LORA), lambda b: (l, b, 0, 0)),
            pl.BlockSpec((None, 1, P, QK_ROPE), lambda b: (l, b, 0, 0)),
            pl.BlockSpec((1, S, 256), lambda b: (b, 0, 0)),
            pl.BlockSpec((H, KV_LORA, V_HEAD), lambda b: (0, 0, 0)),
        ],
        out_specs=pl.BlockSpec((1, S, H * V_HEAD), lambda b: (b, 0, 0)),
        out_shape=jax.ShapeDtypeStruct((B, S, H * V_HEAD), BF),
        compiler_params=_cparams("parallel"),
        name="mla_full",
    )(qcat, past_ckv, past_kr, knew, wuv_h)


def _out_kernel(x_ref, yrg_ref, yrw_ref, ymla_ref, wout_ref, o_ref):
    ycat = jnp.concatenate([yrg_ref[...], yrw_ref[...], ymla_ref[...]], axis=-1)
    o_ref[...] = x_ref[...] + jnp.dot(ycat, wout_ref[...], preferred_element_type=F32)


def _out_proj(x2d, yrg, yrw, ymla, wout):
    T = x2d.shape[0]
    tm = min(ROW_TILE, T)
    row = lambda i: (i, 0)
    const = lambda i: (0, 0)
    return pl.pallas_call(
        _out_kernel,
        grid=(T // tm,),
        in_specs=[
            pl.BlockSpec((tm, D_MODEL), row),
            pl.BlockSpec((tm, RG_WIDTH), row),
            pl.BlockSpec((tm, RW_WIDTH), row),
            pl.BlockSpec((tm, MLA_HEADS * V_HEAD), row),
            pl.BlockSpec((D_MODEL, D_MODEL), const),
        ],
        out_specs=pl.BlockSpec((tm, D_MODEL), row),
        out_shape=jax.ShapeDtypeStruct((T, D_MODEL), F32),
        compiler_params=_cparams("parallel"),
        name="out_proj",
    )(x2d, yrg, yrw, ymla, wout)


def _mem_kernel(x_ref, g_ref, wq_ref, mk_ref, mv_ref, wo_ref, o_ref):
    x = x_ref[...]
    hn = _rms(x, g_ref[...]).astype(BF)
    q = jnp.dot(hn, wq_ref[...], preferred_element_type=F32).astype(BF)
    by_head = len(mk_ref.shape) == 4
    outs = []
    for h in range(MEM_HEADS):
        sl = slice(h * MEM_HEAD, (h + 1) * MEM_HEAD)
        mk_h = mk_ref[0, :, h, :].astype(BF) if by_head else mk_ref[0, :, sl]
        mv_h = mv_ref[0, :, h, :].astype(BF) if by_head else mv_ref[0, :, sl]
        s = lax.dot_general(q[:, sl], mk_h, _NT, preferred_element_type=F32) * MEM_SCALE
        e = jnp.exp(s - jnp.max(s, axis=-1, keepdims=True))
        pr = e / jnp.sum(e, axis=-1, keepdims=True)
        outs.append(jnp.dot(pr.astype(BF), mv_h, preferred_element_type=F32).astype(BF))
    o = jnp.concatenate(outs, axis=-1)
    o_ref[...] = x + jnp.dot(o, wo_ref[...], preferred_element_type=F32)


def _mem_attn(x2d, B, S, g, wq, mk, mv, wo, l):
    T = B * S
    tm = min(ROW_TILE, S)
    nst = S // tm
    row = lambda i: (i, 0)
    const = lambda i: (0, 0)
    if mk.ndim == 5:
        mem_spec = pl.BlockSpec((None, 1, N_MEM, MEM_HEADS, MEM_HEAD),
                                lambda i: (l, i // nst, 0, 0, 0))
    else:
        mem_spec = pl.BlockSpec((1, N_MEM, D_MODEL), lambda i: (i // nst, 0, 0))
    return pl.pallas_call(
        _mem_kernel,
        grid=(T // tm,),
        in_specs=[
            pl.BlockSpec((tm, D_MODEL), row),
            pl.BlockSpec((1, D_MODEL), const),
            pl.BlockSpec((D_MODEL, D_MODEL), const),
            mem_spec,
            mem_spec,
            pl.BlockSpec((D_MODEL, D_MODEL), const),
        ],
        out_specs=pl.BlockSpec((tm, D_MODEL), row),
        out_shape=jax.ShapeDtypeStruct((T, D_MODEL), F32),
        compiler_params=_cparams("parallel"),
        name="mem_attn",
    )(x2d, g, wq, mk, mv, wo)


def _ffn_kernel(x_ref, g_ref, w1_ref, w2_ref, gf_ref, o_ref, *, final):
    x = x_ref[...]
    xn = _rms(x, g_ref[...]).astype(BF)
    acc = x
    for c in range(D_FF // FF_TILE):
        h = jnp.dot(xn, w1_ref[:, c * FF_TILE:(c + 1) * FF_TILE], preferred_element_type=F32)
        h = jnp.square(jnp.maximum(h, 0.0)).astype(BF)
        acc = acc + jnp.dot(h, w2_ref[c * FF_TILE:(c + 1) * FF_TILE, :], preferred_element_type=F32)
    if final:
        acc = _rms(acc, gf_ref[...])
    o_ref[...] = acc


def _ffn(x2d, g, w1, w2, gf, final):
    T = x2d.shape[0]
    tm = min(FFN_ROW_TILE, T)
    const = lambda i: (0, 0)
    return pl.pallas_call(
        functools.partial(_ffn_kernel, final=final),
        grid=(T // tm,),
        in_specs=[
            pl.BlockSpec((tm, D_MODEL), lambda i: (i, 0)),
            pl.BlockSpec((1, D_MODEL), const),
            pl.BlockSpec((D_MODEL, D_FF), const, pipeline_mode=pl.Buffered(1)),
            pl.BlockSpec((D_FF, D_MODEL), const, pipeline_mode=pl.Buffered(1)),
            pl.BlockSpec((1, D_MODEL), const),
        ],
        out_specs=pl.BlockSpec((tm, D_MODEL), lambda i: (i, 0)),
        out_shape=jax.ShapeDtypeStruct((T, D_MODEL), F32),
        compiler_params=_cparams("parallel"),
        name="ffn",
    )(x2d, g, w1, w2, gf)


def _memproj_kernel(*refs, n_prev):
    m_ref, g_ref, w_ref = refs[:3]
    k_ref, v_ref, kb_ref, vb_ref = refs[3 + n_prev:]
    mn = _rms(m_ref[...], g_ref[...]).astype(BF)
    kv = jnp.dot(mn, w_ref[...], preferred_element_type=F32)
    k = kv[:, :D_MODEL]
    v = kv[:, D_MODEL:]
    for h in range(MEM_HEADS):
        k_ref[0, :, h, :] = k[:, h * MEM_HEAD:(h + 1) * MEM_HEAD]
        v_ref[0, :, h, :] = v[:, h * MEM_HEAD:(h + 1) * MEM_HEAD]
    kb_ref[...] = k.astype(BF)
    vb_ref[...] = v.astype(BF)


def _mem_project(mem2d, g, wkv, l, depth, prev):
    T = mem2d.shape[0]
    tm = min(ROW_TILE, T)
    row = lambda i: (i, 0)
    const = lambda i: (0, 0)
    n_prev = 0 if prev is None else len(prev)
    slot = pl.BlockSpec((1, tm, MEM_HEADS, MEM_HEAD), lambda i: (l, i, 0, 0))
    return pl.pallas_call(
        functools.partial(_memproj_kernel, n_prev=n_prev),
        grid=(T // tm,),
        in_specs=[
            pl.BlockSpec((tm, D_MODEL), row),
            pl.BlockSpec((1, D_MODEL), const),
            pl.BlockSpec((D_MODEL, 2 * D_MODEL), const),
        ] + [pl.BlockSpec(memory_space=pl.ANY)] * n_prev,
        out_specs=[slot, slot, pl.BlockSpec((tm, D_MODEL), row), pl.BlockSpec((tm, D_MODEL), row)],
        out_shape=[jax.ShapeDtypeStruct((depth, T, MEM_HEADS, MEM_HEAD), F32)] * 2
                  + [jax.ShapeDtypeStruct((T, D_MODEL), BF)] * 2,
        input_output_aliases={3 + j: j for j in range(n_prev)},
        compiler_params=_cparams("parallel"),
        name="mem_project",
    )(mem2d, g, wkv, *([] if prev is None else prev))


def _fold_kernel(a_ref, b_ref, o_ref):
    o_ref[0] = _mm3(a_ref[0], b_ref[0])


def _fold_heads(a, b):
    G, m, k = a.shape
    n = b.shape[2]
    return pl.pallas_call(
        _fold_kernel,
        grid=(G,),
        in_specs=[pl.BlockSpec((1, m, k), lambda g: (g, 0, 0)),
                  pl.BlockSpec((1, k, n), lambda g: (g, 0, 0))],
        out_specs=pl.BlockSpec((1, m, n), lambda g: (g, 0, 0)),
        out_shape=jax.ShapeDtypeStruct((G, m, n), F32),
        compiler_params=_cparams("parallel"),
        name="fold_heads",
    )(a, b)


def _block_diag(blocks):
    G, m, n = blocks.shape
    eye = jnp.eye(G, dtype=blocks.dtype)
    return jnp.einsum('gmn,gh->gmhn', blocks, eye).reshape(G * m, G * n)


def _prep_weights(l, W):
    r2 = lambda v: v.reshape(1, -1)
    w_in = W['w_in'][l]
    kr_cols = w_in[:, 1792:1824]
    kr_swap = jnp.concatenate([kr_cols[:, 16:], kr_cols[:, :16]], axis=1)
    w_ext = jnp.concatenate([w_in[:, :1792], jnp.tile(kr_cols, (1, 4)), jnp.tile(kr_swap, (1, 4))],
                            axis=1).astype(BF)
    w_uq = W['mla_w_uq'][l]
    rope = w_uq[:, :, QK_NOPE:]
    rope_swap = jnp.concatenate([rope[:, :, 16:], rope[:, :, :16]], axis=2)
    w_fold = _fold_heads(jnp.transpose(w_uq[:, :, :QK_NOPE], (1, 0, 2)),
                         jnp.transpose(W['mla_w_uk'][l], (1, 2, 0)))
    wq_ext = jnp.concatenate([jnp.transpose(w_fold, (1, 0, 2)).reshape(Q_LORA, -1),
                              rope.reshape(Q_LORA, -1), rope_swap.reshape(Q_LORA, -1)],
                             axis=1).astype(BF)
    w_uv = W['mla_w_uv'][l]
    wuv_t = jnp.transpose(w_uv, (1, 2, 0)).astype(BF)
    wuv_h = jnp.transpose(w_uv, (1, 0, 2)).astype(BF)
    zpad = lambda top, mat, bot: jnp.concatenate(
        [jnp.zeros((top, RW_WIDTH), F32), mat, jnp.zeros((bot, RW_WIDTH), F32)], axis=0)
    return dict(
        norm_mix=r2(W['norm_mix'][l]), w_ext=w_ext, gq=r2(W['mla_q_norm'][l]),
        gkv=r2(W['mla_kv_norm'][l]), wq_ext=wq_ext, wuv_t=wuv_t, wuv_h=wuv_h,
        cw=W['rg_conv_w'][l], cb=r2(W['rg_conv_b'][l]),
        wa_bd=_block_diag(W['rg_wa'][l]).astype(BF), ba=r2(W['rg_ba'][l]),
        wx_bd=_block_diag(W['rg_wx'][l]).astype(BF), bx=r2(W['rg_bx'][l]),
        lam=r2(W['rg_lambda'][l]),
        mu=r2(W['rw_mu'][l]), w0=r2(W['rw_w0'][l]), w2p=zpad(0, W['rw_w2'][l], 96),
        a0=r2(W['rw_a0'][l]), a2p=zpad(32, W['rw_a2'][l], 64), g2p=zpad(64, W['rw_g2'][l], 0),
        k_k=r2(W['rw_k_k'][l]), k_a=r2(W['rw_k_a'][l]), r_k=r2(W['rw_r_k'][l]),
        gn_w=r2(W['rw_gn_w'][l]), gn_b=r2(W['rw_gn_b'][l]),
        w_out=W['w_out'][l].astype(BF),
        norm_mem=r2(W['norm_mem'][l]), wq=W['mem_w_q'][l].reshape(D_MODEL, D_MODEL).astype(BF),
        wo=W['mem_w_o'][l].reshape(D_MODEL, D_MODEL).astype(BF),
        norm_ffn=r2(W['norm_ffn'][l]), w1=W['ffn_w1'][l].astype(BF), w2=W['ffn_w2'][l].astype(BF),
        norm_mem_kv=r2(W['norm_mem_kv'][l]),
        wkv=jnp.concatenate([W['mem_w_k'][l].reshape(D_MODEL, D_MODEL),
                             W['mem_w_v'][l].reshape(D_MODEL, D_MODEL)], axis=1).astype(BF),
    )


def _rope_tables(pos, rows):
    half = QK_ROPE // 2
    inv = ROPE_BASE ** (-jnp.arange(half, dtype=F32) / half)
    ang = pos.astype(F32)[:, None] * inv
    cos, sin = jnp.cos(ang), jnp.sin(ang)
    cosq = jnp.tile(jnp.concatenate([cos, cos], axis=1), (1, MLA_HEADS))
    sinq = jnp.tile(jnp.concatenate([-sin, sin], axis=1), (1, MLA_HEADS))
    reps = max(1, rows // pos.shape[0])
    return jnp.tile(cosq, (reps, 1)), jnp.tile(sinq, (reps, 1))


def _layer(x2d, B, S, P, tabs, rg_buf, rg_h, rw_shift, rw_state, past, mem_k, mem_v, gf, l, depth,
           prev_kv):
    cosq, sinq = tabs
    final = l == depth - 1
    pa, prw, ckv, kr, kcat, qcat = _in_proj(x2d, B, S, P['norm_mix'], P['w_ext'], P['gq'], P['gkv'],
                                            P['wq_ext'], cosq, sinq, l, depth, prev_kv)
    buf8 = jnp.concatenate([jnp.zeros((B, 5, RG_WIDTH), F32), rg_buf], axis=1)
    yrg, buf_new, h_new = _rglru(pa.reshape(B, S, 512), buf8, rg_h.reshape(B, 1, RG_WIDTH),
                                 P['cw'], P['cb'], P['wa_bd'], P['ba'], P['wx_bd'], P['bx'], P['lam'])
    eye_h = jnp.eye(RW_HEADS, dtype=F32)
    state_bd = jnp.einsum('bhij,hg->bhigj', rw_state, eye_h).reshape(B, RW_WIDTH, RW_WIDTH)
    ones_bd = _block_diag(jnp.ones((RW_HEADS, RW_HEAD, RW_HEAD), F32)).astype(BF)
    yrw, shift_new, state_new = _rwkv(prw.reshape(B, S, RW_IN), rw_shift.reshape(B, 1, RW_IN), state_bd,
                                      P['mu'], P['w0'], P['w2p'], P['a0'], P['a2p'], P['g2p'],
                                      P['k_k'], P['k_a'], P['r_k'], P['gn_w'], P['gn_b'], ones_bd)
    wkv_new = jnp.einsum('bhigj,hg->bhij',
                         state_new.reshape(B, RW_HEADS, RW_HEAD, RW_HEADS, RW_HEAD), eye_h)
    kcat3 = kcat.reshape(B, S, 256)
    if past is None:
        ymla = _attn_causal(qcat, kcat3, P['wuv_t'])
    else:
        ymla = _attn_full(qcat, past[0], past[1], kcat3, P['wuv_h'], l)
    x1 = _out_proj(x2d, yrg.reshape(B * S, RG_WIDTH), yrw.reshape(B * S, RW_WIDTH),
                   ymla.reshape(B * S, MLA_HEADS * V_HEAD), P['w_out'])
    x2 = _mem_attn(x1, B, S, P['norm_mem'], P['wq'], mem_k, mem_v, P['wo'], l)
    x3 = _ffn(x2, P['norm_ffn'], P['w1'], P['w2'], gf, final)
    new_state = (buf_new[:, 5:8], h_new.reshape(B, RG_WIDTH), shift_new.reshape(B, RW_IN), wkv_new)
    return x3, new_state, (ckv, kr)


def kernel(x_prompt, x_sample, state_rg_conv, state_rg_h, state_rw_shift, state_rw_wkv, cache_mla_ckv, cache_mla_krope, cache_mem_k, cache_mem_v, mem_prompt, norm_mix, w_in, rg_conv_w, rg_conv_b, rg_wa, rg_ba, rg_wx, rg_bx, rg_lambda, rw_mu, rw_w0, rw_w2, rw_a0, rw_a2, rw_g2, rw_k_k, rw_k_a, rw_r_k, rw_gn_w, rw_gn_b, mla_q_norm, mla_kv_norm, mla_w_uq, mla_w_uk, mla_w_uv, w_out, norm_mem, norm_mem_kv, mem_w_q, mem_w_k, mem_w_v, mem_w_o, norm_ffn, ffn_w1, ffn_w2, norm_final):
    W = dict(norm_mix=norm_mix, w_in=w_in, rg_conv_w=rg_conv_w, rg_conv_b=rg_conv_b, rg_wa=rg_wa,
             rg_ba=rg_ba, rg_wx=rg_wx, rg_bx=rg_bx, rg_lambda=rg_lambda, rw_mu=rw_mu, rw_w0=rw_w0,
             rw_w2=rw_w2, rw_a0=rw_a0, rw_a2=rw_a2, rw_g2=rw_g2, rw_k_k=rw_k_k, rw_k_a=rw_k_a,
             rw_r_k=rw_r_k.reshape(rw_r_k.shape[0], RW_WIDTH), rw_gn_w=rw_gn_w, rw_gn_b=rw_gn_b,
             mla_q_norm=mla_q_norm, mla_kv_norm=mla_kv_norm, mla_w_uq=mla_w_uq, mla_w_uk=mla_w_uk,
             mla_w_uv=mla_w_uv, w_out=w_out, norm_mem=norm_mem, norm_mem_kv=norm_mem_kv,
             mem_w_q=mem_w_q, mem_w_k=mem_w_k, mem_w_v=mem_w_v, mem_w_o=mem_w_o, norm_ffn=norm_ffn,
             ffn_w1=ffn_w1, ffn_w2=ffn_w2)
    depth = norm_mix.shape[0]
    Bp, Sp, _ = x_prompt.shape
    Bs, Ss, _ = x_sample.shape
    past_len = cache_mla_ckv.shape[2]
    tabs_p = _rope_tables(jnp.arange(Sp, dtype=jnp.int32), min(ROW_TILE, Sp))
    tabs_s = _rope_tables(past_len + jnp.arange(Ss, dtype=jnp.int32), min(ROW_TILE, Ss))
    gf = norm_final.reshape(1, D_MODEL)

    xp = x_prompt.reshape(Bp * Sp, D_MODEL)
    xs = x_sample.reshape(Bs * Ss, D_MODEL)
    z = lambda *s: jnp.zeros(s, F32)
    p_states, s_states = [], []
    mem_kv = kv_p = kv_s = None
    for l in range(depth):
        P = _prep_weights(l, W)
        mk, mv, mkb, mvb = _mem_project(mem_prompt.reshape(Bp * N_MEM, D_MODEL), P['norm_mem_kv'],
                                        P['wkv'], l, depth, mem_kv)
        mem_kv = (mk, mv)
        xp, st_p, kv_p = _layer(xp, Bp, Sp, P, tabs_p, z(Bp, 3, RG_WIDTH), z(Bp, RG_WIDTH),
                                z(Bp, RW_IN), z(Bp, RW_HEADS, RW_HEAD, RW_HEAD), None,
                                mkb.reshape(Bp, N_MEM, D_MODEL), mvb.reshape(Bp, N_MEM, D_MODEL),
                                gf, l, depth, kv_p)
        xs, st_s, kv_s = _layer(xs, Bs, Ss, P, tabs_s, state_rg_conv[l], state_rg_h[l],
                                state_rw_shift[l], state_rw_wkv[l],
                                (cache_mla_ckv, cache_mla_krope),
                                cache_mem_k, cache_mem_v, gf, l, depth, kv_s)
        p_states.append(st_p)
        s_states.append(st_s)

    sp = [jnp.stack(t) for t in zip(*p_states)]
    ss = [jnp.stack(t) for t in zip(*s_states)]
    mem_shape = (depth, Bp, N_MEM, MEM_HEADS, MEM_HEAD)
    return (xp.reshape(Bp, Sp, D_MODEL), xs.reshape(Bs, Ss, D_MODEL),
            sp[0], sp[1], sp[2], sp[3],
            kv_p[0].reshape(depth, Bp, Sp, KV_LORA), kv_p[1].reshape(depth, Bp, Sp, QK_ROPE),
            mem_kv[0].reshape(mem_shape), mem_kv[1].reshape(mem_shape),
            ss[0], ss[1], ss[2], ss[3],
            kv_s[0].reshape(depth, Bs, Ss, KV_LORA), kv_s[1].reshape(depth, Bs, Ss, QK_ROPE))
```

```python
import functools
import math

import jax
import jax.numpy as jnp
from jax import lax
from jax.experimental import pallas as pl
from jax.experimental.pallas import tpu as pltpu

BF = jnp.bfloat16
F32 = jnp.float32

D_MODEL = 1024
RG_WIDTH = 256
RG_BLOCKS = 4
CONV_W = 4
LRU_C = 8.0
RW_HEADS = 4
RW_HEAD = 64
RW_WIDTH = 256
RW_IN = 896
GN_EPS = 64e-5
L2_EPS = 1e-12
MLA_HEADS = 8
QK_NOPE = 64
QK_ROPE = 32
V_HEAD = 64
Q_LORA = 256
KV_LORA = 128
MLA_SCALE = (QK_NOPE + QK_ROPE) ** -0.5
SCALE_LOG2E = MLA_SCALE * math.log2(math.e)
ROPE_BASE = 10000.0
CHUNK = 64
N_MEM = 256
MEM_HEADS = 4
MEM_HEAD = 256
MEM_SCALE = MEM_HEAD ** -0.5
D_FF = 4096
EPS = 1e-6

LANES = 128
VMEM_LIMIT = 52 * 1024 * 1024

ROW_TILE = 1024
RG_TILE = 512
RW_TILE = 512
RW_CHUNK = 64
RW_PASSES_GRAM = 1
RW_PASSES_INV = 1
RW_PASSES_LOCAL = 1
RW_PASSES_STATE = 1
ATT_TILE = 512
IN_SPLIT = 2
ROPE_LANE0 = 8
PV_ROWS = 144
MEMPROJ_TILE = 256
FFN_ROW_TILE = 1024
FF_TILE = 2048


def _cparams(*sem):
    return pltpu.CompilerParams(dimension_semantics=sem, vmem_limit_bytes=VMEM_LIMIT)


def _rms(x, g):
    return x * lax.rsqrt(jnp.mean(x * x, axis=-1, keepdims=True) + EPS) * g


def _dot(a, b):
    return jnp.dot(a.astype(BF), b.astype(BF), preferred_element_type=F32)


def _dot_nt(a, b):
    return lax.dot_general(a.astype(BF), b.astype(BF), (((1,), (1,)), ((), ())),
                           preferred_element_type=F32)


def _split(x):
    hi = x.astype(BF)
    lo = (x - hi.astype(F32)).astype(BF)
    return hi, lo


def _mm3(a, b, dims=(((1,), (0,)), ((), ())), passes=3):
    dg = functools.partial(lax.dot_general, dimension_numbers=dims, preferred_element_type=F32)
    if passes == 1:
        return dg(a.astype(BF), b.astype(BF))
    ah, al = _split(a)
    bh, bl = _split(b)
    return dg(ah, bh) + (dg(ah, bl) + dg(al, bh))


_NT = (((1,), (1,)), ((), ()))
_TN = (((0,), (0,)), ((), ()))


def _softplus(x):
    return jnp.maximum(x, 0.0) + jnp.log1p(jnp.exp(-jnp.abs(x)))


def _sigmoid(x):
    return 0.5 * jnp.tanh(0.5 * x) + 0.5


def _gelu_tanh(x):
    c = math.sqrt(2.0 / math.pi)
    return 0.5 * x * (1.0 + jnp.tanh(c * (x + 0.044715 * (x * x * x))))


def _shift_rows(x, d):
    return pltpu.roll(x, d, 0)


def _in_kernel(x_ref, g_ref, w_ref, gq_ref, gkv_ref, wq_ref, cos_ref, sin_ref, *rest, tm, l):
    prev = rest[:2] if l else ()
    pa_ref, prw_ref, ckv_ref, kr_ref, kcat_ref, qcat_ref = rest[len(prev):]
    for src, dst in zip(prev, (ckv_ref, kr_ref)):
        dst[0:l] = src[...]
    lane = lax.broadcasted_iota(jnp.int32, (1, LANES), 1)
    rope_lanes = (lane >= ROPE_LANE0) & (lane < ROPE_LANE0 + QK_ROPE)

    def rows(sl):
        xn = _rms(x_ref[sl, :], g_ref[...]).astype(BF)
        p = jnp.dot(xn, w_ref[...], preferred_element_type=F32)
        pa_ref[sl, :] = p[:, :512]
        prw_ref[sl, :] = p[:, 512:1408]
        cq = p[:, 1408:1664]
        ckv = p[:, 1664:1792]
        ka = p[:, 1792:1920]
        kb = p[:, 1920:2048]
        cos = cos_ref[sl, :]
        sin = sin_ref[sl, :]
        ckvn = _rms(ckv, gkv_ref[...])
        krt = ka * cos[:, :LANES] + kb * sin[:, :LANES]
        ckv_ref[l, sl, :] = ckvn
        kr_ref[l, sl, :] = krt[:, :QK_ROPE]
        kext = jnp.where(lane == 0, 1.0,
                         jnp.where(rope_lanes, pltpu.roll(krt, ROPE_LANE0, 1), 0.0))
        kcat_ref[sl, :] = jnp.concatenate([ckvn, kext], axis=-1).astype(BF)

        cqn = _rms(cq, gq_ref[...]).astype(BF)
        qq = jnp.dot(cqn, wq_ref[...], preferred_element_type=F32)
        qlat = qq[:, :1024]
        rr = qq[:, 1024:1280] * cos + qq[:, 1280:1536] * sin
        for h in range(MLA_HEADS):
            g = h // 4
            shift = (ROPE_LANE0 - QK_ROPE * (h % 4)) % LANES
            rpart = jnp.where(rope_lanes,
                              pltpu.roll(rr[:, g * LANES:(g + 1) * LANES], shift, 1), 0.0)
            qcat_ref[0, h, sl, :LANES] = (qlat[:, h * LANES:(h + 1) * LANES] * SCALE_LOG2E).astype(BF)
            qcat_ref[0, h, sl, LANES:] = (rpart * SCALE_LOG2E).astype(BF)

    step = tm // IN_SPLIT if tm % (16 * IN_SPLIT) == 0 else tm
    for r0 in range(0, tm, step):
        rows(slice(r0, r0 + step))


def _in_proj(x2d, B, S, g, w_ext, gq, gkv, wq_ext, cosq, sinq, l, prev):
    T = B * S
    prev = () if prev is None else prev
    tm = min(ROW_TILE, S)
    nst = S // tm
    ntab = cosq.shape[0] // tm
    row = lambda i: (i, 0)
    const = lambda i: (0, 0)
    outs = pl.pallas_call(
        functools.partial(_in_kernel, tm=tm, l=l),
        grid=(T // tm,),
        in_specs=[
            pl.BlockSpec((tm, D_MODEL), row),
            pl.BlockSpec((1, D_MODEL), const),
            pl.BlockSpec((D_MODEL, 2048), const),
            pl.BlockSpec((1, Q_LORA), const),
            pl.BlockSpec((1, KV_LORA), const),
            pl.BlockSpec((Q_LORA, 1536), const),
            pl.BlockSpec((tm, 256), lambda i: (i % ntab, 0)),
            pl.BlockSpec((tm, 256), lambda i: (i % ntab, 0)),
        ] + [pl.BlockSpec((l, tm, a.shape[2]), lambda i: (0, i, 0)) for a in prev],
        out_specs=[
            pl.BlockSpec((tm, 512), row),
            pl.BlockSpec((tm, RW_IN), row),
            pl.BlockSpec((l + 1, tm, KV_LORA), lambda i: (0, i, 0)),
            pl.BlockSpec((l + 1, tm, QK_ROPE), lambda i: (0, i, 0)),
            pl.BlockSpec((tm, 256), row),
            pl.BlockSpec((1, MLA_HEADS, tm, 256), lambda i: (i // nst, 0, i % nst, 0)),
        ],
        out_shape=[
            jax.ShapeDtypeStruct((T, 512), F32),
            jax.ShapeDtypeStruct((T, RW_IN), F32),
            jax.ShapeDtypeStruct((l + 1, T, KV_LORA), F32),
            jax.ShapeDtypeStruct((l + 1, T, QK_ROPE), F32),
            jax.ShapeDtypeStruct((T, 256), BF),
            jax.ShapeDtypeStruct((B, MLA_HEADS, S, 256), BF),
        ],
        compiler_params=_cparams("parallel"),
        name="in_proj",
    )(x2d, g, w_ext, gq, gkv, wq_ext, cosq, sinq, *prev)
    return outs


def _rglru_kernel(pa_ref, buf0_ref, h0_ref, cw_ref, cb_ref, wa_ref, ba_ref, wx_ref, bx_ref,
                  lam_ref, y_ref, buf_ref, hout_ref, ext_ref, h_ref, *, R):
    t = pl.program_id(1)

    @pl.when(t == 0)
    def _():
        ext_ref[0:8, :] = buf0_ref[0]
        h_ref[...] = h0_ref[0]

    pa = pa_ref[0]
    xr = pa[:, :RG_WIDTH]
    gate = pa[:, RG_WIDTH:]
    ext_ref[8:8 + R, :] = xr
    cw = cw_ref[...]
    conv = (cb_ref[...] + cw[3:4] * xr + cw[2:3] * ext_ref[7:7 + R, :]
            + cw[1:2] * ext_ref[6:6 + R, :] + cw[0:1] * ext_ref[5:5 + R, :])
    hist = ext_ref[R:R + 8, :]
    ext_ref[0:8, :] = hist
    buf_ref[0] = hist

    ra = _sigmoid(_dot(conv, wa_ref[...]) + ba_ref[...])
    ia = _sigmoid(_dot(conv, wx_ref[...]) + bx_ref[...])
    log_a = (-LRU_C) * ra * _softplus(-lam_ref[...])
    a = jnp.exp(log_a)
    om = -jnp.tanh(log_a) * (a * a + 1.0)
    b = jnp.where(om > 0.0, om * lax.rsqrt(om), 0.0) * (ia * conv)

    rows = lax.broadcasted_iota(jnp.int32, (R, RG_WIDTH), 0)
    d = 1
    while d < R:
        m = rows >= d
        b = jnp.where(m, a * _shift_rows(b, d), 0.0) + b
        a = jnp.where(m, a * _shift_rows(a, d), a)
        d *= 2
    h = a * h_ref[...] + b
    hl = h[R - 1:R, :]
    h_ref[...] = hl
    hout_ref[0] = hl
    y_ref[0] = (_gelu_tanh(gate) * h).astype(BF)


def _rglru(pa, buf8, h0, cw, cb, wa_bd, ba, wx_bd, bx, lam):
    B, S, _ = pa.shape
    R = min(RG_TILE, S)
    vec = lambda b, t: (0, 0)
    return pl.pallas_call(
        functools.partial(_rglru_kernel, R=R),
        grid=(B, S // R),
        in_specs=[
            pl.BlockSpec((1, R, 512), lambda b, t: (b, t, 0)),
            pl.BlockSpec((1, 8, RG_WIDTH), lambda b, t: (b, 0, 0)),
            pl.BlockSpec((1, 1, RG_WIDTH), lambda b, t: (b, 0, 0)),
            pl.BlockSpec((CONV_W, RG_WIDTH), vec),
            pl.BlockSpec((1, RG_WIDTH), vec),
            pl.BlockSpec((RG_WIDTH, RG_WIDTH), vec),
            pl.BlockSpec((1, RG_WIDTH), vec),
            pl.BlockSpec((RG_WIDTH, RG_WIDTH), vec),
            pl.BlockSpec((1, RG_WIDTH), vec),
            pl.BlockSpec((1, RG_WIDTH), vec),
        ],
        out_specs=[
            pl.BlockSpec((1, R, RG_WIDTH), lambda b, t: (b, t, 0)),
            pl.BlockSpec((1, 8, RG_WIDTH), lambda b, t: (b, 0, 0)),
            pl.BlockSpec((1, 1, RG_WIDTH), lambda b, t: (b, 0, 0)),
        ],
        out_shape=[
            jax.ShapeDtypeStruct((B, S, RG_WIDTH), BF),
            jax.ShapeDtypeStruct((B, 8, RG_WIDTH), F32),
            jax.ShapeDtypeStruct((B, 1, RG_WIDTH), F32),
        ],
        scratch_shapes=[pltpu.VMEM((R + 8, RG_WIDTH), F32), pltpu.VMEM((1, RG_WIDTH), F32)],
        compiler_params=_cparams("parallel", "arbitrary"),
        name="rglru",
    )(pa, buf8, h0, cw, cb, wa_bd, ba, wx_bd, bx, lam)


def _rwkv_kernel(p_ref, shift0_ref, state0_ref, mu_ref, w0_ref, w2_ref, a0_ref, a2_ref, g2_ref,
                 kk_ref, ka_ref, rk_ref, gnw_ref, gnb_ref, ones_ref,
                 y_ref, shift_ref, state_ref, prev_ref, s_ref, *, R, C):
    t = pl.program_id(1)
    W = RW_WIDTH
    HC = RW_HEADS * C

    @pl.when(t == 0)
    def _():
        prev_ref[...] = shift0_ref[0]
        s_ref[...] = state0_ref[0]

    p = p_ref[0]
    rows = lax.broadcasted_iota(jnp.int32, (R, RW_IN), 0)
    prev = jnp.where(rows == 0, prev_ref[...], _shift_rows(p, 1))
    last = p[R - 1:R, :]
    prev_ref[...] = last
    shift_ref[0] = last
    xs = p + mu_ref[...] * (prev - p)
    r = xs[:, 0:W]
    k = xs[:, W:2 * W]
    v = xs[:, 2 * W:3 * W]
    x4 = xs[:, 3 * W:]
    ones_bd = ones_ref[...]

    def segsum(x):
        hi, lo = _split(x)
        dg = functools.partial(jnp.dot, preferred_element_type=F32)
        return dg(hi, ones_bd) + dg(lo, ones_bd)

    log_w = -_softplus(-(w0_ref[...] + _mm3(jnp.tanh(x4), w2_ref[...]))) - 0.5
    logw = -jnp.exp(log_w)
    a = _sigmoid(a0_ref[...] + _mm3(x4, a2_ref[...]))
    g = _mm3(_sigmoid(x4), g2_ref[...])
    kk = k * kk_ref[...]
    kk = kk * lax.rsqrt(segsum(kk * kk) + L2_EPS)
    k = k * (1.0 + (a - 1.0) * ka_ref[...])
    bonus = segsum(r * k * rk_ref[...]) * v
    kka = kk * a

    lane_head = lax.broadcasted_iota(jnp.int32, (1, W), 1) // RW_HEAD
    hmask = [lane_head == h for h in range(RW_HEADS)]

    def stack(x):
        return jnp.concatenate([jnp.where(hmask[h], x, 0.0) for h in range(RW_HEADS)], axis=0)

    ri = lax.broadcasted_iota(jnp.int32, (HC, HC), 0) % C
    ci = lax.broadcasted_iota(jnp.int32, (HC, HC), 1) % C
    strict = ri > ci
    incl = ri >= ci
    eye_hc = lax.broadcasted_iota(jnp.int32, (HC, HC), 0) == lax.broadcasted_iota(jnp.int32, (HC, HC), 1)
    eye_w = lax.broadcasted_iota(jnp.int32, (W, W), 0) == lax.broadcasted_iota(jnp.int32, (W, W), 1)
    crow = lax.broadcasted_iota(jnp.int32, (R, W), 0) % C

    mm_gram = functools.partial(_mm3, passes=RW_PASSES_GRAM)
    mm_inv = functools.partial(_mm3, passes=RW_PASSES_INV)
    mm_loc = functools.partial(_mm3, passes=RW_PASSES_LOCAL)
    mm_state = functools.partial(_mm3, passes=RW_PASSES_STATE)

    cum = logw
    d = 1
    while d < C:
        cum = cum + jnp.where(crow >= d, _shift_rows(cum, d), 0.0)
        d *= 2
    nch = R // C
    chunks = range(nch)
    sls = [slice(c * C, (c + 1) * C) for c in chunks]
    ends = [cum[(c + 1) * C - 1:(c + 1) * C, :] for c in chunks]
    cl = jnp.concatenate([jnp.broadcast_to(e, (C, W)) for e in ends], axis=0) if nch > 1 \
        else jnp.broadcast_to(ends[0], (C, W))
    w_inc = jnp.exp(cum)
    w_inv = jnp.exp(-cum)
    w_rem = jnp.exp(cl - cum)
    t_a = jnp.exp(cum - logw) * kk
    t_b = kka * w_inv
    t_k = k * w_inv
    t_r = r * w_inc
    t_kt = k * w_rem
    t_bt = kka * w_rem

    a_s = [stack(t_a[sl]) for sl in sls]
    r_s = [stack(t_r[sl]) for sl in sls]
    v_s = [stack(v[sl]) for sl in sls]
    kt_s = [stack(t_kt[sl]) for sl in sls]
    bt_s = [stack(t_bt[sl]) for sl in sls]
    bk = [jnp.concatenate([stack(t_b[sl]), stack(t_k[sl])], axis=0) for sl in sls]
    g_a = [mm_gram(a_s[c], bk[c], _NT) for c in chunks]
    g_r = [mm_gram(r_s[c], bk[c], _NT) for c in chunks]
    n = [jnp.where(strict, -g[:, :HC], 0.0) for g in g_a]
    l_k = [jnp.where(strict, g[:, HC:], 0.0) for g in g_a]
    m_b = [jnp.where(incl, g[:, :HC], 0.0) for g in g_r]
    m_k = [jnp.where(incl, g[:, HC:], 0.0) for g in g_r]

    tinv = [jnp.where(eye_hc, 1.0, 0.0) + x for x in n]
    pw = n
    d = 2
    while d < C:
        pw = [mm_inv(x, x) for x in pw]
        tinv = [tinv[c] + mm_inv(tinv[c], pw[c]) for c in chunks]
        d *= 2

    x1 = [mm_loc(l_k[c], v_s[c]) for c in chunks]
    tu = [mm_loc(tinv[c], jnp.concatenate([x1[c], a_s[c]], axis=1)) for c in chunks]
    mb = [mm_loc(m_b[c], tu[c]) for c in chunks]
    y_loc = [mm_loc(m_k[c], v_s[c]) - mb[c][:, :W] for c in chunks]
    r_t = [r_s[c] - mb[c][:, W:] for c in chunks]
    p_m = [jnp.where(eye_w, jnp.exp(ends[c]), 0.0) - mm_loc(tu[c][:, W:], bt_s[c], _TN) for c in chunks]
    q_m = [mm_loc(v_s[c], kt_s[c], _TN) - mm_loc(tu[c][:, :W], bt_s[c], _TN) for c in chunks]

    st = s_ref[...]
    ys = []
    for c in chunks:
        y_st = mm_state(r_t[c], st, _NT) + y_loc[c]
        yc = y_st[0:C]
        for h in range(1, RW_HEADS):
            yc = yc + y_st[h * C:(h + 1) * C]
        ys.append(yc)
        st = mm_state(st, p_m[c]) + q_m[c]
    s_ref[...] = st

    y = jnp.concatenate(ys, axis=0) if len(ys) > 1 else ys[0]
    state_ref[0] = s_ref[...]
    inv_n = 1.0 / RW_HEAD
    mean = segsum(y) * inv_n
    yc = y - mean
    var = segsum(yc * yc) * inv_n
    yn = yc * lax.rsqrt(var + GN_EPS) * gnw_ref[...] + gnb_ref[...]
    y_ref[0] = ((yn + bonus) * g).astype(BF)


def _rwkv(prw, shift0, state_bd, mu, w0, w2p, a0, a2p, g2p, k_k, k_a, r_k, gn_w, gn_b, ones_bd):
    B, S, _ = prw.shape
    R = min(RW_TILE, S)
    C = min(RW_CHUNK, S)
    vec = lambda b, t: (0, 0)
    wv = pl.BlockSpec((1, RW_WIDTH), vec)
    lora = pl.BlockSpec((LANES, RW_WIDTH), vec)
    return pl.pallas_call(
        functools.partial(_rwkv_kernel, R=R, C=C),
        grid=(B, S // R),
        in_specs=[
            pl.BlockSpec((1, R, RW_IN), lambda b, t: (b, t, 0)),
            pl.BlockSpec((1, 1, RW_IN), lambda b, t: (b, 0, 0)),
            pl.BlockSpec((1, RW_WIDTH, RW_WIDTH), lambda b, t: (b, 0, 0)),
            pl.BlockSpec((1, RW_IN), vec),
            wv, lora, wv, lora, lora, wv, wv, wv, wv, wv,
            pl.BlockSpec((RW_WIDTH, RW_WIDTH), vec),
        ],
        out_specs=[
            pl.BlockSpec((1, R, RW_WIDTH), lambda b, t: (b, t, 0)),
            pl.BlockSpec((1, 1, RW_IN), lambda b, t: (b, 0, 0)),
            pl.BlockSpec((1, RW_WIDTH, RW_WIDTH), lambda b, t: (b, 0, 0)),
        ],
        out_shape=[
            jax.ShapeDtypeStruct((B, S, RW_WIDTH), BF),
            jax.ShapeDtypeStruct((B, 1, RW_IN), F32),
            jax.ShapeDtypeStruct((B, RW_WIDTH, RW_WIDTH), F32),
        ],
        scratch_shapes=[pltpu.VMEM((1, RW_IN), F32), pltpu.VMEM((RW_WIDTH, RW_WIDTH), F32)],
        compiler_params=_cparams("parallel", "arbitrary"),
        name="rwkv7",
    )(prw, shift0, state_bd, mu, w0, w2p, a0, a2p, g2p, k_k, k_a, r_k, gn_w, gn_b, ones_bd)


def _attn_causal_kernel(q_ref, k_ref, wuv_ref, o_ref, m_ref, acc_ref, *, tq):
    qi = pl.program_id(1)
    q = q_ref[0].reshape(MLA_HEADS * tq, 256)
    m_ref[...] = jnp.full(m_ref.shape, -jnp.inf, F32)
    acc_ref[...] = jnp.zeros(acc_ref.shape, F32)

    def step(kblk, masked):
        s = lax.dot_general(kblk, q, _NT, preferred_element_type=F32)
        if masked:
            lead = s.shape[0] - tq
            kc = lax.broadcasted_iota(jnp.int32, (tq, s.shape[1]), 0) // CHUNK
            qc = (lax.broadcasted_iota(jnp.int32, (tq, s.shape[1]), 1) % tq) // CHUNK
            own = jnp.where(kc <= qc, s[lead:], -jnp.inf)
            s = jnp.concatenate([s[:lead], own], axis=0) if lead else own
        m_prev = m_ref[...]
        m_new = jnp.maximum(m_prev, jnp.max(s, axis=0, keepdims=True))
        alpha = jnp.exp2(m_prev - m_new)
        p = jnp.exp2(s - m_new).astype(BF)
        pv = lax.dot_general(kblk[:, :PV_ROWS], p, _TN, preferred_element_type=F32)
        acc_ref[...] = alpha * acc_ref[...] + pv
        m_ref[...] = m_new

    def body(kj, carry):
        step(k_ref[0, pl.ds(pl.multiple_of(kj * (2 * tq), 2 * tq), 2 * tq), :], False)
        return carry

    lax.fori_loop(0, qi // 2, body, 0)

    @pl.when(qi % 2 == 1)
    def _():
        step(k_ref[0, pl.ds(pl.multiple_of((qi - 1) * tq, 2 * tq), 2 * tq), :], True)

    @pl.when(qi % 2 == 0)
    def _():
        step(k_ref[0, pl.ds(pl.multiple_of(qi * tq, tq), tq), :], True)


    o = (acc_ref[0:KV_LORA, :] / acc_ref[KV_LORA:KV_LORA + 1, :]).astype(BF)
    for hp in range(MLA_HEADS // 2):
        yt = [jnp.dot(wuv_ref[h], o[:, h * tq:(h + 1) * tq], preferred_element_type=F32)
              for h in (2 * hp, 2 * hp + 1)]
        o_ref[0, :, hp * LANES:(hp + 1) * LANES] = jnp.concatenate(yt, axis=0).T.astype(BF)


def _attn_causal(qcat, kcat, wuv_t):
    B, H, S, _ = qcat.shape
    tq = min(ATT_TILE, S)
    return pl.pallas_call(
        functools.partial(_attn_causal_kernel, tq=tq),
        grid=(B, S // tq),
        in_specs=[
            pl.BlockSpec((1, H, tq, 256), lambda b, i: (b, 0, i, 0)),
            pl.BlockSpec((1, S, 256), lambda b, i: (b, 0, 0)),
            pl.BlockSpec((H, V_HEAD, KV_LORA), lambda b, i: (0, 0, 0)),
        ],
        out_specs=pl.BlockSpec((1, tq, H * V_HEAD), lambda b, i: (b, i, 0)),
        out_shape=jax.ShapeDtypeStruct((B, S, H * V_HEAD), BF),
        scratch_shapes=[pltpu.VMEM((1, H * tq), F32), pltpu.VMEM((PV_ROWS, H * tq), F32)],
        compiler_params=_cparams("parallel", "arbitrary"),
        name="mla_causal",
    )(qcat, kcat, wuv_t)


def _attn_full_kernel(q_ref, ckv_ref, kr_ref, kn_ref, wuv_ref, o_ref, *, S):
    q = q_ref[0].reshape(MLA_HEADS * S, 256)
    ckv = ckv_ref[0].astype(BF)
    kr = kr_ref[0].astype(BF)
    kn = kn_ref[0]
    sel = (lax.broadcasted_iota(jnp.int32, (LANES, QK_ROPE), 0)
           == lax.broadcasted_iota(jnp.int32, (LANES, QK_ROPE), 1) + ROPE_LANE0)
    q_rope = jnp.dot(q[:, LANES:], jnp.where(sel, 1.0, 0.0).astype(BF),
                     preferred_element_type=F32).astype(BF)
    s1 = (lax.dot_general(q[:, :LANES], ckv, _NT, preferred_element_type=F32)
          + lax.dot_general(q_rope, kr, _NT, preferred_element_type=F32))
    s2 = lax.dot_general(q, kn, _NT, preferred_element_type=F32)
    m = jnp.maximum(jnp.max(s1, axis=-1, keepdims=True), jnp.max(s2, axis=-1, keepdims=True))
    p1 = jnp.exp2(s1 - m)
    p2 = jnp.exp2(s2 - m)
    l = jnp.sum(p1, axis=-1, keepdims=True) + jnp.sum(p2, axis=-1, keepdims=True)
    o = ((jnp.dot(p1.astype(BF), ckv, preferred_element_type=F32)
          + jnp.dot(p2.astype(BF), kn[:, :KV_LORA], preferred_element_type=F32)) / l).astype(BF)
    y = [jnp.dot(o[h * S:(h + 1) * S], wuv_ref[h], preferred_element_type=F32)
         for h in range(MLA_HEADS)]
    o_ref[0] = jnp.concatenate(y, axis=-1).astype(BF)


def _attn_full(qcat, past_ckv, past_kr, knew, wuv_h, l):
    B, H, S, _ = qcat.shape
    P = past_ckv.shape[2]
    return pl.pallas_call(
        functools.partial(_attn_full_kernel, S=S),
        grid=(B,),
        in_specs=[
            pl.BlockSpec((1, H, S, 256), lambda b: (b, 0, 0, 0)),
            pl.BlockSpec((None, 1, P, KV_LORA), lambda b: (l, b, 0, 0)),
            pl.BlockSpec((None, 1, P, QK_ROPE), lambda b: (l, b, 0, 0)),
            pl.BlockSpec((1, S, 256), lambda b: (b, 0, 0)),
            pl.BlockSpec((H, KV_LORA, V_HEAD), lambda b: (0, 0, 0)),
        ],
        out_specs=pl.BlockSpec((1, S, H * V_HEAD), lambda b: (b, 0, 0)),
        out_shape=jax.ShapeDtypeStruct((B, S, H * V_HEAD), BF),
        compiler_params=_cparams("parallel"),
        name="mla_full",
    )(qcat, past_ckv, past_kr, knew, wuv_h)


def _out_kernel(x_ref, yrg_ref, yrw_ref, ymla_ref, wout_ref, o_ref):
    ycat = jnp.concatenate([yrg_ref[...], yrw_ref[...], ymla_ref[...]], axis=-1)
    o_ref[...] = x_ref[...] + jnp.dot(ycat, wout_ref[...], preferred_element_type=F32)


def _out_proj(x2d, yrg, yrw, ymla, wout):
    T = x2d.shape[0]
    tm = min(ROW_TILE, T)
    row = lambda i: (i, 0)
    const = lambda i: (0, 0)
    return pl.pallas_call(
        _out_kernel,
        grid=(T // tm,),
        in_specs=[
            pl.BlockSpec((tm, D_MODEL), row),
            pl.BlockSpec((tm, RG_WIDTH), row),
            pl.BlockSpec((tm, RW_WIDTH), row),
            pl.BlockSpec((tm, MLA_HEADS * V_HEAD), row),
            pl.BlockSpec((D_MODEL, D_MODEL), const),
        ],
        out_specs=pl.BlockSpec((tm, D_MODEL), row),
        out_shape=jax.ShapeDtypeStruct((T, D_MODEL), F32),
        compiler_params=_cparams("parallel"),
        name="out_proj",
    )(x2d, yrg, yrw, ymla, wout)


def _mem_kernel(x_ref, g_ref, wq_ref, mk_ref, mv_ref, wo_ref, o_ref):
    x = x_ref[...]
    hn = _rms(x, g_ref[...]).astype(BF)
    q = jnp.dot(hn, wq_ref[...], preferred_element_type=F32).astype(BF)
    by_head = len(mk_ref.shape) == 4
    outs = []
    for h in range(MEM_HEADS):
        sl = slice(h * MEM_HEAD, (h + 1) * MEM_HEAD)
        mk_h = mk_ref[0, :, h, :].astype(BF) if by_head else mk_ref[0, :, sl]
        mv_h = mv_ref[0, :, h, :].astype(BF) if by_head else mv_ref[0, :, sl]
        s = lax.dot_general(q[:, sl], mk_h, _NT, preferred_element_type=F32) * MEM_SCALE
        e = jnp.exp(s - jnp.max(s, axis=-1, keepdims=True))
        pr = e * (1.0 / jnp.sum(e, axis=-1, keepdims=True))
        outs.append(jnp.dot(pr.astype(BF), mv_h, preferred_element_type=F32).astype(BF))
    o = jnp.concatenate(outs, axis=-1)
    o_ref[...] = x + jnp.dot(o, wo_ref[...], preferred_element_type=F32)


def _mem_attn(x2d, B, S, g, wq, mk, mv, wo, l):
    T = B * S
    tm = min(ROW_TILE, S)
    nst = S // tm
    row = lambda i: (i, 0)
    const = lambda i: (0, 0)
    if mk.ndim == 5:
        mem_spec = pl.BlockSpec((None, 1, N_MEM, MEM_HEADS, MEM_HEAD),
                                lambda i: (l, i // nst, 0, 0, 0))
    else:
        mem_spec = pl.BlockSpec((1, N_MEM, D_MODEL), lambda i: (i // nst, 0, 0))
    return pl.pallas_call(
        _mem_kernel,
        grid=(T // tm,),
        in_specs=[
            pl.BlockSpec((tm, D_MODEL), row),
            pl.BlockSpec((1, D_MODEL), const),
            pl.BlockSpec((D_MODEL, D_MODEL), const),
            mem_spec,
            mem_spec,
            pl.BlockSpec((D_MODEL, D_MODEL), const),
        ],
        out_specs=pl.BlockSpec((tm, D_MODEL), row),
        out_shape=jax.ShapeDtypeStruct((T, D_MODEL), F32),
        compiler_params=_cparams("parallel"),
        name="mem_attn",
    )(x2d, g, wq, mk, mv, wo)


def _ffn_kernel(x_ref, g_ref, w1_ref, w2_ref, gf_ref, o_ref, *, final):
    x = x_ref[...]
    xn = _rms(x, g_ref[...]).astype(BF)
    acc = x
    for c in range(D_FF // FF_TILE):
        h = jnp.dot(xn, w1_ref[:, c * FF_TILE:(c + 1) * FF_TILE], preferred_element_type=F32)
        h = jnp.square(jnp.maximum(h, 0.0)).astype(BF)
        acc = acc + jnp.dot(h, w2_ref[c * FF_TILE:(c + 1) * FF_TILE, :], preferred_element_type=F32)
    if final:
        acc = _rms(acc, gf_ref[...])
    o_ref[...] = acc


def _ffn(x2d, g, w1, w2, gf, final):
    T = x2d.shape[0]
    tm = min(FFN_ROW_TILE, T)
    const = lambda i: (0, 0)
    return pl.pallas_call(
        functools.partial(_ffn_kernel, final=final),
        grid=(T // tm,),
        in_specs=[
            pl.BlockSpec((tm, D_MODEL), lambda i: (i, 0)),
            pl.BlockSpec((1, D_MODEL), const),
            pl.BlockSpec((D_MODEL, D_FF), const, pipeline_mode=pl.Buffered(1)),
            pl.BlockSpec((D_FF, D_MODEL), const, pipeline_mode=pl.Buffered(1)),
            pl.BlockSpec((1, D_MODEL), const),
        ],
        out_specs=pl.BlockSpec((tm, D_MODEL), lambda i: (i, 0)),
        out_shape=jax.ShapeDtypeStruct((T, D_MODEL), F32),
        compiler_params=_cparams("parallel"),
        name="ffn",
    )(x2d, g, w1, w2, gf)


def _memproj_kernel(*refs, l):
    m_ref, g_ref, w_ref = refs[:3]
    prev = refs[3:5] if l else ()
    k_ref, v_ref, kb_ref, vb_ref = refs[3 + len(prev):]
    for src, dst in zip(prev, (k_ref, v_ref)):
        dst[0:l] = src[...]
    mn = _rms(m_ref[...], g_ref[...]).astype(BF)
    kv = jnp.dot(mn, w_ref[...], preferred_element_type=F32)
    k = kv[:, :D_MODEL]
    v = kv[:, D_MODEL:]
    for h in range(MEM_HEADS):
        k_ref[l, :, h, :] = k[:, h * MEM_HEAD:(h + 1) * MEM_HEAD]
        v_ref[l, :, h, :] = v[:, h * MEM_HEAD:(h + 1) * MEM_HEAD]
    kb_ref[...] = k.astype(BF)
    vb_ref[...] = v.astype(BF)


def _mem_project(mem2d, g, wkv, l, prev):
    T = mem2d.shape[0]
    tm = min(MEMPROJ_TILE, T)
    row = lambda i: (i, 0)
    const = lambda i: (0, 0)
    prev = () if prev is None else prev
    stacked = lambda n: pl.BlockSpec((n, tm, MEM_HEADS, MEM_HEAD), lambda i: (0, i, 0, 0))
    return pl.pallas_call(
        functools.partial(_memproj_kernel, l=l),
        grid=(T // tm,),
        in_specs=[
            pl.BlockSpec((tm, D_MODEL), row),
            pl.BlockSpec((1, D_MODEL), const),
            pl.BlockSpec((D_MODEL, 2 * D_MODEL), const),
        ] + [stacked(l) for _ in prev],
        out_specs=[stacked(l + 1), stacked(l + 1),
                   pl.BlockSpec((tm, D_MODEL), row), pl.BlockSpec((tm, D_MODEL), row)],
        out_shape=[jax.ShapeDtypeStruct((l + 1, T, MEM_HEADS, MEM_HEAD), F32)] * 2
                  + [jax.ShapeDtypeStruct((T, D_MODEL), BF)] * 2,
        compiler_params=_cparams("parallel"),
        name="mem_project",
    )(mem2d, g, wkv, *prev)


def _fold_kernel(a_ref, b_ref, o_ref):
    o_ref[0] = _mm3(a_ref[0], b_ref[0])


def _fold_heads(a, b):
    G, m, k = a.shape
    n = b.shape[2]
    return pl.pallas_call(
        _fold_kernel,
        grid=(G,),
        in_specs=[pl.BlockSpec((1, m, k), lambda g: (g, 0, 0)),
                  pl.BlockSpec((1, k, n), lambda g: (g, 0, 0))],
        out_specs=pl.BlockSpec((1, m, n), lambda g: (g, 0, 0)),
        out_shape=jax.ShapeDtypeStruct((G, m, n), F32),
        compiler_params=_cparams("parallel"),
        name="fold_heads",
    )(a, b)


def _block_diag(blocks):
    G, m, n = blocks.shape
    eye = jnp.eye(G, dtype=blocks.dtype)
    return jnp.einsum('gmn,gh->gmhn', blocks, eye).reshape(G * m, G * n)


def _prep_weights(l, W):
    r2 = lambda v: v.reshape(1, -1)
    w_in = W['w_in'][l]
    kr_cols = w_in[:, 1792:1824]
    kr_swap = jnp.concatenate([kr_cols[:, 16:], kr_cols[:, :16]], axis=1)
    w_ext = jnp.concatenate([w_in[:, :1792], jnp.tile(kr_cols, (1, 4)), jnp.tile(kr_swap, (1, 4))],
                            axis=1).astype(BF)
    w_uq = W['mla_w_uq'][l]
    rope = w_uq[:, :, QK_NOPE:]
    rope_swap = jnp.concatenate([rope[:, :, 16:], rope[:, :, :16]], axis=2)
    w_fold = _fold_heads(jnp.transpose(w_uq[:, :, :QK_NOPE], (1, 0, 2)),
                         jnp.transpose(W['mla_w_uk'][l], (1, 2, 0)))
    wq_ext = jnp.concatenate([jnp.transpose(w_fold, (1, 0, 2)).reshape(Q_LORA, -1),
                              rope.reshape(Q_LORA, -1), rope_swap.reshape(Q_LORA, -1)],
                             axis=1).astype(BF)
    w_uv = W['mla_w_uv'][l]
    wuv_t = jnp.transpose(w_uv, (1, 2, 0)).astype(BF)
    wuv_h = jnp.transpose(w_uv, (1, 0, 2)).astype(BF)
    zpad = lambda top, mat, bot: jnp.concatenate(
        [jnp.zeros((top, RW_WIDTH), F32), mat, jnp.zeros((bot, RW_WIDTH), F32)], axis=0)
    return dict(
        norm_mix=r2(W['norm_mix'][l]), w_ext=w_ext, gq=r2(W['mla_q_norm'][l]),
        gkv=r2(W['mla_kv_norm'][l]), wq_ext=wq_ext, wuv_t=wuv_t, wuv_h=wuv_h,
        cw=W['rg_conv_w'][l], cb=r2(W['rg_conv_b'][l]),
        wa_bd=_block_diag(W['rg_wa'][l]).astype(BF), ba=r2(W['rg_ba'][l]),
        wx_bd=_block_diag(W['rg_wx'][l]).astype(BF), bx=r2(W['rg_bx'][l]),
        lam=r2(W['rg_lambda'][l]),
        mu=r2(W['rw_mu'][l]), w0=r2(W['rw_w0'][l]), w2p=zpad(0, W['rw_w2'][l], 96),
        a0=r2(W['rw_a0'][l]), a2p=zpad(32, W['rw_a2'][l], 64), g2p=zpad(64, W['rw_g2'][l], 0),
        k_k=r2(W['rw_k_k'][l]), k_a=r2(W['rw_k_a'][l]), r_k=r2(W['rw_r_k'][l]),
        gn_w=r2(W['rw_gn_w'][l]), gn_b=r2(W['rw_gn_b'][l]),
        w_out=W['w_out'][l].astype(BF),
        norm_mem=r2(W['norm_mem'][l]), wq=W['mem_w_q'][l].reshape(D_MODEL, D_MODEL).astype(BF),
        wo=W['mem_w_o'][l].reshape(D_MODEL, D_MODEL).astype(BF),
        norm_ffn=r2(W['norm_ffn'][l]), w1=W['ffn_w1'][l].astype(BF), w2=W['ffn_w2'][l].astype(BF),
        norm_mem_kv=r2(W['norm_mem_kv'][l]),
        wkv=jnp.concatenate([W['mem_w_k'][l].reshape(D_MODEL, D_MODEL),
                             W['mem_w_v'][l].reshape(D_MODEL, D_MODEL)], axis=1).astype(BF),
    )


def _rope_tables(pos, rows):
    half = QK_ROPE // 2
    inv = ROPE_BASE ** (-jnp.arange(half, dtype=F32) / half)
    ang = pos.astype(F32)[:, None] * inv
    cos, sin = jnp.cos(ang), jnp.sin(ang)
    cosq = jnp.tile(jnp.concatenate([cos, cos], axis=1), (1, MLA_HEADS))
    sinq = jnp.tile(jnp.concatenate([-sin, sin], axis=1), (1, MLA_HEADS))
    reps = max(1, rows // pos.shape[0])
    return jnp.tile(cosq, (reps, 1)), jnp.tile(sinq, (reps, 1))


def _layer(x2d, B, S, P, tabs, rg_buf, rg_h, rw_shift, rw_state, past, mem_k, mem_v, gf, l, depth,
           prev_kv):
    cosq, sinq = tabs
    final = l == depth - 1
    pa, prw, ckv, kr, kcat, qcat = _in_proj(x2d, B, S, P['norm_mix'], P['w_ext'], P['gq'], P['gkv'],
                                            P['wq_ext'], cosq, sinq, l, prev_kv)
    buf8 = jnp.concatenate([jnp.zeros((B, 5, RG_WIDTH), F32), rg_buf], axis=1)
    yrg, buf_new, h_new = _rglru(pa.reshape(B, S, 512), buf8, rg_h.reshape(B, 1, RG_WIDTH),
                                 P['cw'], P['cb'], P['wa_bd'], P['ba'], P['wx_bd'], P['bx'], P['lam'])
    eye_h = jnp.eye(RW_HEADS, dtype=F32)
    state_bd = jnp.einsum('bhij,hg->bhigj', rw_state, eye_h).reshape(B, RW_WIDTH, RW_WIDTH)
    ones_bd = _block_diag(jnp.ones((RW_HEADS, RW_HEAD, RW_HEAD), F32)).astype(BF)
    yrw, shift_new, state_new = _rwkv(prw.reshape(B, S, RW_IN), rw_shift.reshape(B, 1, RW_IN), state_bd,
                                      P['mu'], P['w0'], P['w2p'], P['a0'], P['a2p'], P['g2p'],
                                      P['k_k'], P['k_a'], P['r_k'], P['gn_w'], P['gn_b'], ones_bd)
    wkv_new = jnp.einsum('bhigj,hg->bhij',
                         state_new.reshape(B, RW_HEADS, RW_HEAD, RW_HEADS, RW_HEAD), eye_h)
    kcat3 = kcat.reshape(B, S, 256)
    if past is None:
        ymla = _attn_causal(qcat, kcat3, P['wuv_t'])
    else:
        ymla = _attn_full(qcat, past[0], past[1], kcat3, P['wuv_h'], l)
    x1 = _out_proj(x2d, yrg.reshape(B * S, RG_WIDTH), yrw.reshape(B * S, RW_WIDTH),
                   ymla.reshape(B * S, MLA_HEADS * V_HEAD), P['w_out'])
    x2 = _mem_attn(x1, B, S, P['norm_mem'], P['wq'], mem_k, mem_v, P['wo'], l)
    x3 = _ffn(x2, P['norm_ffn'], P['w1'], P['w2'], gf, final)
    new_state = (buf_new[:, 5:8], h_new.reshape(B, RG_WIDTH), shift_new.reshape(B, RW_IN), wkv_new)
    return x3, new_state, (ckv, kr)


def kernel(x_prompt, x_sample, state_rg_conv, state_rg_h, state_rw_shift, state_rw_wkv, cache_mla_ckv, cache_mla_krope, cache_mem_k, cache_mem_v, mem_prompt, norm_mix, w_in, rg_conv_w, rg_conv_b, rg_wa, rg_ba, rg_wx, rg_bx, rg_lambda, rw_mu, rw_w0, rw_w2, rw_a0, rw_a2, rw_g2, rw_k_k, rw_k_a, rw_r_k, rw_gn_w, rw_gn_b, mla_q_norm, mla_kv_norm, mla_w_uq, mla_w_uk, mla_w_uv, w_out, norm_mem, norm_mem_kv, mem_w_q, mem_w_k, mem_w_v, mem_w_o, norm_ffn, ffn_w1, ffn_w2, norm_final):
    W = dict(norm_mix=norm_mix, w_in=w_in, rg_conv_w=rg_conv_w, rg_conv_b=rg_conv_b, rg_wa=rg_wa,
             rg_ba=rg_ba, rg_wx=rg_wx, rg_bx=rg_bx, rg_lambda=rg_lambda, rw_mu=rw_mu, rw_w0=rw_w0,
             rw_w2=rw_w2, rw_a0=rw_a0, rw_a2=rw_a2, rw_g2=rw_g2, rw_k_k=rw_k_k, rw_k_a=rw_k_a,
             rw_r_k=rw_r_k.reshape(rw_r_k.shape[0], RW_WIDTH), rw_gn_w=rw_gn_w, rw_gn_b=rw_gn_b,
             mla_q_norm=mla_q_norm, mla_kv_norm=mla_kv_norm, mla_w_uq=mla_w_uq, mla_w_uk=mla_w_uk,
             mla_w_uv=mla_w_uv, w_out=w_out, norm_mem=norm_mem, norm_mem_kv=norm_mem_kv,
             mem_w_q=mem_w_q, mem_w_k=mem_w_k, mem_w_v=mem_w_v, mem_w_o=mem_w_o, norm_ffn=norm_ffn,
             ffn_w1=ffn_w1, ffn_w2=ffn_w2)
    depth = norm_mix.shape[0]
    Bp, Sp, _ = x_prompt.shape
    Bs, Ss, _ = x_sample.shape
    past_len = cache_mla_ckv.shape[2]
    tabs_p = _rope_tables(jnp.arange(Sp, dtype=jnp.int32), min(ROW_TILE, Sp))
    tabs_s = _rope_tables(past_len + jnp.arange(Ss, dtype=jnp.int32), min(ROW_TILE, Ss))
    gf = norm_final.reshape(1, D_MODEL)

    xp = x_prompt.reshape(Bp * Sp, D_MODEL)
    xs = x_sample.reshape(Bs * Ss, D_MODEL)
    z = lambda *s: jnp.zeros(s, F32)
    p_states, s_states = [], []
    mem_kv = kv_p = kv_s = None
    for l in range(depth):
        P = _prep_weights(l, W)
        mk, mv, mkb, mvb = _mem_project(mem_prompt.reshape(Bp * N_MEM, D_MODEL), P['norm_mem_kv'],
                                        P['wkv'], l, mem_kv)
        mem_kv = (mk, mv)
        xp, st_p, kv_p = _layer(xp, Bp, Sp, P, tabs_p, z(Bp, 3, RG_WIDTH), z(Bp, RG_WIDTH),
                                z(Bp, RW_IN), z(Bp, RW_HEADS, RW_HEAD, RW_HEAD), None,
                                mkb.reshape(Bp, N_MEM, D_MODEL), mvb.reshape(Bp, N_MEM, D_MODEL),
                                gf, l, depth, kv_p)
        xs, st_s, kv_s = _layer(xs, Bs, Ss, P, tabs_s, state_rg_conv[l], state_rg_h[l],
                                state_rw_shift[l], state_rw_wkv[l],
                                (cache_mla_ckv, cache_mla_krope),
                                cache_mem_k, cache_mem_v, gf, l, depth, kv_s)
        p_states.append(st_p)
        s_states.append(st_s)

    sp = [jnp.stack(t) for t in zip(*p_states)]
    ss = [jnp.stack(t) for t in zip(*s_states)]
    mem_shape = (depth, Bp, N_MEM, MEM_HEADS, MEM_HEAD)
    return (xp.reshape(Bp, Sp, D_MODEL), xs.reshape(Bs, Ss, D_MODEL),
            sp[0], sp[1], sp[2], sp[3],
            kv_p[0].reshape(depth, Bp, Sp, KV_LORA), kv_p[1].reshape(depth, Bp, Sp, QK_ROPE),
            mem_kv[0].reshape(mem_shape), mem_kv[1].reshape(mem_shape),
            ss[0], ss[1], ss[2], ss[3],
            kv_s[0].reshape(depth, Bs, Ss, KV_LORA), kv_s[1].reshape(depth, Bs, Ss, QK_ROPE))
```

```python
import functools
import math

import jax
import jax.numpy as jnp
from jax import lax
from jax.experimental import pallas as pl
from jax.experimental.pallas import tpu as pltpu

BF = jnp.bfloat16
F32 = jnp.float32

D_MODEL = 1024
RG_WIDTH = 256
RG_BLOCKS = 4
CONV_W = 4
LRU_C = 8.0
RW_HEADS = 4
RW_HEAD = 64
RW_WIDTH = 256
RW_IN = 896
GN_EPS = 64e-5
L2_EPS = 1e-12
MLA_HEADS = 8
QK_NOPE = 64
QK_ROPE = 32
V_HEAD = 64
Q_LORA = 256
KV_LORA = 128
MLA_SCALE = (QK_NOPE + QK_ROPE) ** -0.5
SCALE_LOG2E = MLA_SCALE * math.log2(math.e)
ROPE_BASE = 10000.0
CHUNK = 64
N_MEM = 256
MEM_HEADS = 4
MEM_HEAD = 256
MEM_SCALE = MEM_HEAD ** -0.5
D_FF = 4096
EPS = 1e-6

LANES = 128
VMEM_LIMIT = 52 * 1024 * 1024

ROW_TILE = 1024
RG_TILE = 512
RW_TILE = 512
RW_CHUNK = 64
RW_PASSES_GRAM = 1
RW_PASSES_INV = 1
RW_PASSES_LOCAL = 1
RW_PASSES_STATE = 1
ATT_TILE = 512
IN_SPLIT = 4
ROPE_LANE0 = 8
PV_ROWS = 144
MEMPROJ_TILE = 256
FFN_ROW_TILE = 1024
FF_TILE = 2048


def _cparams(*sem):
    return pltpu.CompilerParams(dimension_semantics=sem, vmem_limit_bytes=VMEM_LIMIT)


def _rms(x, g):
    return x * lax.rsqrt(jnp.mean(x * x, axis=-1, keepdims=True) + EPS) * g


def _dot(a, b):
    return jnp.dot(a.astype(BF), b.astype(BF), preferred_element_type=F32)


def _dot_nt(a, b):
    return lax.dot_general(a.astype(BF), b.astype(BF), (((1,), (1,)), ((), ())),
                           preferred_element_type=F32)


def _split(x):
    hi = x.astype(BF)
    lo = (x - hi.astype(F32)).astype(BF)
    return hi, lo


def _mm3(a, b, dims=(((1,), (0,)), ((), ())), passes=3):
    dg = functools.partial(lax.dot_general, dimension_numbers=dims, preferred_element_type=F32)
    if passes == 1:
        return dg(a.astype(BF), b.astype(BF))
    ah, al = _split(a)
    bh, bl = _split(b)
    return dg(ah, bh) + (dg(ah, bl) + dg(al, bh))


_NT = (((1,), (1,)), ((), ()))
_TN = (((0,), (0,)), ((), ()))


def _softplus(x):
    return jnp.maximum(x, 0.0) + jnp.log1p(jnp.exp(-jnp.abs(x)))


def _sigmoid(x):
    return 0.5 * jnp.tanh(0.5 * x) + 0.5


def _gelu_tanh(x):
    c = math.sqrt(2.0 / math.pi)
    return 0.5 * x * (1.0 + jnp.tanh(c * (x + 0.044715 * (x * x * x))))


def _shift_rows(x, d):
    return pltpu.roll(x, d, 0)


def _in_kernel(x_ref, g_ref, w_ref, gq_ref, gkv_ref, wq_ref, cos_ref, sin_ref, *rest, tm, l):
    prev = rest[:2] if l else ()
    pa_ref, prw_ref, ckv_ref, kr_ref, kcat_ref, qcat_ref = rest[len(prev):]
    for src, dst in zip(prev, (ckv_ref, kr_ref)):
        dst[0:l] = src[...]
    lane = lax.broadcasted_iota(jnp.int32, (1, LANES), 1)
    rope_lanes = (lane >= ROPE_LANE0) & (lane < ROPE_LANE0 + QK_ROPE)

    def rows(sl):
        xn = _rms(x_ref[sl, :], g_ref[...]).astype(BF)
        p = jnp.dot(xn, w_ref[...], preferred_element_type=F32)
        pa_ref[sl, :] = p[:, :512]
        prw_ref[sl, :] = p[:, 512:1408]
        cq = p[:, 1408:1664]
        ckv = p[:, 1664:1792]
        ka = p[:, 1792:1920]
        kb = p[:, 1920:2048]
        cos = cos_ref[sl, :]
        sin = sin_ref[sl, :]
        ckvn = _rms(ckv, gkv_ref[...])
        krt = ka * cos[:, :LANES] + kb * sin[:, :LANES]
        ckv_ref[l, sl, :] = ckvn
        kr_ref[l, sl, :] = krt[:, :QK_ROPE]
        kext = jnp.where(lane == 0, 1.0,
                         jnp.where(rope_lanes, pltpu.roll(krt, ROPE_LANE0, 1), 0.0))
        kcat_ref[sl, :] = jnp.concatenate([ckvn, kext], axis=-1).astype(BF)

        cqn = _rms(cq, gq_ref[...]).astype(BF)
        qq = jnp.dot(cqn, wq_ref[...], preferred_element_type=F32)
        qlat = qq[:, :1024]
        rr = qq[:, 1024:1280] * cos + qq[:, 1280:1536] * sin
        for h in range(MLA_HEADS):
            g = h // 4
            shift = (ROPE_LANE0 - QK_ROPE * (h % 4)) % LANES
            rpart = jnp.where(rope_lanes,
                              pltpu.roll(rr[:, g * LANES:(g + 1) * LANES], shift, 1), 0.0)
            qcat_ref[0, h, sl, :LANES] = (qlat[:, h * LANES:(h + 1) * LANES] * SCALE_LOG2E).astype(BF)
            qcat_ref[0, h, sl, LANES:] = (rpart * SCALE_LOG2E).astype(BF)

    step = tm // IN_SPLIT if tm % (16 * IN_SPLIT) == 0 else tm
    for r0 in range(0, tm, step):
        rows(slice(r0, r0 + step))


def _in_proj(x2d, B, S, g, w_ext, gq, gkv, wq_ext, cosq, sinq, l, prev):
    T = B * S
    prev = () if prev is None else prev
    tm = min(ROW_TILE, S)
    nst = S // tm
    ntab = cosq.shape[0] // tm
    row = lambda i: (i, 0)
    const = lambda i: (0, 0)
    outs = pl.pallas_call(
        functools.partial(_in_kernel, tm=tm, l=l),
        grid=(T // tm,),
        in_specs=[
            pl.BlockSpec((tm, D_MODEL), row),
            pl.BlockSpec((1, D_MODEL), const),
            pl.BlockSpec((D_MODEL, 2048), const),
            pl.BlockSpec((1, Q_LORA), const),
            pl.BlockSpec((1, KV_LORA), const),
            pl.BlockSpec((Q_LORA, 1536), const),
            pl.BlockSpec((tm, 256), lambda i: (i % ntab, 0)),
            pl.BlockSpec((tm, 256), lambda i: (i % ntab, 0)),
        ] + [pl.BlockSpec((l, tm, a.shape[2]), lambda i: (0, i, 0)) for a in prev],
        out_specs=[
            pl.BlockSpec((tm, 512), row),
            pl.BlockSpec((tm, RW_IN), row),
            pl.BlockSpec((l + 1, tm, KV_LORA), lambda i: (0, i, 0)),
            pl.BlockSpec((l + 1, tm, QK_ROPE), lambda i: (0, i, 0)),
            pl.BlockSpec((tm, 256), row),
            pl.BlockSpec((1, MLA_HEADS, tm, 256), lambda i: (i // nst, 0, i % nst, 0)),
        ],
        out_shape=[
            jax.ShapeDtypeStruct((T, 512), F32),
            jax.ShapeDtypeStruct((T, RW_IN), F32),
            jax.ShapeDtypeStruct((l + 1, T, KV_LORA), F32),
            jax.ShapeDtypeStruct((l + 1, T, QK_ROPE), F32),
            jax.ShapeDtypeStruct((T, 256), BF),
            jax.ShapeDtypeStruct((B, MLA_HEADS, S, 256), BF),
        ],
        compiler_params=_cparams("parallel"),
        name="in_proj",
    )(x2d, g, w_ext, gq, gkv, wq_ext, cosq, sinq, *prev)
    return outs


def _rglru_kernel(pa_ref, buf0_ref, h0_ref, cw_ref, cb_ref, wa_ref, ba_ref, wx_ref, bx_ref,
                  lam_ref, y_ref, buf_ref, hout_ref, ext_ref, h_ref, *, R):
    t = pl.program_id(1)

    @pl.when(t == 0)
    def _():
        ext_ref[0:8, :] = buf0_ref[0]
        h_ref[...] = h0_ref[0]

    pa = pa_ref[0]
    xr = pa[:, :RG_WIDTH]
    gate = pa[:, RG_WIDTH:]
    ext_ref[8:8 + R, :] = xr
    cw = cw_ref[...]
    conv = (cb_ref[...] + cw[3:4] * xr + cw[2:3] * ext_ref[7:7 + R, :]
            + cw[1:2] * ext_ref[6:6 + R, :] + cw[0:1] * ext_ref[5:5 + R, :])
    hist = ext_ref[R:R + 8, :]
    ext_ref[0:8, :] = hist
    buf_ref[0] = hist

    ra = _sigmoid(_dot(conv, wa_ref[...]) + ba_ref[...])
    ia = _sigmoid(_dot(conv, wx_ref[...]) + bx_ref[...])
    log_a = (-LRU_C) * ra * _softplus(-lam_ref[...])
    a = jnp.exp(log_a)
    om = -jnp.tanh(log_a) * (a * a + 1.0)
    b = jnp.where(om > 0.0, om * lax.rsqrt(om), 0.0) * (ia * conv)

    rows = lax.broadcasted_iota(jnp.int32, (R, RG_WIDTH), 0)
    d = 1
    while d < R:
        m = rows >= d
        b = jnp.where(m, a * _shift_rows(b, d), 0.0) + b
        a = jnp.where(m, a * _shift_rows(a, d), a)
        d *= 2
    h = a * h_ref[...] + b
    hl = h[R - 1:R, :]
    h_ref[...] = hl
    hout_ref[0] = hl
    y_ref[0] = (_gelu_tanh(gate) * h).astype(BF)


def _rglru(pa, buf8, h0, cw, cb, wa_bd, ba, wx_bd, bx, lam):
    B, S, _ = pa.shape
    R = min(RG_TILE, S)
    vec = lambda b, t: (0, 0)
    return pl.pallas_call(
        functools.partial(_rglru_kernel, R=R),
        grid=(B, S // R),
        in_specs=[
            pl.BlockSpec((1, R, 512), lambda b, t: (b, t, 0)),
            pl.BlockSpec((1, 8, RG_WIDTH), lambda b, t: (b, 0, 0)),
            pl.BlockSpec((1, 1, RG_WIDTH), lambda b, t: (b, 0, 0)),
            pl.BlockSpec((CONV_W, RG_WIDTH), vec),
            pl.BlockSpec((1, RG_WIDTH), vec),
            pl.BlockSpec((RG_WIDTH, RG_WIDTH), vec),
            pl.BlockSpec((1, RG_WIDTH), vec),
            pl.BlockSpec((RG_WIDTH, RG_WIDTH), vec),
            pl.BlockSpec((1, RG_WIDTH), vec),
            pl.BlockSpec((1, RG_WIDTH), vec),
        ],
        out_specs=[
            pl.BlockSpec((1, R, RG_WIDTH), lambda b, t: (b, t, 0)),
            pl.BlockSpec((1, 8, RG_WIDTH), lambda b, t: (b, 0, 0)),
            pl.BlockSpec((1, 1, RG_WIDTH), lambda b, t: (b, 0, 0)),
        ],
        out_shape=[
            jax.ShapeDtypeStruct((B, S, RG_WIDTH), BF),
            jax.ShapeDtypeStruct((B, 8, RG_WIDTH), F32),
            jax.ShapeDtypeStruct((B, 1, RG_WIDTH), F32),
        ],
        scratch_shapes=[pltpu.VMEM((R + 8, RG_WIDTH), F32), pltpu.VMEM((1, RG_WIDTH), F32)],
        compiler_params=_cparams("parallel", "arbitrary"),
        name="rglru",
    )(pa, buf8, h0, cw, cb, wa_bd, ba, wx_bd, bx, lam)


def _rwkv_kernel(p_ref, shift0_ref, state0_ref, mu_ref, w0_ref, w2_ref, a0_ref, a2_ref, g2_ref,
                 kk_ref, ka_ref, rk_ref, gnw_ref, gnb_ref, ones_ref,
                 y_ref, shift_ref, state_ref, prev_ref, s_ref, *, R, C):
    t = pl.program_id(1)
    W = RW_WIDTH
    HC = RW_HEADS * C

    @pl.when(t == 0)
    def _():
        prev_ref[...] = shift0_ref[0]
        s_ref[...] = state0_ref[0]

    p = p_ref[0]
    rows = lax.broadcasted_iota(jnp.int32, (R, RW_IN), 0)
    prev = jnp.where(rows == 0, prev_ref[...], _shift_rows(p, 1))
    last = p[R - 1:R, :]
    prev_ref[...] = last
    shift_ref[0] = last
    xs = p + mu_ref[...] * (prev - p)
    r = xs[:, 0:W]
    k = xs[:, W:2 * W]
    v = xs[:, 2 * W:3 * W]
    x4 = xs[:, 3 * W:]
    ones_bd = ones_ref[...]

    def segsum(x):
        hi, lo = _split(x)
        dg = functools.partial(jnp.dot, preferred_element_type=F32)
        return dg(hi, ones_bd) + dg(lo, ones_bd)

    log_w = -_softplus(-(w0_ref[...] + _mm3(jnp.tanh(x4), w2_ref[...]))) - 0.5
    logw = -jnp.exp(log_w)
    a = _sigmoid(a0_ref[...] + _mm3(x4, a2_ref[...]))
    g = _mm3(_sigmoid(x4), g2_ref[...])
    kk = k * kk_ref[...]
    kk = kk * lax.rsqrt(segsum(kk * kk) + L2_EPS)
    k = k * (1.0 + (a - 1.0) * ka_ref[...])
    bonus = segsum(r * k * rk_ref[...]) * v
    kka = kk * a

    lane_head = lax.broadcasted_iota(jnp.int32, (1, W), 1) // RW_HEAD
    hmask = [lane_head == h for h in range(RW_HEADS)]

    def stack(x):
        return jnp.concatenate([jnp.where(hmask[h], x, 0.0) for h in range(RW_HEADS)], axis=0)

    ri = lax.broadcasted_iota(jnp.int32, (HC, HC), 0) % C
    ci = lax.broadcasted_iota(jnp.int32, (HC, HC), 1) % C
    strict = ri > ci
    incl = ri >= ci
    eye_hc = lax.broadcasted_iota(jnp.int32, (HC, HC), 0) == lax.broadcasted_iota(jnp.int32, (HC, HC), 1)
    eye_w = lax.broadcasted_iota(jnp.int32, (W, W), 0) == lax.broadcasted_iota(jnp.int32, (W, W), 1)
    crow = lax.broadcasted_iota(jnp.int32, (R, W), 0) % C

    mm_gram = functools.partial(_mm3, passes=RW_PASSES_GRAM)
    mm_inv = functools.partial(_mm3, passes=RW_PASSES_INV)
    mm_loc = functools.partial(_mm3, passes=RW_PASSES_LOCAL)
    mm_state = functools.partial(_mm3, passes=RW_PASSES_STATE)

    cum = logw
    d = 1
    while d < C:
        cum = cum + jnp.where(crow >= d, _shift_rows(cum, d), 0.0)
        d *= 2
    nch = R // C
    chunks = range(nch)
    sls = [slice(c * C, (c + 1) * C) for c in chunks]
    ends = [cum[(c + 1) * C - 1:(c + 1) * C, :] for c in chunks]
    cl = jnp.concatenate([jnp.broadcast_to(e, (C, W)) for e in ends], axis=0) if nch > 1 \
        else jnp.broadcast_to(ends[0], (C, W))
    w_inc = jnp.exp(cum)
    w_inv = jnp.exp(-cum)
    w_rem = jnp.exp(cl - cum)
    t_a = jnp.exp(cum - logw) * kk
    t_b = kka * w_inv
    t_k = k * w_inv
    t_r = r * w_inc
    t_kt = k * w_rem
    t_bt = kka * w_rem

    a_s = [stack(t_a[sl]) for sl in sls]
    r_s = [stack(t_r[sl]) for sl in sls]
    v_s = [stack(v[sl]) for sl in sls]
    kt_s = [stack(t_kt[sl]) for sl in sls]
    bt_s = [stack(t_bt[sl]) for sl in sls]
    bk = [jnp.concatenate([stack(t_b[sl]), stack(t_k[sl])], axis=0) for sl in sls]
    g_a = [mm_gram(a_s[c], bk[c], _NT) for c in chunks]
    g_r = [mm_gram(r_s[c], bk[c], _NT) for c in chunks]
    n = [jnp.where(strict, -g[:, :HC], 0.0) for g in g_a]
    l_k = [jnp.where(strict, g[:, HC:], 0.0) for g in g_a]
    m_b = [jnp.where(incl, g[:, :HC], 0.0) for g in g_r]
    m_k = [jnp.where(incl, g[:, HC:], 0.0) for g in g_r]

    def level(b):
        return (ri // (2 * b) == ci // (2 * b)) & (ri % (2 * b) >= b) & (ci % (2 * b) < b)

    tinv = [jnp.where(eye_hc, 1.0, jnp.where(level(1), x, 0.0)) for x in n]
    b = 2
    while b < C:
        lower = [jnp.where(level(b), x, 0.0) for x in n]
        tinv = [tinv[c] + mm_inv(mm_inv(tinv[c], lower[c]), tinv[c]) for c in chunks]
        b *= 2

    x1 = [mm_loc(l_k[c], v_s[c]) for c in chunks]
    tu = [mm_loc(tinv[c], jnp.concatenate([x1[c], a_s[c]], axis=1)) for c in chunks]
    mb = [mm_loc(m_b[c], tu[c]) for c in chunks]
    y_loc = [mm_loc(m_k[c], v_s[c]) - mb[c][:, :W] for c in chunks]
    r_t = [r_s[c] - mb[c][:, W:] for c in chunks]
    p_m = [jnp.where(eye_w, jnp.exp(ends[c]), 0.0) - mm_loc(tu[c][:, W:], bt_s[c], _TN) for c in chunks]
    q_m = [mm_loc(v_s[c], kt_s[c], _TN) - mm_loc(tu[c][:, :W], bt_s[c], _TN) for c in chunks]

    st = s_ref[...]
    ys = []
    for c in chunks:
        y_st = mm_state(r_t[c], st, _NT) + y_loc[c]
        yc = y_st[0:C]
        for h in range(1, RW_HEADS):
            yc = yc + y_st[h * C:(h + 1) * C]
        ys.append(yc)
        st = mm_state(st, p_m[c]) + q_m[c]
    s_ref[...] = st

    y = jnp.concatenate(ys, axis=0) if len(ys) > 1 else ys[0]
    state_ref[0] = s_ref[...]
    inv_n = 1.0 / RW_HEAD
    mean = segsum(y) * inv_n
    yc = y - mean
    var = segsum(yc * yc) * inv_n
    yn = yc * lax.rsqrt(var + GN_EPS) * gnw_ref[...] + gnb_ref[...]
    y_ref[0] = ((yn + bonus) * g).astype(BF)


def _rwkv(prw, shift0, state_bd, mu, w0, w2p, a0, a2p, g2p, k_k, k_a, r_k, gn_w, gn_b, ones_bd):
    B, S, _ = prw.shape
    R = min(RW_TILE, S)
    C = min(RW_CHUNK, S)
    vec = lambda b, t: (0, 0)
    wv = pl.BlockSpec((1, RW_WIDTH), vec)
    lora = pl.BlockSpec((LANES, RW_WIDTH), vec)
    return pl.pallas_call(
        functools.partial(_rwkv_kernel, R=R, C=C),
        grid=(B, S // R),
        in_specs=[
            pl.BlockSpec((1, R, RW_IN), lambda b, t: (b, t, 0)),
            pl.BlockSpec((1, 1, RW_IN), lambda b, t: (b, 0, 0)),
            pl.BlockSpec((1, RW_WIDTH, RW_WIDTH), lambda b, t: (b, 0, 0)),
            pl.BlockSpec((1, RW_IN), vec),
            wv, lora, wv, lora, lora, wv, wv, wv, wv, wv,
            pl.BlockSpec((RW_WIDTH, RW_WIDTH), vec),
        ],
        out_specs=[
            pl.BlockSpec((1, R, RW_WIDTH), lambda b, t: (b, t, 0)),
            pl.BlockSpec((1, 1, RW_IN), lambda b, t: (b, 0, 0)),
            pl.BlockSpec((1, RW_WIDTH, RW_WIDTH), lambda b, t: (b, 0, 0)),
        ],
        out_shape=[
            jax.ShapeDtypeStruct((B, S, RW_WIDTH), BF),
            jax.ShapeDtypeStruct((B, 1, RW_IN), F32),
            jax.ShapeDtypeStruct((B, RW_WIDTH, RW_WIDTH), F32),
        ],
        scratch_shapes=[pltpu.VMEM((1, RW_IN), F32), pltpu.VMEM((RW_WIDTH, RW_WIDTH), F32)],
        compiler_params=_cparams("parallel", "arbitrary"),
        name="rwkv7",
    )(prw, shift0, state_bd, mu, w0, w2p, a0, a2p, g2p, k_k, k_a, r_k, gn_w, gn_b, ones_bd)


def _attn_causal_kernel(q_ref, k_ref, wuv_ref, o_ref, m_ref, acc_ref, *, tq):
    qi = pl.program_id(1)
    q = q_ref[0].reshape(MLA_HEADS * tq, 256)
    m_ref[...] = jnp.full(m_ref.shape, -jnp.inf, F32)
    acc_ref[...] = jnp.zeros(acc_ref.shape, F32)

    def step(kblk, masked):
        s = lax.dot_general(kblk, q, _NT, preferred_element_type=F32)
        if masked:
            lead = s.shape[0] - tq
            kc = lax.broadcasted_iota(jnp.int32, (tq, s.shape[1]), 0) // CHUNK
            qc = (lax.broadcasted_iota(jnp.int32, (tq, s.shape[1]), 1) % tq) // CHUNK
            own = jnp.where(kc <= qc, s[lead:], -jnp.inf)
            s = jnp.concatenate([s[:lead], own], axis=0) if lead else own
        m_prev = m_ref[...]
        m_new = jnp.maximum(m_prev, jnp.max(s, axis=0, keepdims=True))
        alpha = jnp.exp2(m_prev - m_new)
        p = jnp.exp2(s - m_new).astype(BF)
        pv = lax.dot_general(kblk[:, :PV_ROWS], p, _TN, preferred_element_type=F32)
        acc_ref[...] = alpha * acc_ref[...] + pv
        m_ref[...] = m_new

    def body(kj, carry):
        step(k_ref[0, pl.ds(pl.multiple_of(kj * (2 * tq), 2 * tq), 2 * tq), :], False)
        return carry

    lax.fori_loop(0, qi // 2, body, 0)

    @pl.when(qi % 2 == 1)
    def _():
        step(k_ref[0, pl.ds(pl.multiple_of((qi - 1) * tq, 2 * tq), 2 * tq), :], True)

    @pl.when(qi % 2 == 0)
    def _():
        step(k_ref[0, pl.ds(pl.multiple_of(qi * tq, tq), tq), :], True)


    o = (acc_ref[0:KV_LORA, :] / acc_ref[KV_LORA:KV_LORA + 1, :]).astype(BF)
    for hp in range(MLA_HEADS // 2):
        yt = [jnp.dot(wuv_ref[h], o[:, h * tq:(h + 1) * tq], preferred_element_type=F32)
              for h in (2 * hp, 2 * hp + 1)]
        o_ref[0, :, hp * LANES:(hp + 1) * LANES] = jnp.concatenate(yt, axis=0).T.astype(BF)


def _attn_causal(qcat, kcat, wuv_t):
    B, H, S, _ = qcat.shape
    tq = min(ATT_TILE, S)
    return pl.pallas_call(
        functools.partial(_attn_causal_kernel, tq=tq),
        grid=(B, S // tq),
        in_specs=[
            pl.BlockSpec((1, H, tq, 256), lambda b, i: (b, 0, i, 0)),
            pl.BlockSpec((1, S, 256), lambda b, i: (b, 0, 0)),
            pl.BlockSpec((H, V_HEAD, KV_LORA), lambda b, i: (0, 0, 0)),
        ],
        out_specs=pl.BlockSpec((1, tq, H * V_HEAD), lambda b, i: (b, i, 0)),
        out_shape=jax.ShapeDtypeStruct((B, S, H * V_HEAD), BF),
        scratch_shapes=[pltpu.VMEM((1, H * tq), F32), pltpu.VMEM((PV_ROWS, H * tq), F32)],
        compiler_params=_cparams("parallel", "arbitrary"),
        name="mla_causal",
    )(qcat, kcat, wuv_t)


def _attn_full_kernel(q_ref, ckv_ref, kr_ref, kn_ref, wuv_ref, o_ref, *, S):
    q = q_ref[0].reshape(MLA_HEADS * S, 256)
    ckv = ckv_ref[0].astype(BF)
    kr = kr_ref[0].astype(BF)
    kn = kn_ref[0]
    sel = (lax.broadcasted_iota(jnp.int32, (LANES, QK_ROPE), 0)
           == lax.broadcasted_iota(jnp.int32, (LANES, QK_ROPE), 1) + ROPE_LANE0)
    q_rope = jnp.dot(q[:, LANES:], jnp.where(sel, 1.0, 0.0).astype(BF),
                     preferred_element_type=F32).astype(BF)
    s1 = (lax.dot_general(q[:, :LANES], ckv, _NT, preferred_element_type=F32)
          + lax.dot_general(q_rope, kr, _NT, preferred_element_type=F32))
    s2 = lax.dot_general(q, kn, _NT, preferred_element_type=F32)
    m = jnp.maximum(jnp.max(s1, axis=-1, keepdims=True), jnp.max(s2, axis=-1, keepdims=True))
    p1 = jnp.exp2(s1 - m)
    p2 = jnp.exp2(s2 - m)
    l = jnp.sum(p1, axis=-1, keepdims=True) + jnp.sum(p2, axis=-1, keepdims=True)
    o = ((jnp.dot(p1.astype(BF), ckv, preferred_element_type=F32)
          + jnp.dot(p2.astype(BF), kn[:, :KV_LORA], preferred_element_type=F32)) / l).astype(BF)
    y = [jnp.dot(o[h * S:(h + 1) * S], wuv_ref[h], preferred_element_type=F32)
         for h in range(MLA_HEADS)]
    o_ref[0] = jnp.concatenate(y, axis=-1).astype(BF)


def _attn_full(qcat, past_ckv, past_kr, knew, wuv_h, l):
    B, H, S, _ = qcat.shape
    P = past_ckv.shape[2]
    return pl.pallas_call(
        functools.partial(_attn_full_kernel, S=S),
        grid=(B,),
        in_specs=[
            pl.BlockSpec((1, H, S, 256), lambda b: (b, 0, 0, 0)),
            pl.BlockSpec((None, 1, P, KV_LORA), lambda b: (l, b, 0, 0)),
            pl.BlockSpec((None, 1, P, QK_ROPE), lambda b: (l, b, 0, 0)),
            pl.BlockSpec((1, S, 256), lambda b: (b, 0, 0)),
            pl.BlockSpec((H, KV_LORA, V_HEAD), lambda b: (0, 0, 0)),
        ],
        out_specs=pl.BlockSpec((1, S, H * V_HEAD), lambda b: (b, 0, 0)),
        out_shape=jax.ShapeDtypeStruct((B, S, H * V_HEAD), BF),
        compiler_params=_cparams("parallel"),
        name="mla_full",
    )(qcat, past_ckv, past_kr, knew, wuv_h)


def _out_kernel(x_ref, yrg_ref, yrw_ref, ymla_ref, wout_ref, o_ref):
    ycat = jnp.concatenate([yrg_ref[...], yrw_ref[...], ymla_ref[...]], axis=-1)
    o_ref[...] = x_ref[...] + jnp.dot(ycat, wout_ref[...], preferred_element_type=F32)


def _out_proj(x2d, yrg, yrw, ymla, wout):
    T = x2d.shape[0]
    tm = min(ROW_TILE, T)
    row = lambda i: (i, 0)
    const = lambda i: (0, 0)
    return pl.pallas_call(
        _out_kernel,
        grid=(T // tm,),
        in_specs=[
            pl.BlockSpec((tm, D_MODEL), row),
            pl.BlockSpec((tm, RG_WIDTH), row),
            pl.BlockSpec((tm, RW_WIDTH), row),
            pl.BlockSpec((tm, MLA_HEADS * V_HEAD), row),
            pl.BlockSpec((D_MODEL, D_MODEL), const),
        ],
        out_specs=pl.BlockSpec((tm, D_MODEL), row),
        out_shape=jax.ShapeDtypeStruct((T, D_MODEL), F32),
        compiler_params=_cparams("parallel"),
        name="out_proj",
    )(x2d, yrg, yrw, ymla, wout)


def _mem_kernel(x_ref, g_ref, wq_ref, mk_ref, mv_ref, wo_ref, o_ref):
    x = x_ref[...]
    hn = _rms(x, g_ref[...]).astype(BF)
    q = jnp.dot(hn, wq_ref[...], preferred_element_type=F32).astype(BF)
    by_head = len(mk_ref.shape) == 4
    outs = []
    for h in range(MEM_HEADS):
        sl = slice(h * MEM_HEAD, (h + 1) * MEM_HEAD)
        mk_h = mk_ref[0, :, h, :].astype(BF) if by_head else mk_ref[0, :, sl]
        mv_h = mv_ref[0, :, h, :].astype(BF) if by_head else mv_ref[0, :, sl]
        s = lax.dot_general(q[:, sl], mk_h, _NT, preferred_element_type=F32) * MEM_SCALE
        e = jnp.exp(s - jnp.max(s, axis=-1, keepdims=True))
        pr = e * (1.0 / jnp.sum(e, axis=-1, keepdims=True))
        outs.append(jnp.dot(pr.astype(BF), mv_h, preferred_element_type=F32).astype(BF))
    o = jnp.concatenate(outs, axis=-1)
    o_ref[...] = x + jnp.dot(o, wo_ref[...], preferred_element_type=F32)


def _mem_attn(x2d, B, S, g, wq, mk, mv, wo, l):
    T = B * S
    tm = min(ROW_TILE, S)
    nst = S // tm
    row = lambda i: (i, 0)
    const = lambda i: (0, 0)
    if mk.ndim == 5:
        mem_spec = pl.BlockSpec((None, 1, N_MEM, MEM_HEADS, MEM_HEAD),
                                lambda i: (l, i // nst, 0, 0, 0))
    else:
        mem_spec = pl.BlockSpec((1, N_MEM, D_MODEL), lambda i: (i // nst, 0, 0))
    return pl.pallas_call(
        _mem_kernel,
        grid=(T // tm,),
        in_specs=[
            pl.BlockSpec((tm, D_MODEL), row),
            pl.BlockSpec((1, D_MODEL), const),
            pl.BlockSpec((D_MODEL, D_MODEL), const),
            mem_spec,
            mem_spec,
            pl.BlockSpec((D_MODEL, D_MODEL), const),
        ],
        out_specs=pl.BlockSpec((tm, D_MODEL), row),
        out_shape=jax.ShapeDtypeStruct((T, D_MODEL), F32),
        compiler_params=_cparams("parallel"),
        name="mem_attn",
    )(x2d, g, wq, mk, mv, wo)


def _ffn_kernel(x_ref, g_ref, w1_ref, w2_ref, gf_ref, o_ref, *, final):
    x = x_ref[...]
    xn = _rms(x, g_ref[...]).astype(BF)
    acc = x
    for c in range(D_FF // FF_TILE):
        h = jnp.dot(xn, w1_ref[:, c * FF_TILE:(c + 1) * FF_TILE], preferred_element_type=F32)
        h = jnp.square(jnp.maximum(h, 0.0)).astype(BF)
        acc = acc + jnp.dot(h, w2_ref[c * FF_TILE:(c + 1) * FF_TILE, :], preferred_element_type=F32)
    if final:
        acc = _rms(acc, gf_ref[...])
    o_ref[...] = acc


def _ffn(x2d, g, w1, w2, gf, final):
    T = x2d.shape[0]
    tm = min(FFN_ROW_TILE, T)
    const = lambda i: (0, 0)
    return pl.pallas_call(
        functools.partial(_ffn_kernel, final=final),
        grid=(T // tm,),
        in_specs=[
            pl.BlockSpec((tm, D_MODEL), lambda i: (i, 0)),
            pl.BlockSpec((1, D_MODEL), const),
            pl.BlockSpec((D_MODEL, D_FF), const, pipeline_mode=pl.Buffered(1)),
            pl.BlockSpec((D_FF, D_MODEL), const, pipeline_mode=pl.Buffered(1)),
            pl.BlockSpec((1, D_MODEL), const),
        ],
        out_specs=pl.BlockSpec((tm, D_MODEL), lambda i: (i, 0)),
        out_shape=jax.ShapeDtypeStruct((T, D_MODEL), F32),
        compiler_params=_cparams("parallel"),
        name="ffn",
    )(x2d, g, w1, w2, gf)


def _memproj_kernel(*refs, l):
    m_ref, g_ref, w_ref = refs[:3]
    prev = refs[3:5] if l else ()
    k_ref, v_ref, kb_ref, vb_ref = refs[3 + len(prev):]
    for src, dst in zip(prev, (k_ref, v_ref)):
        dst[0:l] = src[...]
    mn = _rms(m_ref[...], g_ref[...]).astype(BF)
    kv = jnp.dot(mn, w_ref[...], preferred_element_type=F32)
    k = kv[:, :D_MODEL]
    v = kv[:, D_MODEL:]
    for h in range(MEM_HEADS):
        k_ref[l, :, h, :] = k[:, h * MEM_HEAD:(h + 1) * MEM_HEAD]
        v_ref[l, :, h, :] = v[:, h * MEM_HEAD:(h + 1) * MEM_HEAD]
    kb_ref[...] = k.astype(BF)
    vb_ref[...] = v.astype(BF)


def _mem_project(mem2d, g, wkv, l, prev):
    T = mem2d.shape[0]
    tm = min(MEMPROJ_TILE, T)
    row = lambda i: (i, 0)
    const = lambda i: (0, 0)
    prev = () if prev is None else prev
    stacked = lambda n: pl.BlockSpec((n, tm, MEM_HEADS, MEM_HEAD), lambda i: (0, i, 0, 0))
    return pl.pallas_call(
        functools.partial(_memproj_kernel, l=l),
        grid=(T // tm,),
        in_specs=[
            pl.BlockSpec((tm, D_MODEL), row),
            pl.BlockSpec((1, D_MODEL), const),
            pl.BlockSpec((D_MODEL, 2 * D_MODEL), const),
        ] + [stacked(l) for _ in prev],
        out_specs=[stacked(l + 1), stacked(l + 1),
                   pl.BlockSpec((tm, D_MODEL), row), pl.BlockSpec((tm, D_MODEL), row)],
        out_shape=[jax.ShapeDtypeStruct((l + 1, T, MEM_HEADS, MEM_HEAD), F32)] * 2
                  + [jax.ShapeDtypeStruct((T, D_MODEL), BF)] * 2,
        compiler_params=_cparams("parallel"),
        name="mem_project",
    )(mem2d, g, wkv, *prev)


def _fold_kernel(a_ref, b_ref, o_ref):
    o_ref[0] = _mm3(a_ref[0], b_ref[0])


def _fold_heads(a, b):
    G, m, k = a.shape
    n = b.shape[2]
    return pl.pallas_call(
        _fold_kernel,
        grid=(G,),
        in_specs=[pl.BlockSpec((1, m, k), lambda g: (g, 0, 0)),
                  pl.BlockSpec((1, k, n), lambda g: (g, 0, 0))],
        out_specs=pl.BlockSpec((1, m, n), lambda g: (g, 0, 0)),
        out_shape=jax.ShapeDtypeStruct((G, m, n), F32),
        compiler_params=_cparams("parallel"),
        name="fold_heads",
    )(a, b)


def _block_diag(blocks):
    G, m, n = blocks.shape
    eye = jnp.eye(G, dtype=blocks.dtype)
    return jnp.einsum('gmn,gh->gmhn', blocks, eye).reshape(G * m, G * n)


def _prep_weights(l, W):
    r2 = lambda v: v.reshape(1, -1)
    w_in = W['w_in'][l]
    kr_cols = w_in[:, 1792:1824]
    kr_swap = jnp.concatenate([kr_cols[:, 16:], kr_cols[:, :16]], axis=1)
    w_ext = jnp.concatenate([w_in[:, :1792], jnp.tile(kr_cols, (1, 4)), jnp.tile(kr_swap, (1, 4))],
                            axis=1).astype(BF)
    w_uq = W['mla_w_uq'][l]
    rope = w_uq[:, :, QK_NOPE:]
    rope_swap = jnp.concatenate([rope[:, :, 16:], rope[:, :, :16]], axis=2)
    w_fold = _fold_heads(jnp.transpose(w_uq[:, :, :QK_NOPE], (1, 0, 2)),
                         jnp.transpose(W['mla_w_uk'][l], (1, 2, 0)))
    wq_ext = jnp.concatenate([jnp.transpose(w_fold, (1, 0, 2)).reshape(Q_LORA, -1),
                              rope.reshape(Q_LORA, -1), rope_swap.reshape(Q_LORA, -1)],
                             axis=1).astype(BF)
    w_uv = W['mla_w_uv'][l]
    wuv_t = jnp.transpose(w_uv, (1, 2, 0)).astype(BF)
    wuv_h = jnp.transpose(w_uv, (1, 0, 2)).astype(BF)
    zpad = lambda top, mat, bot: jnp.concatenate(
        [jnp.zeros((top, RW_WIDTH), F32), mat, jnp.zeros((bot, RW_WIDTH), F32)], axis=0)
    return dict(
        norm_mix=r2(W['norm_mix'][l]), w_ext=w_ext, gq=r2(W['mla_q_norm'][l]),
        gkv=r2(W['mla_kv_norm'][l]), wq_ext=wq_ext, wuv_t=wuv_t, wuv_h=wuv_h,
        cw=W['rg_conv_w'][l], cb=r2(W['rg_conv_b'][l]),
        wa_bd=_block_diag(W['rg_wa'][l]).astype(BF), ba=r2(W['rg_ba'][l]),
        wx_bd=_block_diag(W['rg_wx'][l]).astype(BF), bx=r2(W['rg_bx'][l]),
        lam=r2(W['rg_lambda'][l]),
        mu=r2(W['rw_mu'][l]), w0=r2(W['rw_w0'][l]), w2p=zpad(0, W['rw_w2'][l], 96),
        a0=r2(W['rw_a0'][l]), a2p=zpad(32, W['rw_a2'][l], 64), g2p=zpad(64, W['rw_g2'][l], 0),
        k_k=r2(W['rw_k_k'][l]), k_a=r2(W['rw_k_a'][l]), r_k=r2(W['rw_r_k'][l]),
        gn_w=r2(W['rw_gn_w'][l]), gn_b=r2(W['rw_gn_b'][l]),
        w_out=W['w_out'][l].astype(BF),
        norm_mem=r2(W['norm_mem'][l]), wq=W['mem_w_q'][l].reshape(D_MODEL, D_MODEL).astype(BF),
        wo=W['mem_w_o'][l].reshape(D_MODEL, D_MODEL).astype(BF),
        norm_ffn=r2(W['norm_ffn'][l]), w1=W['ffn_w1'][l].astype(BF), w2=W['ffn_w2'][l].astype(BF),
        norm_mem_kv=r2(W['norm_mem_kv'][l]),
        wkv=jnp.concatenate([W['mem_w_k'][l].reshape(D_MODEL, D_MODEL),
                             W['mem_w_v'][l].reshape(D_MODEL, D_MODEL)], axis=1).astype(BF),
    )


def _rope_tables(pos, rows):
    half = QK_ROPE // 2
    inv = ROPE_BASE ** (-jnp.arange(half, dtype=F32) / half)
    ang = pos.astype(F32)[:, None] * inv
    cos, sin = jnp.cos(ang), jnp.sin(ang)
    cosq = jnp.tile(jnp.concatenate([cos, cos], axis=1), (1, MLA_HEADS))
    sinq = jnp.tile(jnp.concatenate([-sin, sin], axis=1), (1, MLA_HEADS))
    reps = max(1, rows // pos.shape[0])
    return jnp.tile(cosq, (reps, 1)), jnp.tile(sinq, (reps, 1))


def _layer(x2d, B, S, P, tabs, rg_buf, rg_h, rw_shift, rw_state, past, mem_k, mem_v, gf, l, depth,
           prev_kv):
    cosq, sinq = tabs
    final = l == depth - 1
    pa, prw, ckv, kr, kcat, qcat = _in_proj(x2d, B, S, P['norm_mix'], P['w_ext'], P['gq'], P['gkv'],
                                            P['wq_ext'], cosq, sinq, l, prev_kv)
    buf8 = jnp.concatenate([jnp.zeros((B, 5, RG_WIDTH), F32), rg_buf], axis=1)
    yrg, buf_new, h_new = _rglru(pa.reshape(B, S, 512), buf8, rg_h.reshape(B, 1, RG_WIDTH),
                                 P['cw'], P['cb'], P['wa_bd'], P['ba'], P['wx_bd'], P['bx'], P['lam'])
    eye_h = jnp.eye(RW_HEADS, dtype=F32)
    state_bd = jnp.einsum('bhij,hg->bhigj', rw_state, eye_h).reshape(B, RW_WIDTH, RW_WIDTH)
    ones_bd = _block_diag(jnp.ones((RW_HEADS, RW_HEAD, RW_HEAD), F32)).astype(BF)
    yrw, shift_new, state_new = _rwkv(prw.reshape(B, S, RW_IN), rw_shift.reshape(B, 1, RW_IN), state_bd,
                                      P['mu'], P['w0'], P['w2p'], P['a0'], P['a2p'], P['g2p'],
                                      P['k_k'], P['k_a'], P['r_k'], P['gn_w'], P['gn_b'], ones_bd)
    wkv_new = jnp.einsum('bhigj,hg->bhij',
                         state_new.reshape(B, RW_HEADS, RW_HEAD, RW_HEADS, RW_HEAD), eye_h)
    kcat3 = kcat.reshape(B, S, 256)
    if past is None:
        ymla = _attn_causal(qcat, kcat3, P['wuv_t'])
    else:
        ymla = _attn_full(qcat, past[0], past[1], kcat3, P['wuv_h'], l)
    x1 = _out_proj(x2d, yrg.reshape(B * S, RG_WIDTH), yrw.reshape(B * S, RW_WIDTH),
                   ymla.reshape(B * S, MLA_HEADS * V_HEAD), P['w_out'])
    x2 = _mem_attn(x1, B, S, P['norm_mem'], P['wq'], mem_k, mem_v, P['wo'], l)
    x3 = _ffn(x2, P['norm_ffn'], P['w1'], P['w2'], gf, final)
    new_state = (buf_new[:, 5:8], h_new.reshape(B, RG_WIDTH), shift_new.reshape(B, RW_IN), wkv_new)
    return x3, new_state, (ckv, kr)


def kernel(x_prompt, x_sample, state_rg_conv, state_rg_h, state_rw_shift, state_rw_wkv, cache_mla_ckv, cache_mla_krope, cache_mem_k, cache_mem_v, mem_prompt, norm_mix, w_in, rg_conv_w, rg_conv_b, rg_wa, rg_ba, rg_wx, rg_bx, rg_lambda, rw_mu, rw_w0, rw_w2, rw_a0, rw_a2, rw_g2, rw_k_k, rw_k_a, rw_r_k, rw_gn_w, rw_gn_b, mla_q_norm, mla_kv_norm, mla_w_uq, mla_w_uk, mla_w_uv, w_out, norm_mem, norm_mem_kv, mem_w_q, mem_w_k, mem_w_v, mem_w_o, norm_ffn, ffn_w1, ffn_w2, norm_final):
    W = dict(norm_mix=norm_mix, w_in=w_in, rg_conv_w=rg_conv_w, rg_conv_b=rg_conv_b, rg_wa=rg_wa,
             rg_ba=rg_ba, rg_wx=rg_wx, rg_bx=rg_bx, rg_lambda=rg_lambda, rw_mu=rw_mu, rw_w0=rw_w0,
             rw_w2=rw_w2, rw_a0=rw_a0, rw_a2=rw_a2, rw_g2=rw_g2, rw_k_k=rw_k_k, rw_k_a=rw_k_a,
             rw_r_k=rw_r_k.reshape(rw_r_k.shape[0], RW_WIDTH), rw_gn_w=rw_gn_w, rw_gn_b=rw_gn_b,
             mla_q_norm=mla_q_norm, mla_kv_norm=mla_kv_norm, mla_w_uq=mla_w_uq, mla_w_uk=mla_w_uk,
             mla_w_uv=mla_w_uv, w_out=w_out, norm_mem=norm_mem, norm_mem_kv=norm_mem_kv,
             mem_w_q=mem_w_q, mem_w_k=mem_w_k, mem_w_v=mem_w_v, mem_w_o=mem_w_o, norm_ffn=norm_ffn,
             ffn_w1=ffn_w1, ffn_w2=ffn_w2)
    depth = norm_mix.shape[0]
    Bp, Sp, _ = x_prompt.shape
    Bs, Ss, _ = x_sample.shape
    past_len = cache_mla_ckv.shape[2]
    tabs_p = _rope_tables(jnp.arange(Sp, dtype=jnp.int32), min(ROW_TILE, Sp))
    tabs_s = _rope_tables(past_len + jnp.arange(Ss, dtype=jnp.int32), min(ROW_TILE, Ss))
    gf = norm_final.reshape(1, D_MODEL)

    xp = x_prompt.reshape(Bp * Sp, D_MODEL)
    xs = x_sample.reshape(Bs * Ss, D_MODEL)
    z = lambda *s: jnp.zeros(s, F32)
    p_states, s_states = [], []
    mem_kv = kv_p = kv_s = None
    for l in range(depth):
        P = _prep_weights(l, W)
        mk, mv, mkb, mvb = _mem_project(mem_prompt.reshape(Bp * N_MEM, D_MODEL), P['norm_mem_kv'],
                                        P['wkv'], l, mem_kv)
        mem_kv = (mk, mv)
        xp, st_p, kv_p = _layer(xp, Bp, Sp, P, tabs_p, z(Bp, 3, RG_WIDTH), z(Bp, RG_WIDTH),
                                z(Bp, RW_IN), z(Bp, RW_HEADS, RW_HEAD, RW_HEAD), None,
                                mkb.reshape(Bp, N_MEM, D_MODEL), mvb.reshape(Bp, N_MEM, D_MODEL),
                                gf, l, depth, kv_p)
        xs, st_s, kv_s = _layer(xs, Bs, Ss, P, tabs_s, state_rg_conv[l], state_rg_h[l],
                                state_rw_shift[l], state_rw_wkv[l],
                                (cache_mla_ckv, cache_mla_krope),
                                cache_mem_k, cache_mem_v, gf, l, depth, kv_s)
        p_states.append(st_p)
        s_states.append(st_s)

    sp = [jnp.stack(t) for t in zip(*p_states)]
    ss = [jnp.stack(t) for t in zip(*s_states)]
    mem_shape = (depth, Bp, N_MEM, MEM_HEADS, MEM_HEAD)
    return (xp.reshape(Bp, Sp, D_MODEL), xs.reshape(Bs, Ss, D_MODEL),
            sp[0], sp[1], sp[2], sp[3],
            kv_p[0].reshape(depth, Bp, Sp, KV_LORA), kv_p[1].reshape(depth, Bp, Sp, QK_ROPE),
            mem_kv[0].reshape(mem_shape), mem_kv[1].reshape(mem_shape),
            ss[0], ss[1], ss[2], ss[3],
            kv_s[0].reshape(depth, Bs, Ss, KV_LORA), kv_s[1].reshape(depth, Bs, Ss, QK_ROPE))
```

```python
import functools
import math

import jax
import jax.numpy as jnp
from jax import lax
from jax.experimental import pallas as pl
from jax.experimental.pallas import tpu as pltpu

BF = jnp.bfloat16
F32 = jnp.float32

D_MODEL = 1024
RG_WIDTH = 256
RG_BLOCKS = 4
CONV_W = 4
LRU_C = 8.0
RW_HEADS = 4
RW_HEAD = 64
RW_WIDTH = 256
RW_IN = 896
GN_EPS = 64e-5
L2_EPS = 1e-12
MLA_HEADS = 8
QK_NOPE = 64
QK_ROPE = 32
V_HEAD = 64
Q_LORA = 256
KV_LORA = 128
MLA_SCALE = (QK_NOPE + QK_ROPE) ** -0.5
SCALE_LOG2E = MLA_SCALE * math.log2(math.e)
ROPE_BASE = 10000.0
CHUNK = 64
N_MEM = 256
MEM_HEADS = 4
MEM_HEAD = 256
MEM_SCALE = MEM_HEAD ** -0.5
D_FF = 4096
EPS = 1e-6

LANES = 128
VMEM_LIMIT = 52 * 1024 * 1024

ROW_TILE = 1024
RG_TILE = 512
RW_TILE = 1024
RW_CHUNK = 64
RW_PASSES_GRAM = 1
RW_PASSES_INV = 1
RW_PASSES_LOCAL = 1
RW_PASSES_STATE = 1
ATT_TILE = 512
IN_SPLIT = 4
ROPE_LANE0 = 8
PV_ROWS = 144
MEMPROJ_TILE = 256
FFN_ROW_TILE = 1024
FF_TILE = 2048


def _cparams(*sem):
    return pltpu.CompilerParams(dimension_semantics=sem, vmem_limit_bytes=VMEM_LIMIT)


def _rms(x, g):
    return x * lax.rsqrt(jnp.mean(x * x, axis=-1, keepdims=True) + EPS) * g


def _dot(a, b):
    return jnp.dot(a.astype(BF), b.astype(BF), preferred_element_type=F32)


def _dot_nt(a, b):
    return lax.dot_general(a.astype(BF), b.astype(BF), (((1,), (1,)), ((), ())),
                           preferred_element_type=F32)


def _split(x):
    hi = x.astype(BF)
    lo = (x - hi.astype(F32)).astype(BF)
    return hi, lo


def _mm3(a, b, dims=(((1,), (0,)), ((), ())), passes=3):
    dg = functools.partial(lax.dot_general, dimension_numbers=dims, preferred_element_type=F32)
    if passes == 1:
        return dg(a.astype(BF), b.astype(BF))
    ah, al = _split(a)
    bh, bl = _split(b)
    return dg(ah, bh) + (dg(ah, bl) + dg(al, bh))


_NT = (((1,), (1,)), ((), ()))
_TN = (((0,), (0,)), ((), ()))


def _softplus(x):
    return jnp.maximum(x, 0.0) + jnp.log1p(jnp.exp(-jnp.abs(x)))


def _sigmoid(x):
    return 0.5 * jnp.tanh(0.5 * x) + 0.5


def _gelu_tanh(x):
    c = math.sqrt(2.0 / math.pi)
    return 0.5 * x * (1.0 + jnp.tanh(c * (x + 0.044715 * (x * x * x))))


def _shift_rows(x, d):
    return pltpu.roll(x, d, 0)


def _in_kernel(x_ref, g_ref, w_ref, gq_ref, gkv_ref, wq_ref, cos_ref, sin_ref, *rest, tm, l):
    prev = rest[:2] if l else ()
    pa_ref, prw_ref, ckv_ref, kr_ref, kcat_ref, qcat_ref = rest[len(prev):]
    for src, dst in zip(prev, (ckv_ref, kr_ref)):
        dst[0:l] = src[...]
    lane = lax.broadcasted_iota(jnp.int32, (1, LANES), 1)
    rope_lanes = (lane >= ROPE_LANE0) & (lane < ROPE_LANE0 + QK_ROPE)

    def rows(sl):
        xn = _rms(x_ref[sl, :], g_ref[...]).astype(BF)
        p = jnp.dot(xn, w_ref[...], preferred_element_type=F32)
        pa_ref[sl, :] = p[:, :512]
        prw_ref[sl, :] = p[:, 512:1408]
        cq = p[:, 1408:1664]
        ckv = p[:, 1664:1792]
        ka = p[:, 1792:1920]
        kb = p[:, 1920:2048]
        cos = cos_ref[sl, :]
        sin = sin_ref[sl, :]
        ckvn = _rms(ckv, gkv_ref[...])
        krt = ka * cos[:, :LANES] + kb * sin[:, :LANES]
        ckv_ref[l, sl, :] = ckvn
        kr_ref[l, sl, :] = krt[:, :QK_ROPE]
        kext = jnp.where(lane == 0, 1.0,
                         jnp.where(rope_lanes, pltpu.roll(krt, ROPE_LANE0, 1), 0.0))
        kcat_ref[sl, :] = jnp.concatenate([ckvn, kext], axis=-1).astype(BF)

        cqn = _rms(cq, gq_ref[...]).astype(BF)
        qq = jnp.dot(cqn, wq_ref[...], preferred_element_type=F32)
        qlat = qq[:, :1024]
        rr = qq[:, 1024:1280] * cos + qq[:, 1280:1536] * sin
        for h in range(MLA_HEADS):
            g = h // 4
            shift = (ROPE_LANE0 - QK_ROPE * (h % 4)) % LANES
            rpart = jnp.where(rope_lanes,
                              pltpu.roll(rr[:, g * LANES:(g + 1) * LANES], shift, 1), 0.0)
            qcat_ref[0, h, sl, :LANES] = (qlat[:, h * LANES:(h + 1) * LANES] * SCALE_LOG2E).astype(BF)
            qcat_ref[0, h, sl, LANES:] = (rpart * SCALE_LOG2E).astype(BF)

    step = tm // IN_SPLIT if tm % (16 * IN_SPLIT) == 0 else tm
    for r0 in range(0, tm, step):
        rows(slice(r0, r0 + step))


def _in_proj(x2d, B, S, g, w_ext, gq, gkv, wq_ext, cosq, sinq, l, prev):
    T = B * S
    prev = () if prev is None else prev
    tm = min(ROW_TILE, S)
    nst = S // tm
    ntab = cosq.shape[0] // tm
    row = lambda i: (i, 0)
    const = lambda i: (0, 0)
    outs = pl.pallas_call(
        functools.partial(_in_kernel, tm=tm, l=l),
        grid=(T // tm,),
        in_specs=[
            pl.BlockSpec((tm, D_MODEL), row),
            pl.BlockSpec((1, D_MODEL), const),
            pl.BlockSpec((D_MODEL, 2048), const),
            pl.BlockSpec((1, Q_LORA), const),
            pl.BlockSpec((1, KV_LORA), const),
            pl.BlockSpec((Q_LORA, 1536), const),
            pl.BlockSpec((tm, 256), lambda i: (i % ntab, 0)),
            pl.BlockSpec((tm, 256), lambda i: (i % ntab, 0)),
        ] + [pl.BlockSpec((l, tm, a.shape[2]), lambda i: (0, i, 0)) for a in prev],
        out_specs=[
            pl.BlockSpec((tm, 512), row),
            pl.BlockSpec((tm, RW_IN), row),
            pl.BlockSpec((l + 1, tm, KV_LORA), lambda i: (0, i, 0)),
            pl.BlockSpec((l + 1, tm, QK_ROPE), lambda i: (0, i, 0)),
            pl.BlockSpec((tm, 256), row),
            pl.BlockSpec((1, MLA_HEADS, tm, 256), lambda i: (i // nst, 0, i % nst, 0)),
        ],
        out_shape=[
            jax.ShapeDtypeStruct((T, 512), F32),
            jax.ShapeDtypeStruct((T, RW_IN), F32),
            jax.ShapeDtypeStruct((l + 1, T, KV_LORA), F32),
            jax.ShapeDtypeStruct((l + 1, T, QK_ROPE), F32),
            jax.ShapeDtypeStruct((T, 256), BF),
            jax.ShapeDtypeStruct((B, MLA_HEADS, S, 256), BF),
        ],
        compiler_params=_cparams("parallel"),
        name="in_proj",
    )(x2d, g, w_ext, gq, gkv, wq_ext, cosq, sinq, *prev)
    return outs


def _rglru_kernel(pa_ref, buf0_ref, h0_ref, cw_ref, cb_ref, wa_ref, ba_ref, wx_ref, bx_ref,
                  lam_ref, y_ref, buf_ref, hout_ref, ext_ref, h_ref, *, R):
    t = pl.program_id(1)

    @pl.when(t == 0)
    def _():
        ext_ref[0:8, :] = buf0_ref[0]
        h_ref[...] = h0_ref[0]

    pa = pa_ref[0]
    xr = pa[:, :RG_WIDTH]
    gate = pa[:, RG_WIDTH:]
    ext_ref[8:8 + R, :] = xr
    cw = cw_ref[...]
    conv = (cb_ref[...] + cw[3:4] * xr + cw[2:3] * ext_ref[7:7 + R, :]
            + cw[1:2] * ext_ref[6:6 + R, :] + cw[0:1] * ext_ref[5:5 + R, :])
    hist = ext_ref[R:R + 8, :]
    ext_ref[0:8, :] = hist
    buf_ref[0] = hist

    ra = _sigmoid(_dot(conv, wa_ref[...]) + ba_ref[...])
    ia = _sigmoid(_dot(conv, wx_ref[...]) + bx_ref[...])
    log_a = (-LRU_C) * ra * _softplus(-lam_ref[...])
    a = jnp.exp(log_a)
    om = -jnp.tanh(log_a) * (a * a + 1.0)
    b = jnp.where(om > 0.0, om * lax.rsqrt(om), 0.0) * (ia * conv)

    rows = lax.broadcasted_iota(jnp.int32, (R, RG_WIDTH), 0)
    d = 1
    while d < R:
        m = rows >= d
        b = jnp.where(m, a * _shift_rows(b, d), 0.0) + b
        a = jnp.where(m, a * _shift_rows(a, d), a)
        d *= 2
    h = a * h_ref[...] + b
    hl = h[R - 1:R, :]
    h_ref[...] = hl
    hout_ref[0] = hl
    y_ref[0] = (_gelu_tanh(gate) * h).astype(BF)


def _rglru(pa, buf8, h0, cw, cb, wa_bd, ba, wx_bd, bx, lam):
    B, S, _ = pa.shape
    R = min(RG_TILE, S)
    vec = lambda b, t: (0, 0)
    return pl.pallas_call(
        functools.partial(_rglru_kernel, R=R),
        grid=(B, S // R),
        in_specs=[
            pl.BlockSpec((1, R, 512), lambda b, t: (b, t, 0)),
            pl.BlockSpec((1, 8, RG_WIDTH), lambda b, t: (b, 0, 0)),
            pl.BlockSpec((1, 1, RG_WIDTH), lambda b, t: (b, 0, 0)),
            pl.BlockSpec((CONV_W, RG_WIDTH), vec),
            pl.BlockSpec((1, RG_WIDTH), vec),
            pl.BlockSpec((RG_WIDTH, RG_WIDTH), vec),
            pl.BlockSpec((1, RG_WIDTH), vec),
            pl.BlockSpec((RG_WIDTH, RG_WIDTH), vec),
            pl.BlockSpec((1, RG_WIDTH), vec),
            pl.BlockSpec((1, RG_WIDTH), vec),
        ],
        out_specs=[
            pl.BlockSpec((1, R, RG_WIDTH), lambda b, t: (b, t, 0)),
            pl.BlockSpec((1, 8, RG_WIDTH), lambda b, t: (b, 0, 0)),
            pl.BlockSpec((1, 1, RG_WIDTH), lambda b, t: (b, 0, 0)),
        ],
        out_shape=[
            jax.ShapeDtypeStruct((B, S, RG_WIDTH), BF),
            jax.ShapeDtypeStruct((B, 8, RG_WIDTH), F32),
            jax.ShapeDtypeStruct((B, 1, RG_WIDTH), F32),
        ],
        scratch_shapes=[pltpu.VMEM((R + 8, RG_WIDTH), F32), pltpu.VMEM((1, RG_WIDTH), F32)],
        compiler_params=_cparams("parallel", "arbitrary"),
        name="rglru",
    )(pa, buf8, h0, cw, cb, wa_bd, ba, wx_bd, bx, lam)


def _rwkv_kernel(p_ref, shift0_ref, state0_ref, mu_ref, w0_ref, w2_ref, a0_ref, a2_ref, g2_ref,
                 kk_ref, ka_ref, rk_ref, gnw_ref, gnb_ref, ones_ref,
                 y_ref, shift_ref, state_ref, prev_ref, s_ref, *, R, C):
    t = pl.program_id(1)
    W = RW_WIDTH
    HC = RW_HEADS * C

    @pl.when(t == 0)
    def _():
        prev_ref[...] = shift0_ref[0]
        s_ref[...] = state0_ref[0]

    p = p_ref[0]
    rows = lax.broadcasted_iota(jnp.int32, (R, RW_IN), 0)
    prev = jnp.where(rows == 0, prev_ref[...], _shift_rows(p, 1))
    last = p[R - 1:R, :]
    prev_ref[...] = last
    shift_ref[0] = last
    xs = p + mu_ref[...] * (prev - p)
    r = xs[:, 0:W]
    k = xs[:, W:2 * W]
    v = xs[:, 2 * W:3 * W]
    x4 = xs[:, 3 * W:]
    ones_bd = ones_ref[...]

    def segsum(x):
        hi, lo = _split(x)
        dg = functools.partial(jnp.dot, preferred_element_type=F32)
        return dg(hi, ones_bd) + dg(lo, ones_bd)

    log_w = -_softplus(-(w0_ref[...] + _mm3(jnp.tanh(x4), w2_ref[...]))) - 0.5
    logw = -jnp.exp(log_w)
    a = _sigmoid(a0_ref[...] + _mm3(x4, a2_ref[...]))
    g = _mm3(_sigmoid(x4), g2_ref[...])
    kk = k * kk_ref[...]
    kk = kk * lax.rsqrt(segsum(kk * kk) + L2_EPS)
    k = k * (1.0 + (a - 1.0) * ka_ref[...])
    bonus = segsum(r * k * rk_ref[...]) * v
    kka = kk * a

    lane_head = lax.broadcasted_iota(jnp.int32, (1, W), 1) // RW_HEAD
    hmask = [lane_head == h for h in range(RW_HEADS)]

    def stack(x):
        return jnp.concatenate([jnp.where(hmask[h], x, 0.0) for h in range(RW_HEADS)], axis=0)

    ri = lax.broadcasted_iota(jnp.int32, (HC, HC), 0) % C
    ci = lax.broadcasted_iota(jnp.int32, (HC, HC), 1) % C
    strict = ri > ci
    incl = ri >= ci
    eye_hc = lax.broadcasted_iota(jnp.int32, (HC, HC), 0) == lax.broadcasted_iota(jnp.int32, (HC, HC), 1)
    eye_w = lax.broadcasted_iota(jnp.int32, (W, W), 0) == lax.broadcasted_iota(jnp.int32, (W, W), 1)
    crow = lax.broadcasted_iota(jnp.int32, (R, W), 0) % C

    mm_gram = functools.partial(_mm3, passes=RW_PASSES_GRAM)
    mm_inv = functools.partial(_mm3, passes=RW_PASSES_INV)
    mm_loc = functools.partial(_mm3, passes=RW_PASSES_LOCAL)
    mm_state = functools.partial(_mm3, passes=RW_PASSES_STATE)

    cum = logw
    d = 1
    while d < C:
        cum = cum + jnp.where(crow >= d, _shift_rows(cum, d), 0.0)
        d *= 2
    nch = R // C
    chunks = range(nch)
    sls = [slice(c * C, (c + 1) * C) for c in chunks]
    ends = [cum[(c + 1) * C - 1:(c + 1) * C, :] for c in chunks]
    cl = jnp.concatenate([jnp.broadcast_to(e, (C, W)) for e in ends], axis=0) if nch > 1 \
        else jnp.broadcast_to(ends[0], (C, W))
    w_inc = jnp.exp(cum)
    w_inv = jnp.exp(-cum)
    w_rem = jnp.exp(cl - cum)
    t_a = jnp.exp(cum - logw) * kk
    t_b = kka * w_inv
    t_k = k * w_inv
    t_r = r * w_inc
    t_kt = k * w_rem
    t_bt = kka * w_rem

    a_s = [stack(t_a[sl]) for sl in sls]
    r_s = [stack(t_r[sl]) for sl in sls]
    v_s = [stack(v[sl]) for sl in sls]
    kt_s = [stack(t_kt[sl]) for sl in sls]
    bt_s = [stack(t_bt[sl]) for sl in sls]
    bk = [jnp.concatenate([stack(t_b[sl]), stack(t_k[sl])], axis=0) for sl in sls]
    g_a = [mm_gram(a_s[c], bk[c], _NT) for c in chunks]
    g_r = [mm_gram(r_s[c], bk[c], _NT) for c in chunks]
    n = [jnp.where(strict, -g[:, :HC], 0.0) for g in g_a]
    l_k = [jnp.where(strict, g[:, HC:], 0.0) for g in g_a]
    m_b = [jnp.where(incl, g[:, :HC], 0.0) for g in g_r]
    m_k = [jnp.where(incl, g[:, HC:], 0.0) for g in g_r]

    def level(b):
        return (ri // (2 * b) == ci // (2 * b)) & (ri % (2 * b) >= b) & (ci % (2 * b) < b)

    tinv = [jnp.where(eye_hc, 1.0, jnp.where(level(1), x, 0.0)) for x in n]
    b = 2
    while b < C:
        lower = [jnp.where(level(b), x, 0.0) for x in n]
        tinv = [tinv[c] + mm_inv(mm_inv(tinv[c], lower[c]), tinv[c]) for c in chunks]
        b *= 2

    x1 = [mm_loc(l_k[c], v_s[c]) for c in chunks]
    tu = [mm_loc(tinv[c], jnp.concatenate([x1[c], a_s[c]], axis=1)) for c in chunks]
    mb = [mm_loc(m_b[c], tu[c]) for c in chunks]
    y_loc = [mm_loc(m_k[c], v_s[c]) - mb[c][:, :W] for c in chunks]
    r_t = [r_s[c] - mb[c][:, W:] for c in chunks]
    p_m = [jnp.where(eye_w, jnp.exp(ends[c]), 0.0) - mm_loc(tu[c][:, W:], bt_s[c], _TN) for c in chunks]
    q_m = [mm_loc(v_s[c], kt_s[c], _TN) - mm_loc(tu[c][:, :W], bt_s[c], _TN) for c in chunks]

    st = s_ref[...]
    ys = []
    for c in chunks:
        y_st = mm_state(r_t[c], st, _NT) + y_loc[c]
        yc = y_st[0:C]
        for h in range(1, RW_HEADS):
            yc = yc + y_st[h * C:(h + 1) * C]
        ys.append(yc)
        st = mm_state(st, p_m[c]) + q_m[c]
    s_ref[...] = st

    y = jnp.concatenate(ys, axis=0) if len(ys) > 1 else ys[0]
    state_ref[0] = s_ref[...]
    inv_n = 1.0 / RW_HEAD
    mean = segsum(y) * inv_n
    yc = y - mean
    var = segsum(yc * yc) * inv_n
    yn = yc * lax.rsqrt(var + GN_EPS) * gnw_ref[...] + gnb_ref[...]
    y_ref[0] = ((yn + bonus) * g).astype(BF)


def _rwkv(prw, shift0, state_bd, mu, w0, w2p, a0, a2p, g2p, k_k, k_a, r_k, gn_w, gn_b, ones_bd):
    B, S, _ = prw.shape
    R = min(RW_TILE, S)
    C = min(RW_CHUNK, S)
    vec = lambda b, t: (0, 0)
    wv = pl.BlockSpec((1, RW_WIDTH), vec)
    lora = pl.BlockSpec((LANES, RW_WIDTH), vec)
    return pl.pallas_call(
        functools.partial(_rwkv_kernel, R=R, C=C),
        grid=(B, S // R),
        in_specs=[
            pl.BlockSpec((1, R, RW_IN), lambda b, t: (b, t, 0)),
            pl.BlockSpec((1, 1, RW_IN), lambda b, t: (b, 0, 0)),
            pl.BlockSpec((1, RW_WIDTH, RW_WIDTH), lambda b, t: (b, 0, 0)),
            pl.BlockSpec((1, RW_IN), vec),
            wv, lora, wv, lora, lora, wv, wv, wv, wv, wv,
            pl.BlockSpec((RW_WIDTH, RW_WIDTH), vec),
        ],
        out_specs=[
            pl.BlockSpec((1, R, RW_WIDTH), lambda b, t: (b, t, 0)),
            pl.BlockSpec((1, 1, RW_IN), lambda b, t: (b, 0, 0)),
            pl.BlockSpec((1, RW_WIDTH, RW_WIDTH), lambda b, t: (b, 0, 0)),
        ],
        out_shape=[
            jax.ShapeDtypeStruct((B, S, RW_WIDTH), BF),
            jax.ShapeDtypeStruct((B, 1, RW_IN), F32),
            jax.ShapeDtypeStruct((B, RW_WIDTH, RW_WIDTH), F32),
        ],
        scratch_shapes=[pltpu.VMEM((1, RW_IN), F32), pltpu.VMEM((RW_WIDTH, RW_WIDTH), F32)],
        compiler_params=_cparams("parallel", "arbitrary"),
        name="rwkv7",
    )(prw, shift0, state_bd, mu, w0, w2p, a0, a2p, g2p, k_k, k_a, r_k, gn_w, gn_b, ones_bd)


def _attn_causal_kernel(q_ref, k_ref, wuv_ref, o_ref, m_ref, acc_ref, *, tq):
    qi = pl.program_id(1)
    q = q_ref[0].reshape(MLA_HEADS * tq, 256)
    m_ref[...] = jnp.full(m_ref.shape, -jnp.inf, F32)
    acc_ref[...] = jnp.zeros(acc_ref.shape, F32)

    def step(kblk, masked):
        s = lax.dot_general(kblk, q, _NT, preferred_element_type=F32)
        if masked:
            lead = s.shape[0] - tq
            kc = lax.broadcasted_iota(jnp.int32, (tq, s.shape[1]), 0) // CHUNK
            qc = (lax.broadcasted_iota(jnp.int32, (tq, s.shape[1]), 1) % tq) // CHUNK
            own = jnp.where(kc <= qc, s[lead:], -jnp.inf)
            s = jnp.concatenate([s[:lead], own], axis=0) if lead else own
        m_prev = m_ref[...]
        m_new = jnp.maximum(m_prev, jnp.max(s, axis=0, keepdims=True))
        alpha = jnp.exp2(m_prev - m_new)
        p = jnp.exp2(s - m_new).astype(BF)
        pv = lax.dot_general(kblk[:, :PV_ROWS], p, _TN, preferred_element_type=F32)
        acc_ref[...] = alpha * acc_ref[...] + pv
        m_ref[...] = m_new

    def body(kj, carry):
        step(k_ref[0, pl.ds(pl.multiple_of(kj * (2 * tq), 2 * tq), 2 * tq), :], False)
        return carry

    lax.fori_loop(0, qi // 2, body, 0)

    @pl.when(qi % 2 == 1)
    def _():
        step(k_ref[0, pl.ds(pl.multiple_of((qi - 1) * tq, 2 * tq), 2 * tq), :], True)

    @pl.when(qi % 2 == 0)
    def _():
        step(k_ref[0, pl.ds(pl.multiple_of(qi * tq, tq), tq), :], True)


    o = (acc_ref[0:KV_LORA, :] / acc_ref[KV_LORA:KV_LORA + 1, :]).astype(BF)
    for hp in range(MLA_HEADS // 2):
        yt = [jnp.dot(wuv_ref[h], o[:, h * tq:(h + 1) * tq], preferred_element_type=F32)
              for h in (2 * hp, 2 * hp + 1)]
        o_ref[0, :, hp * LANES:(hp + 1) * LANES] = jnp.concatenate(yt, axis=0).T.astype(BF)


def _attn_causal(qcat, kcat, wuv_t):
    B, H, S, _ = qcat.shape
    tq = min(ATT_TILE, S)
    return pl.pallas_call(
        functools.partial(_attn_causal_kernel, tq=tq),
        grid=(B, S // tq),
        in_specs=[
            pl.BlockSpec((1, H, tq, 256), lambda b, i: (b, 0, i, 0)),
            pl.BlockSpec((1, S, 256), lambda b, i: (b, 0, 0)),
            pl.BlockSpec((H, V_HEAD, KV_LORA), lambda b, i: (0, 0, 0)),
        ],
        out_specs=pl.BlockSpec((1, tq, H * V_HEAD), lambda b, i: (b, i, 0)),
        out_shape=jax.ShapeDtypeStruct((B, S, H * V_HEAD), BF),
        scratch_shapes=[pltpu.VMEM((1, H * tq), F32), pltpu.VMEM((PV_ROWS, H * tq), F32)],
        compiler_params=_cparams("parallel", "arbitrary"),
        name="mla_causal",
    )(qcat, kcat, wuv_t)


def _attn_full_kernel(q_ref, ckv_ref, kr_ref, kn_ref, wuv_ref, o_ref, *, S):
    q = q_ref[0].reshape(MLA_HEADS * S, 256)
    ckv = ckv_ref[0].astype(BF)
    kr = kr_ref[0].astype(BF)
    kn = kn_ref[0]
    sel = (lax.broadcasted_iota(jnp.int32, (LANES, QK_ROPE), 0)
           == lax.broadcasted_iota(jnp.int32, (LANES, QK_ROPE), 1) + ROPE_LANE0)
    q_rope = jnp.dot(q[:, LANES:], jnp.where(sel, 1.0, 0.0).astype(BF),
                     preferred_element_type=F32).astype(BF)
    s1 = (lax.dot_general(q[:, :LANES], ckv, _NT, preferred_element_type=F32)
          + lax.dot_general(q_rope, kr, _NT, preferred_element_type=F32))
    s2 = lax.dot_general(q, kn, _NT, preferred_element_type=F32)
    m = jnp.maximum(jnp.max(s1, axis=-1, keepdims=True), jnp.max(s2, axis=-1, keepdims=True))
    p1 = jnp.exp2(s1 - m)
    p2 = jnp.exp2(s2 - m)
    l = jnp.sum(p1, axis=-1, keepdims=True) + jnp.sum(p2, axis=-1, keepdims=True)
    o = ((jnp.dot(p1.astype(BF), ckv, preferred_element_type=F32)
          + jnp.dot(p2.astype(BF), kn[:, :KV_LORA], preferred_element_type=F32)) / l).astype(BF)
    y = [jnp.dot(o[h * S:(h + 1) * S], wuv_ref[h], preferred_element_type=F32)
         for h in range(MLA_HEADS)]
    o_ref[0] = jnp.concatenate(y, axis=-1).astype(BF)


def _attn_full(qcat, past_ckv, past_kr, knew, wuv_h, l):
    B, H, S, _ = qcat.shape
    P = past_ckv.shape[2]
    return pl.pallas_call(
        functools.partial(_attn_full_kernel, S=S),
        grid=(B,),
        in_specs=[
            pl.BlockSpec((1, H, S, 256), lambda b: (b, 0, 0, 0)),
            pl.BlockSpec((None, 1, P, KV_LORA), lambda b: (l, b, 0, 0)),
            pl.BlockSpec((None, 1, P, QK_ROPE), lambda b: (l, b, 0, 0)),
            pl.BlockSpec((1, S, 256), lambda b: (b, 0, 0)),
            pl.BlockSpec((H, KV_LORA, V_HEAD), lambda b: (0, 0, 0)),
        ],
        out_specs=pl.BlockSpec((1, S, H * V_HEAD), lambda b: (b, 0, 0)),
        out_shape=jax.ShapeDtypeStruct((B, S, H * V_HEAD), BF),
        compiler_params=_cparams("parallel"),
        name="mla_full",
    )(qcat, past_ckv, past_kr, knew, wuv_h)


def _out_kernel(x_ref, yrg_ref, yrw_ref, ymla_ref, wout_ref, o_ref):
    ycat = jnp.concatenate([yrg_ref[...], yrw_ref[...], ymla_ref[...]], axis=-1)
    o_ref[...] = x_ref[...] + jnp.dot(ycat, wout_ref[...], preferred_element_type=F32)


def _out_proj(x2d, yrg, yrw, ymla, wout):
    T = x2d.shape[0]
    tm = min(ROW_TILE, T)
    row = lambda i: (i, 0)
    const = lambda i: (0, 0)
    return pl.pallas_call(
        _out_kernel,
        grid=(T // tm,),
        in_specs=[
            pl.BlockSpec((tm, D_MODEL), row),
            pl.BlockSpec((tm, RG_WIDTH), row),
            pl.BlockSpec((tm, RW_WIDTH), row),
            pl.BlockSpec((tm, MLA_HEADS * V_HEAD), row),
            pl.BlockSpec((D_MODEL, D_MODEL), const),
        ],
        out_specs=pl.BlockSpec((tm, D_MODEL), row),
        out_shape=jax.ShapeDtypeStruct((T, D_MODEL), F32),
        compiler_params=_cparams("parallel"),
        name="out_proj",
    )(x2d, yrg, yrw, ymla, wout)


def _mem_kernel(x_ref, g_ref, wq_ref, mk_ref, mv_ref, wo_ref, o_ref):
    x = x_ref[...]
    hn = _rms(x, g_ref[...]).astype(BF)
    q = jnp.dot(hn, wq_ref[...], preferred_element_type=F32).astype(BF)
    by_head = len(mk_ref.shape) == 4
    outs = []
    for h in range(MEM_HEADS):
        sl = slice(h * MEM_HEAD, (h + 1) * MEM_HEAD)
        mk_h = mk_ref[0, :, h, :].astype(BF) if by_head else mk_ref[0, :, sl]
        mv_h = mv_ref[0, :, h, :].astype(BF) if by_head else mv_ref[0, :, sl]
        s = lax.dot_general(q[:, sl], mk_h, _NT, preferred_element_type=F32) * MEM_SCALE
        e = jnp.exp(s - jnp.max(s, axis=-1, keepdims=True))
        pr = e * (1.0 / jnp.sum(e, axis=-1, keepdims=True))
        outs.append(jnp.dot(pr.astype(BF), mv_h, preferred_element_type=F32).astype(BF))
    o = jnp.concatenate(outs, axis=-1)
    o_ref[...] = x + jnp.dot(o, wo_ref[...], preferred_element_type=F32)


def _mem_attn(x2d, B, S, g, wq, mk, mv, wo, l):
    T = B * S
    tm = min(ROW_TILE, S)
    nst = S // tm
    row = lambda i: (i, 0)
    const = lambda i: (0, 0)
    if mk.ndim == 5:
        mem_spec = pl.BlockSpec((None, 1, N_MEM, MEM_HEADS, MEM_HEAD),
                                lambda i: (l, i // nst, 0, 0, 0))
    else:
        mem_spec = pl.BlockSpec((1, N_MEM, D_MODEL), lambda i: (i // nst, 0, 0))
    return pl.pallas_call(
        _mem_kernel,
        grid=(T // tm,),
        in_specs=[
            pl.BlockSpec((tm, D_MODEL), row),
            pl.BlockSpec((1, D_MODEL), const),
            pl.BlockSpec((D_MODEL, D_MODEL), const),
            mem_spec,
            mem_spec,
            pl.BlockSpec((D_MODEL, D_MODEL), const),
        ],
        out_specs=pl.BlockSpec((tm, D_MODEL), row),
        out_shape=jax.ShapeDtypeStruct((T, D_MODEL), F32),
        compiler_params=_cparams("parallel"),
        name="mem_attn",
    )(x2d, g, wq, mk, mv, wo)


def _ffn_kernel(x_ref, g_ref, w1_ref, w2_ref, gf_ref, o_ref, *, final):
    x = x_ref[...]
    xn = _rms(x, g_ref[...]).astype(BF)
    acc = x
    for c in range(D_FF // FF_TILE):
        h = jnp.dot(xn, w1_ref[:, c * FF_TILE:(c + 1) * FF_TILE], preferred_element_type=F32)
        h = jnp.square(jnp.maximum(h, 0.0)).astype(BF)
        acc = acc + jnp.dot(h, w2_ref[c * FF_TILE:(c + 1) * FF_TILE, :], preferred_element_type=F32)
    if final:
        acc = _rms(acc, gf_ref[...])
    o_ref[...] = acc


def _ffn(x2d, g, w1, w2, gf, final):
    T = x2d.shape[0]
    tm = min(FFN_ROW_TILE, T)
    const = lambda i: (0, 0)
    return pl.pallas_call(
        functools.partial(_ffn_kernel, final=final),
        grid=(T // tm,),
        in_specs=[
            pl.BlockSpec((tm, D_MODEL), lambda i: (i, 0)),
            pl.BlockSpec((1, D_MODEL), const),
            pl.BlockSpec((D_MODEL, D_FF), const, pipeline_mode=pl.Buffered(1)),
            pl.BlockSpec((D_FF, D_MODEL), const, pipeline_mode=pl.Buffered(1)),
            pl.BlockSpec((1, D_MODEL), const),
        ],
        out_specs=pl.BlockSpec((tm, D_MODEL), lambda i: (i, 0)),
        out_shape=jax.ShapeDtypeStruct((T, D_MODEL), F32),
        compiler_params=_cparams("parallel"),
        name="ffn",
    )(x2d, g, w1, w2, gf)


def _memproj_kernel(*refs, l):
    m_ref, g_ref, w_ref = refs[:3]
    prev = refs[3:5] if l else ()
    k_ref, v_ref, kb_ref, vb_ref = refs[3 + len(prev):]
    for src, dst in zip(prev, (k_ref, v_ref)):
        dst[0:l] = src[...]
    mn = _rms(m_ref[...], g_ref[...]).astype(BF)
    kv = jnp.dot(mn, w_ref[...], preferred_element_type=F32)
    k = kv[:, :D_MODEL]
    v = kv[:, D_MODEL:]
    for h in range(MEM_HEADS):
        k_ref[l, :, h, :] = k[:, h * MEM_HEAD:(h + 1) * MEM_HEAD]
        v_ref[l, :, h, :] = v[:, h * MEM_HEAD:(h + 1) * MEM_HEAD]
    kb_ref[...] = k.astype(BF)
    vb_ref[...] = v.astype(BF)


def _mem_project(mem2d, g, wkv, l, prev):
    T = mem2d.shape[0]
    tm = min(MEMPROJ_TILE, T)
    row = lambda i: (i, 0)
    const = lambda i: (0, 0)
    prev = () if prev is None else prev
    stacked = lambda n: pl.BlockSpec((n, tm, MEM_HEADS, MEM_HEAD), lambda i: (0, i, 0, 0))
    return pl.pallas_call(
        functools.partial(_memproj_kernel, l=l),
        grid=(T // tm,),
        in_specs=[
            pl.BlockSpec((tm, D_MODEL), row),
            pl.BlockSpec((1, D_MODEL), const),
            pl.BlockSpec((D_MODEL, 2 * D_MODEL), const),
        ] + [stacked(l) for _ in prev],
        out_specs=[stacked(l + 1), stacked(l + 1),
                   pl.BlockSpec((tm, D_MODEL), row), pl.BlockSpec((tm, D_MODEL), row)],
        out_shape=[jax.ShapeDtypeStruct((l + 1, T, MEM_HEADS, MEM_HEAD), F32)] * 2
                  + [jax.ShapeDtypeStruct((T, D_MODEL), BF)] * 2,
        compiler_params=_cparams("parallel"),
        name="mem_project",
    )(mem2d, g, wkv, *prev)


def _fold_kernel(a_ref, b_ref, o_ref):
    o_ref[0] = _mm3(a_ref[0], b_ref[0])


def _fold_heads(a, b):
    G, m, k = a.shape
    n = b.shape[2]
    return pl.pallas_call(
        _fold_kernel,
        grid=(G,),
        in_specs=[pl.BlockSpec((1, m, k), lambda g: (g, 0, 0)),
                  pl.BlockSpec((1, k, n), lambda g: (g, 0, 0))],
        out_specs=pl.BlockSpec((1, m, n), lambda g: (g, 0, 0)),
        out_shape=jax.ShapeDtypeStruct((G, m, n), F32),
        compiler_params=_cparams("parallel"),
        name="fold_heads",
    )(a, b)


def _block_diag(blocks):
    G, m, n = blocks.shape
    eye = jnp.eye(G, dtype=blocks.dtype)
    return jnp.einsum('gmn,gh->gmhn', blocks, eye).reshape(G * m, G * n)


def _prep_weights(l, W):
    r2 = lambda v: v.reshape(1, -1)
    w_in = W['w_in'][l]
    kr_cols = w_in[:, 1792:1824]
    kr_swap = jnp.concatenate([kr_cols[:, 16:], kr_cols[:, :16]], axis=1)
    w_ext = jnp.concatenate([w_in[:, :1792], jnp.tile(kr_cols, (1, 4)), jnp.tile(kr_swap, (1, 4))],
                            axis=1).astype(BF)
    w_uq = W['mla_w_uq'][l]
    rope = w_uq[:, :, QK_NOPE:]
    rope_swap = jnp.concatenate([rope[:, :, 16:], rope[:, :, :16]], axis=2)
    w_fold = _fold_heads(jnp.transpose(w_uq[:, :, :QK_NOPE], (1, 0, 2)),
                         jnp.transpose(W['mla_w_uk'][l], (1, 2, 0)))
    wq_ext = jnp.concatenate([jnp.transpose(w_fold, (1, 0, 2)).reshape(Q_LORA, -1),
                              rope.reshape(Q_LORA, -1), rope_swap.reshape(Q_LORA, -1)],
                             axis=1).astype(BF)
    w_uv = W['mla_w_uv'][l]
    wuv_t = jnp.transpose(w_uv, (1, 2, 0)).astype(BF)
    wuv_h = jnp.transpose(w_uv, (1, 0, 2)).astype(BF)
    zpad = lambda top, mat, bot: jnp.concatenate(
        [jnp.zeros((top, RW_WIDTH), F32), mat, jnp.zeros((bot, RW_WIDTH), F32)], axis=0)
    return dict(
        norm_mix=r2(W['norm_mix'][l]), w_ext=w_ext, gq=r2(W['mla_q_norm'][l]),
        gkv=r2(W['mla_kv_norm'][l]), wq_ext=wq_ext, wuv_t=wuv_t, wuv_h=wuv_h,
        cw=W['rg_conv_w'][l], cb=r2(W['rg_conv_b'][l]),
        wa_bd=_block_diag(W['rg_wa'][l]).astype(BF), ba=r2(W['rg_ba'][l]),
        wx_bd=_block_diag(W['rg_wx'][l]).astype(BF), bx=r2(W['rg_bx'][l]),
        lam=r2(W['rg_lambda'][l]),
        mu=r2(W['rw_mu'][l]), w0=r2(W['rw_w0'][l]), w2p=zpad(0, W['rw_w2'][l], 96),
        a0=r2(W['rw_a0'][l]), a2p=zpad(32, W['rw_a2'][l], 64), g2p=zpad(64, W['rw_g2'][l], 0),
        k_k=r2(W['rw_k_k'][l]), k_a=r2(W['rw_k_a'][l]), r_k=r2(W['rw_r_k'][l]),
        gn_w=r2(W['rw_gn_w'][l]), gn_b=r2(W['rw_gn_b'][l]),
        w_out=W['w_out'][l].astype(BF),
        norm_mem=r2(W['norm_mem'][l]), wq=W['mem_w_q'][l].reshape(D_MODEL, D_MODEL).astype(BF),
        wo=W['mem_w_o'][l].reshape(D_MODEL, D_MODEL).astype(BF),
        norm_ffn=r2(W['norm_ffn'][l]), w1=W['ffn_w1'][l].astype(BF), w2=W['ffn_w2'][l].astype(BF),
        norm_mem_kv=r2(W['norm_mem_kv'][l]),
        wkv=jnp.concatenate([W['mem_w_k'][l].reshape(D_MODEL, D_MODEL),
                             W['mem_w_v'][l].reshape(D_MODEL, D_MODEL)], axis=1).astype(BF),
    )


def _rope_tables(pos, rows):
    half = QK_ROPE // 2
    inv = ROPE_BASE ** (-jnp.arange(half, dtype=F32) / half)
    ang = pos.astype(F32)[:, None] * inv
    cos, sin = jnp.cos(ang), jnp.sin(ang)
    cosq = jnp.tile(jnp.concatenate([cos, cos], axis=1), (1, MLA_HEADS))
    sinq = jnp.tile(jnp.concatenate([-sin, sin], axis=1), (1, MLA_HEADS))
    reps = max(1, rows // pos.shape[0])
    return jnp.tile(cosq, (reps, 1)), jnp.tile(sinq, (reps, 1))


def _layer(x2d, B, S, P, tabs, rg_buf, rg_h, rw_shift, rw_state, past, mem_k, mem_v, gf, l, depth,
           prev_kv):
    cosq, sinq = tabs
    final = l == depth - 1
    pa, prw, ckv, kr, kcat, qcat = _in_proj(x2d, B, S, P['norm_mix'], P['w_ext'], P['gq'], P['gkv'],
                                            P['wq_ext'], cosq, sinq, l, prev_kv)
    buf8 = jnp.concatenate([jnp.zeros((B, 5, RG_WIDTH), F32), rg_buf], axis=1)
    yrg, buf_new, h_new = _rglru(pa.reshape(B, S, 512), buf8, rg_h.reshape(B, 1, RG_WIDTH),
                                 P['cw'], P['cb'], P['wa_bd'], P['ba'], P['wx_bd'], P['bx'], P['lam'])
    eye_h = jnp.eye(RW_HEADS, dtype=F32)
    state_bd = jnp.einsum('bhij,hg->bhigj', rw_state, eye_h).reshape(B, RW_WIDTH, RW_WIDTH)
    ones_bd = _block_diag(jnp.ones((RW_HEADS, RW_HEAD, RW_HEAD), F32)).astype(BF)
    yrw, shift_new, state_new = _rwkv(prw.reshape(B, S, RW_IN), rw_shift.reshape(B, 1, RW_IN), state_bd,
                                      P['mu'], P['w0'], P['w2p'], P['a0'], P['a2p'], P['g2p'],
                                      P['k_k'], P['k_a'], P['r_k'], P['gn_w'], P['gn_b'], ones_bd)
    wkv_new = jnp.einsum('bhigj,hg->bhij',
                         state_new.reshape(B, RW_HEADS, RW_HEAD, RW_HEADS, RW_HEAD), eye_h)
    kcat3 = kcat.reshape(B, S, 256)
    if past is None:
        ymla = _attn_causal(qcat, kcat3, P['wuv_t'])
    else:
        ymla = _attn_full(qcat, past[0], past[1], kcat3, P['wuv_h'], l)
    x1 = _out_proj(x2d, yrg.reshape(B * S, RG_WIDTH), yrw.reshape(B * S, RW_WIDTH),
                   ymla.reshape(B * S, MLA_HEADS * V_HEAD), P['w_out'])
    x2 = _mem_attn(x1, B, S, P['norm_mem'], P['wq'], mem_k, mem_v, P['wo'], l)
    x3 = _ffn(x2, P['norm_ffn'], P['w1'], P['w2'], gf, final)
    new_state = (buf_new[:, 5:8], h_new.reshape(B, RG_WIDTH), shift_new.reshape(B, RW_IN), wkv_new)
    return x3, new_state, (ckv, kr)


def kernel(x_prompt, x_sample, state_rg_conv, state_rg_h, state_rw_shift, state_rw_wkv, cache_mla_ckv, cache_mla_krope, cache_mem_k, cache_mem_v, mem_prompt, norm_mix, w_in, rg_conv_w, rg_conv_b, rg_wa, rg_ba, rg_wx, rg_bx, rg_lambda, rw_mu, rw_w0, rw_w2, rw_a0, rw_a2, rw_g2, rw_k_k, rw_k_a, rw_r_k, rw_gn_w, rw_gn_b, mla_q_norm, mla_kv_norm, mla_w_uq, mla_w_uk, mla_w_uv, w_out, norm_mem, norm_mem_kv, mem_w_q, mem_w_k, mem_w_v, mem_w_o, norm_ffn, ffn_w1, ffn_w2, norm_final):
    W = dict(norm_mix=norm_mix, w_in=w_in, rg_conv_w=rg_conv_w, rg_conv_b=rg_conv_b, rg_wa=rg_wa,
             rg_ba=rg_ba, rg_wx=rg_wx, rg_bx=rg_bx, rg_lambda=rg_lambda, rw_mu=rw_mu, rw_w0=rw_w0,
             rw_w2=rw_w2, rw_a0=rw_a0, rw_a2=rw_a2, rw_g2=rw_g2, rw_k_k=rw_k_k, rw_k_a=rw_k_a,
             rw_r_k=rw_r_k.reshape(rw_r_k.shape[0], RW_WIDTH), rw_gn_w=rw_gn_w, rw_gn_b=rw_gn_b,
             mla_q_norm=mla_q_norm, mla_kv_norm=mla_kv_norm, mla_w_uq=mla_w_uq, mla_w_uk=mla_w_uk,
             mla_w_uv=mla_w_uv, w_out=w_out, norm_mem=norm_mem, norm_mem_kv=norm_mem_kv,
             mem_w_q=mem_w_q, mem_w_k=mem_w_k, mem_w_v=mem_w_v, mem_w_o=mem_w_o, norm_ffn=norm_ffn,
             ffn_w1=ffn_w1, ffn_w2=ffn_w2)
    depth = norm_mix.shape[0]
    Bp, Sp, _ = x_prompt.shape
    Bs, Ss, _ = x_sample.shape
    past_len = cache_mla_ckv.shape[2]
    tabs_p = _rope_tables(jnp.arange(Sp, dtype=jnp.int32), min(ROW_TILE, Sp))
    tabs_s = _rope_tables(past_len + jnp.arange(Ss, dtype=jnp.int32), min(ROW_TILE, Ss))
    gf = norm_final.reshape(1, D_MODEL)

    xp = x_prompt.reshape(Bp * Sp, D_MODEL)
    xs = x_sample.reshape(Bs * Ss, D_MODEL)
    z = lambda *s: jnp.zeros(s, F32)
    p_states, s_states = [], []
    mem_kv = kv_p = kv_s = None
    for l in range(depth):
        P = _prep_weights(l, W)
        mk, mv, mkb, mvb = _mem_project(mem_prompt.reshape(Bp * N_MEM, D_MODEL), P['norm_mem_kv'],
                                        P['wkv'], l, mem_kv)
        mem_kv = (mk, mv)
        xp, st_p, kv_p = _layer(xp, Bp, Sp, P, tabs_p, z(Bp, 3, RG_WIDTH), z(Bp, RG_WIDTH),
                                z(Bp, RW_IN), z(Bp, RW_HEADS, RW_HEAD, RW_HEAD), None,
                                mkb.reshape(Bp, N_MEM, D_MODEL), mvb.reshape(Bp, N_MEM, D_MODEL),
                                gf, l, depth, kv_p)
        xs, st_s, kv_s = _layer(xs, Bs, Ss, P, tabs_s, state_rg_conv[l], state_rg_h[l],
                                state_rw_shift[l], state_rw_wkv[l],
                                (cache_mla_ckv, cache_mla_krope),
                                cache_mem_k, cache_mem_v, gf, l, depth, kv_s)
        p_states.append(st_p)
        s_states.append(st_s)

    sp = [jnp.stack(t) for t in zip(*p_states)]
    ss = [jnp.stack(t) for t in zip(*s_states)]
    mem_shape = (depth, Bp, N_MEM, MEM_HEADS, MEM_HEAD)
    return (xp.reshape(Bp, Sp, D_MODEL), xs.reshape(Bs, Ss, D_MODEL),
            sp[0], sp[1], sp[2], sp[3],
            kv_p[0].reshape(depth, Bp, Sp, KV_LORA), kv_p[1].reshape(depth, Bp, Sp, QK_ROPE),
            mem_kv[0].reshape(mem_shape), mem_kv[1].reshape(mem_shape),
            ss[0], ss[1], ss[2], ss[3],
            kv_s[0].reshape(depth, Bs, Ss, KV_LORA), kv_s[1].reshape(depth, Bs, Ss, QK_ROPE))
```

```python
import functools
import math

import jax
import jax.numpy as jnp
from jax import lax
from jax.experimental import pallas as pl
from jax.experimental.pallas import tpu as pltpu

BF = jnp.bfloat16
F32 = jnp.float32

D_MODEL = 1024
RG_WIDTH = 256
RG_BLOCKS = 4
CONV_W = 4
LRU_C = 8.0
RW_HEADS = 4
RW_HEAD = 64
RW_WIDTH = 256
RW_IN = 896
GN_EPS = 64e-5
L2_EPS = 1e-12
MLA_HEADS = 8
QK_NOPE = 64
QK_ROPE = 32
V_HEAD = 64
Q_LORA = 256
KV_LORA = 128
MLA_SCALE = (QK_NOPE + QK_ROPE) ** -0.5
SCALE_LOG2E = MLA_SCALE * math.log2(math.e)
ROPE_BASE = 10000.0
CHUNK = 64
N_MEM = 256
MEM_HEADS = 4
MEM_HEAD = 256
MEM_SCALE = MEM_HEAD ** -0.5
D_FF = 4096
EPS = 1e-6

LANES = 128
VMEM_LIMIT = 52 * 1024 * 1024

ROW_TILE = 1024
RG_TILE = 512
RW_TILE = 1024
RW_CHUNK = 64
RW_PASSES_GRAM = 1
RW_PASSES_INV = 1
RW_PASSES_LOCAL = 1
RW_PASSES_STATE = 1
ATT_TILE = 512
IN_SPLIT = 4
ROPE_LANE0 = 8
PV_ROWS = 144
MEMPROJ_TILE = 256
FFN_ROW_TILE = 1024
FF_TILE = 2048


def _cparams(*sem):
    return pltpu.CompilerParams(dimension_semantics=sem, vmem_limit_bytes=VMEM_LIMIT)


def _rms(x, g):
    return x * lax.rsqrt(jnp.mean(x * x, axis=-1, keepdims=True) + EPS) * g


def _dot(a, b):
    return jnp.dot(a.astype(BF), b.astype(BF), preferred_element_type=F32)


def _dot_nt(a, b):
    return lax.dot_general(a.astype(BF), b.astype(BF), (((1,), (1,)), ((), ())),
                           preferred_element_type=F32)


def _split(x):
    hi = x.astype(BF)
    lo = (x - hi.astype(F32)).astype(BF)
    return hi, lo


def _mm3(a, b, dims=(((1,), (0,)), ((), ())), passes=3):
    dg = functools.partial(lax.dot_general, dimension_numbers=dims, preferred_element_type=F32)
    if passes == 1:
        return dg(a.astype(BF), b.astype(BF))
    ah, al = _split(a)
    bh, bl = _split(b)
    return dg(ah, bh) + (dg(ah, bl) + dg(al, bh))


_NT = (((1,), (1,)), ((), ()))
_TN = (((0,), (0,)), ((), ()))


def _softplus(x):
    return jnp.maximum(x, 0.0) + jnp.log1p(jnp.exp(-jnp.abs(x)))


def _sigmoid(x):
    return 0.5 * jnp.tanh(0.5 * x) + 0.5


def _gelu_tanh(x):
    c = math.sqrt(2.0 / math.pi)
    return 0.5 * x * (1.0 + jnp.tanh(c * (x + 0.044715 * (x * x * x))))


def _shift_rows(x, d):
    return pltpu.roll(x, d, 0)


def _in_kernel(x_ref, g_ref, w_ref, gq_ref, gkv_ref, wq_ref, cos_ref, sin_ref, *rest, tm, l, seq):
    prev = rest[:2] if l else ()
    pa_ref, prw_ref, ckv_ref, kr_ref, kcat_ref, qcat_ref = rest[len(prev):]
    for src, dst in zip(prev, (ckv_ref, kr_ref)):
        dst[0:l] = src[...]
    lane = lax.broadcasted_iota(jnp.int32, (1, LANES), 1)
    rope_lanes = (lane >= ROPE_LANE0) & (lane < ROPE_LANE0 + QK_ROPE)

    def rows(sl):
        xn = _rms(x_ref[sl, :], g_ref[...]).astype(BF)
        p = jnp.dot(xn, w_ref[...], preferred_element_type=F32)
        pa_ref[sl, :] = p[:, :512]
        prw_ref[sl, :] = p[:, 512:1408]
        cq = p[:, 1408:1664]
        ckv = p[:, 1664:1792]
        ka = p[:, 1792:1920]
        kb = p[:, 1920:2048]
        cos = cos_ref[sl, :]
        sin = sin_ref[sl, :]
        ckvn = _rms(ckv, gkv_ref[...])
        krt = ka * cos[:, :LANES] + kb * sin[:, :LANES]
        ckv_ref[l, sl, :] = ckvn
        kr_ref[l, sl, :] = krt[:, :QK_ROPE]
        kext = jnp.where(lane == 0, 1.0,
                         jnp.where(rope_lanes, pltpu.roll(krt, ROPE_LANE0, 1), 0.0))
        kcat_ref[sl, :] = jnp.concatenate([ckvn, kext], axis=-1).astype(BF)

        cqn = _rms(cq, gq_ref[...]).astype(BF)
        qq = jnp.dot(cqn, wq_ref[...], preferred_element_type=F32)
        qlat = qq[:, :1024]
        rr = qq[:, 1024:1280] * cos + qq[:, 1280:1536] * sin
        for h in range(MLA_HEADS):
            g = h // 4
            shift = (ROPE_LANE0 - QK_ROPE * (h % 4)) % LANES
            rpart = jnp.where(rope_lanes,
                              pltpu.roll(rr[:, g * LANES:(g + 1) * LANES], shift, 1), 0.0)
            ql = (qlat[:, h * LANES:(h + 1) * LANES] * SCALE_LOG2E).astype(BF)
            qr = (rpart * SCALE_LOG2E).astype(BF)
            if seq >= tm:
                qcat_ref[0, h, sl, :LANES] = ql
                qcat_ref[0, h, sl, LANES:] = qr
            else:
                b0, nb = sl.start // seq, (sl.stop - sl.start) // seq
                qcat_ref[b0:b0 + nb, h, :, :LANES] = ql.reshape(nb, seq, LANES)
                qcat_ref[b0:b0 + nb, h, :, LANES:] = qr.reshape(nb, seq, LANES)

    step = tm // IN_SPLIT
    if tm % (16 * IN_SPLIT) or (seq < tm and step % seq):
        step = tm
    for r0 in range(0, tm, step):
        rows(slice(r0, r0 + step))


def _in_tile(B, S):
    if S >= ROW_TILE:
        return ROW_TILE
    return S * max(1, min(B, ROW_TILE // S))


def _in_proj(x2d, B, S, g, w_ext, gq, gkv, wq_ext, cosq, sinq, l, prev):
    T = B * S
    prev = () if prev is None else prev
    tm = _in_tile(B, S)
    nst = max(1, S // tm)
    ntab = cosq.shape[0] // tm
    if S >= tm:
        q_spec = pl.BlockSpec((1, MLA_HEADS, tm, 256), lambda i: (i // nst, 0, i % nst, 0))
    else:
        q_spec = pl.BlockSpec((tm // S, MLA_HEADS, S, 256), lambda i: (i, 0, 0, 0))
    row = lambda i: (i, 0)
    const = lambda i: (0, 0)
    outs = pl.pallas_call(
        functools.partial(_in_kernel, tm=tm, l=l, seq=S),
        grid=(T // tm,),
        in_specs=[
            pl.BlockSpec((tm, D_MODEL), row),
            pl.BlockSpec((1, D_MODEL), const),
            pl.BlockSpec((D_MODEL, 2048), const),
            pl.BlockSpec((1, Q_LORA), const),
            pl.BlockSpec((1, KV_LORA), const),
            pl.BlockSpec((Q_LORA, 1536), const),
            pl.BlockSpec((tm, 256), lambda i: (i % ntab, 0)),
            pl.BlockSpec((tm, 256), lambda i: (i % ntab, 0)),
        ] + [pl.BlockSpec((l, tm, a.shape[2]), lambda i: (0, i, 0)) for a in prev],
        out_specs=[
            pl.BlockSpec((tm, 512), row),
            pl.BlockSpec((tm, RW_IN), row),
            pl.BlockSpec((l + 1, tm, KV_LORA), lambda i: (0, i, 0)),
            pl.BlockSpec((l + 1, tm, QK_ROPE), lambda i: (0, i, 0)),
            pl.BlockSpec((tm, 256), row),
            q_spec,
        ],
        out_shape=[
            jax.ShapeDtypeStruct((T, 512), F32),
            jax.ShapeDtypeStruct((T, RW_IN), F32),
            jax.ShapeDtypeStruct((l + 1, T, KV_LORA), F32),
            jax.ShapeDtypeStruct((l + 1, T, QK_ROPE), F32),
            jax.ShapeDtypeStruct((T, 256), BF),
            jax.ShapeDtypeStruct((B, MLA_HEADS, S, 256), BF),
        ],
        compiler_params=_cparams("parallel"),
        name="in_proj",
    )(x2d, g, w_ext, gq, gkv, wq_ext, cosq, sinq, *prev)
    return outs


def _rglru_kernel(pa_ref, buf0_ref, h0_ref, cw_ref, cb_ref, wa_ref, ba_ref, wx_ref, bx_ref,
                  lam_ref, y_ref, buf_ref, hout_ref, ext_ref, h_ref, *, R):
    t = pl.program_id(1)

    @pl.when(t == 0)
    def _():
        ext_ref[0:8, :] = buf0_ref[0]
        h_ref[...] = h0_ref[0]

    pa = pa_ref[0]
    xr = pa[:, :RG_WIDTH]
    gate = pa[:, RG_WIDTH:]
    ext_ref[8:8 + R, :] = xr
    cw = cw_ref[...]
    conv = (cb_ref[...] + cw[3:4] * xr + cw[2:3] * ext_ref[7:7 + R, :]
            + cw[1:2] * ext_ref[6:6 + R, :] + cw[0:1] * ext_ref[5:5 + R, :])
    hist = ext_ref[R:R + 8, :]
    ext_ref[0:8, :] = hist
    buf_ref[0] = hist

    ra = _sigmoid(_dot(conv, wa_ref[...]) + ba_ref[...])
    ia = _sigmoid(_dot(conv, wx_ref[...]) + bx_ref[...])
    log_a = (-LRU_C) * ra * _softplus(-lam_ref[...])
    a = jnp.exp(log_a)
    om = -jnp.tanh(log_a) * (a * a + 1.0)
    b = jnp.where(om > 0.0, om * lax.rsqrt(om), 0.0) * (ia * conv)

    rows = lax.broadcasted_iota(jnp.int32, (R, RG_WIDTH), 0)
    d = 1
    while d < R:
        m = rows >= d
        b = jnp.where(m, a * _shift_rows(b, d), 0.0) + b
        a = jnp.where(m, a * _shift_rows(a, d), a)
        d *= 2
    h = a * h_ref[...] + b
    hl = h[R - 1:R, :]
    h_ref[...] = hl
    hout_ref[0] = hl
    y_ref[0] = (_gelu_tanh(gate) * h).astype(BF)


def _rglru(pa, buf8, h0, cw, cb, wa_bd, ba, wx_bd, bx, lam):
    B, S, _ = pa.shape
    R = min(RG_TILE, S)
    vec = lambda b, t: (0, 0)
    return pl.pallas_call(
        functools.partial(_rglru_kernel, R=R),
        grid=(B, S // R),
        in_specs=[
            pl.BlockSpec((1, R, 512), lambda b, t: (b, t, 0)),
            pl.BlockSpec((1, 8, RG_WIDTH), lambda b, t: (b, 0, 0)),
            pl.BlockSpec((1, 1, RG_WIDTH), lambda b, t: (b, 0, 0)),
            pl.BlockSpec((CONV_W, RG_WIDTH), vec),
            pl.BlockSpec((1, RG_WIDTH), vec),
            pl.BlockSpec((RG_WIDTH, RG_WIDTH), vec),
            pl.BlockSpec((1, RG_WIDTH), vec),
            pl.BlockSpec((RG_WIDTH, RG_WIDTH), vec),
            pl.BlockSpec((1, RG_WIDTH), vec),
            pl.BlockSpec((1, RG_WIDTH), vec),
        ],
        out_specs=[
            pl.BlockSpec((1, R, RG_WIDTH), lambda b, t: (b, t, 0)),
            pl.BlockSpec((1, 8, RG_WIDTH), lambda b, t: (b, 0, 0)),
            pl.BlockSpec((1, 1, RG_WIDTH), lambda b, t: (b, 0, 0)),
        ],
        out_shape=[
            jax.ShapeDtypeStruct((B, S, RG_WIDTH), BF),
            jax.ShapeDtypeStruct((B, 8, RG_WIDTH), F32),
            jax.ShapeDtypeStruct((B, 1, RG_WIDTH), F32),
        ],
        scratch_shapes=[pltpu.VMEM((R + 8, RG_WIDTH), F32), pltpu.VMEM((1, RG_WIDTH), F32)],
        compiler_params=_cparams("parallel", "arbitrary"),
        name="rglru",
    )(pa, buf8, h0, cw, cb, wa_bd, ba, wx_bd, bx, lam)


def _rwkv_kernel(p_ref, shift0_ref, state0_ref, mu_ref, w0_ref, w2_ref, a0_ref, a2_ref, g2_ref,
                 kk_ref, ka_ref, rk_ref, gnw_ref, gnb_ref, ones_ref,
                 y_ref, shift_ref, state_ref, prev_ref, s_ref, *, R, C):
    t = pl.program_id(1)
    W = RW_WIDTH
    HC = RW_HEADS * C

    @pl.when(t == 0)
    def _():
        prev_ref[...] = shift0_ref[0]
        s_ref[...] = state0_ref[0]

    p = p_ref[0]
    rows = lax.broadcasted_iota(jnp.int32, (R, RW_IN), 0)
    prev = jnp.where(rows == 0, prev_ref[...], _shift_rows(p, 1))
    last = p[R - 1:R, :]
    prev_ref[...] = last
    shift_ref[0] = last
    xs = p + mu_ref[...] * (prev - p)
    r = xs[:, 0:W]
    k = xs[:, W:2 * W]
    v = xs[:, 2 * W:3 * W]
    x4 = xs[:, 3 * W:]
    ones_bd = ones_ref[...]

    def segsum(x):
        hi, lo = _split(x)
        dg = functools.partial(jnp.dot, preferred_element_type=F32)
        return dg(hi, ones_bd) + dg(lo, ones_bd)

    log_w = -_softplus(-(w0_ref[...] + _mm3(jnp.tanh(x4), w2_ref[...]))) - 0.5
    logw = -jnp.exp(log_w)
    a = _sigmoid(a0_ref[...] + _mm3(x4, a2_ref[...]))
    g = _mm3(_sigmoid(x4), g2_ref[...])
    kk = k * kk_ref[...]
    kk = kk * lax.rsqrt(segsum(kk * kk) + L2_EPS)
    k = k * (1.0 + (a - 1.0) * ka_ref[...])
    bonus = segsum(r * k * rk_ref[...]) * v
    kka = kk * a

    lane_head = lax.broadcasted_iota(jnp.int32, (1, W), 1) // RW_HEAD
    hmask = [lane_head == h for h in range(RW_HEADS)]

    def stack(x):
        return jnp.concatenate([jnp.where(hmask[h], x, 0.0) for h in range(RW_HEADS)], axis=0)

    ri = lax.broadcasted_iota(jnp.int32, (HC, HC), 0) % C
    ci = lax.broadcasted_iota(jnp.int32, (HC, HC), 1) % C
    strict = ri > ci
    incl = ri >= ci
    eye_hc = lax.broadcasted_iota(jnp.int32, (HC, HC), 0) == lax.broadcasted_iota(jnp.int32, (HC, HC), 1)
    eye_w = lax.broadcasted_iota(jnp.int32, (W, W), 0) == lax.broadcasted_iota(jnp.int32, (W, W), 1)
    crow = lax.broadcasted_iota(jnp.int32, (R, W), 0) % C

    mm_gram = functools.partial(_mm3, passes=RW_PASSES_GRAM)
    mm_inv = functools.partial(_mm3, passes=RW_PASSES_INV)
    mm_loc = functools.partial(_mm3, passes=RW_PASSES_LOCAL)
    mm_state = functools.partial(_mm3, passes=RW_PASSES_STATE)

    cum = logw
    d = 1
    while d < C:
        cum = cum + jnp.where(crow >= d, _shift_rows(cum, d), 0.0)
        d *= 2
    nch = R // C
    chunks = range(nch)
    sls = [slice(c * C, (c + 1) * C) for c in chunks]
    ends = [cum[(c + 1) * C - 1:(c + 1) * C, :] for c in chunks]
    cl = jnp.concatenate([jnp.broadcast_to(e, (C, W)) for e in ends], axis=0) if nch > 1 \
        else jnp.broadcast_to(ends[0], (C, W))
    w_inc = jnp.exp(cum)
    w_inv = jnp.exp(-cum)
    w_rem = jnp.exp(cl - cum)
    t_a = jnp.exp(cum - logw) * kk
    t_b = kka * w_inv
    t_k = k * w_inv
    t_r = r * w_inc
    t_kt = k * w_rem
    t_bt = kka * w_rem

    a_s = [stack(t_a[sl]) for sl in sls]
    r_s = [stack(t_r[sl]) for sl in sls]
    v_s = [stack(v[sl]) for sl in sls]
    kt_s = [stack(t_kt[sl]) for sl in sls]
    bt_s = [stack(t_bt[sl]) for sl in sls]
    bk = [jnp.concatenate([stack(t_b[sl]), stack(t_k[sl])], axis=0) for sl in sls]
    g_a = [mm_gram(a_s[c], bk[c], _NT) for c in chunks]
    g_r = [mm_gram(r_s[c], bk[c], _NT) for c in chunks]
    n = [jnp.where(strict, -g[:, :HC], 0.0) for g in g_a]
    l_k = [jnp.where(strict, g[:, HC:], 0.0) for g in g_a]
    m_b = [jnp.where(incl, g[:, :HC], 0.0) for g in g_r]
    m_k = [jnp.where(incl, g[:, HC:], 0.0) for g in g_r]

    def level(b):
        return (ri // (2 * b) == ci // (2 * b)) & (ri % (2 * b) >= b) & (ci % (2 * b) < b)

    tinv = [jnp.where(eye_hc, 1.0, jnp.where(level(1), x, 0.0)) for x in n]
    b = 2
    while b < C:
        lower = [jnp.where(level(b), x, 0.0) for x in n]
        tinv = [tinv[c] + mm_inv(mm_inv(tinv[c], lower[c]), tinv[c]) for c in chunks]
        b *= 2

    x1 = [mm_loc(l_k[c], v_s[c]) for c in chunks]
    tu = [mm_loc(tinv[c], jnp.concatenate([x1[c], a_s[c]], axis=1)) for c in chunks]
    mb = [mm_loc(m_b[c], tu[c]) for c in chunks]
    y_loc = [mm_loc(m_k[c], v_s[c]) - mb[c][:, :W] for c in chunks]
    r_t = [r_s[c] - mb[c][:, W:] for c in chunks]
    p_m = [jnp.where(eye_w, jnp.exp(ends[c]), 0.0) - mm_loc(tu[c][:, W:], bt_s[c], _TN) for c in chunks]
    q_m = [mm_loc(v_s[c], kt_s[c], _TN) - mm_loc(tu[c][:, :W], bt_s[c], _TN) for c in chunks]

    st = s_ref[...]
    ys = []
    for c in chunks:
        y_st = mm_state(r_t[c], st, _NT) + y_loc[c]
        yc = y_st[0:C]
        for h in range(1, RW_HEADS):
            yc = yc + y_st[h * C:(h + 1) * C]
        ys.append(yc)
        st = mm_state(st, p_m[c]) + q_m[c]
    s_ref[...] = st

    y = jnp.concatenate(ys, axis=0) if len(ys) > 1 else ys[0]
    state_ref[0] = s_ref[...]
    inv_n = 1.0 / RW_HEAD
    mean = segsum(y) * inv_n
    yc = y - mean
    var = segsum(yc * yc) * inv_n
    yn = yc * lax.rsqrt(var + GN_EPS) * gnw_ref[...] + gnb_ref[...]
    y_ref[0] = ((yn + bonus) * g).astype(BF)


def _rwkv(prw, shift0, state_bd, mu, w0, w2p, a0, a2p, g2p, k_k, k_a, r_k, gn_w, gn_b, ones_bd):
    B, S, _ = prw.shape
    R = min(RW_TILE, S)
    C = min(RW_CHUNK, S)
    vec = lambda b, t: (0, 0)
    wv = pl.BlockSpec((1, RW_WIDTH), vec)
    lora = pl.BlockSpec((LANES, RW_WIDTH), vec)
    return pl.pallas_call(
        functools.partial(_rwkv_kernel, R=R, C=C),
        grid=(B, S // R),
        in_specs=[
            pl.BlockSpec((1, R, RW_IN), lambda b, t: (b, t, 0)),
            pl.BlockSpec((1, 1, RW_IN), lambda b, t: (b, 0, 0)),
            pl.BlockSpec((1, RW_WIDTH, RW_WIDTH), lambda b, t: (b, 0, 0)),
            pl.BlockSpec((1, RW_IN), vec),
            wv, lora, wv, lora, lora, wv, wv, wv, wv, wv,
            pl.BlockSpec((RW_WIDTH, RW_WIDTH), vec),
        ],
        out_specs=[
            pl.BlockSpec((1, R, RW_WIDTH), lambda b, t: (b, t, 0)),
            pl.BlockSpec((1, 1, RW_IN), lambda b, t: (b, 0, 0)),
            pl.BlockSpec((1, RW_WIDTH, RW_WIDTH), lambda b, t: (b, 0, 0)),
        ],
        out_shape=[
            jax.ShapeDtypeStruct((B, S, RW_WIDTH), BF),
            jax.ShapeDtypeStruct((B, 1, RW_IN), F32),
            jax.ShapeDtypeStruct((B, RW_WIDTH, RW_WIDTH), F32),
        ],
        scratch_shapes=[pltpu.VMEM((1, RW_IN), F32), pltpu.VMEM((RW_WIDTH, RW_WIDTH), F32)],
        compiler_params=_cparams("parallel", "arbitrary"),
        name="rwkv7",
    )(prw, shift0, state_bd, mu, w0, w2p, a0, a2p, g2p, k_k, k_a, r_k, gn_w, gn_b, ones_bd)


def _attn_causal_kernel(q_ref, k_ref, wuv_ref, o_ref, m_ref, acc_ref, *, tq):
    qi = pl.program_id(1)
    q = q_ref[0].reshape(MLA_HEADS * tq, 256)
    m_ref[...] = jnp.full(m_ref.shape, -jnp.inf, F32)
    acc_ref[...] = jnp.zeros(acc_ref.shape, F32)

    def step(kblk, masked):
        s = lax.dot_general(kblk, q, _NT, preferred_element_type=F32)
        if masked:
            lead = s.shape[0] - tq
            kc = lax.broadcasted_iota(jnp.int32, (tq, s.shape[1]), 0) // CHUNK
            qc = (lax.broadcasted_iota(jnp.int32, (tq, s.shape[1]), 1) % tq) // CHUNK
            own = jnp.where(kc <= qc, s[lead:], -jnp.inf)
            s = jnp.concatenate([s[:lead], own], axis=0) if lead else own
        m_prev = m_ref[...]
        m_new = jnp.maximum(m_prev, jnp.max(s, axis=0, keepdims=True))
        alpha = jnp.exp2(m_prev - m_new)
        p = jnp.exp2(s - m_new).astype(BF)
        pv = lax.dot_general(kblk[:, :PV_ROWS], p, _TN, preferred_element_type=F32)
        acc_ref[...] = alpha * acc_ref[...] + pv
        m_ref[...] = m_new

    def body(kj, carry):
        step(k_ref[0, pl.ds(pl.multiple_of(kj * (2 * tq), 2 * tq), 2 * tq), :], False)
        return carry

    lax.fori_loop(0, qi // 2, body, 0)

    @pl.when(qi % 2 == 1)
    def _():
        step(k_ref[0, pl.ds(pl.multiple_of((qi - 1) * tq, 2 * tq), 2 * tq), :], True)

    @pl.when(qi % 2 == 0)
    def _():
        step(k_ref[0, pl.ds(pl.multiple_of(qi * tq, tq), tq), :], True)


    o = (acc_ref[0:KV_LORA, :] / acc_ref[KV_LORA:KV_LORA + 1, :]).astype(BF)
    for hp in range(MLA_HEADS // 2):
        yt = [jnp.dot(wuv_ref[h], o[:, h * tq:(h + 1) * tq], preferred_element_type=F32)
              for h in (2 * hp, 2 * hp + 1)]
        o_ref[0, :, hp * LANES:(hp + 1) * LANES] = jnp.concatenate(yt, axis=0).T.astype(BF)


def _attn_causal(qcat, kcat, wuv_t):
    B, H, S, _ = qcat.shape
    tq = min(ATT_TILE, S)
    return pl.pallas_call(
        functools.partial(_attn_causal_kernel, tq=tq),
        grid=(B, S // tq),
        in_specs=[
            pl.BlockSpec((1, H, tq, 256), lambda b, i: (b, 0, i, 0)),
            pl.BlockSpec((1, S, 256), lambda b, i: (b, 0, 0)),
            pl.BlockSpec((H, V_HEAD, KV_LORA), lambda b, i: (0, 0, 0)),
        ],
        out_specs=pl.BlockSpec((1, tq, H * V_HEAD), lambda b, i: (b, i, 0)),
        out_shape=jax.ShapeDtypeStruct((B, S, H * V_HEAD), BF),
        scratch_shapes=[pltpu.VMEM((1, H * tq), F32), pltpu.VMEM((PV_ROWS, H * tq), F32)],
        compiler_params=_cparams("parallel", "arbitrary"),
        name="mla_causal",
    )(qcat, kcat, wuv_t)


def _attn_full_kernel(q_ref, ckv_ref, kr_ref, kn_ref, wuv_ref, o_ref, *, S):
    q = q_ref[0].reshape(MLA_HEADS * S, 256)
    ckv = ckv_ref[0].astype(BF)
    kr = kr_ref[0].astype(BF)
    kn = kn_ref[0]
    sel = (lax.broadcasted_iota(jnp.int32, (LANES, QK_ROPE), 0)
           == lax.broadcasted_iota(jnp.int32, (LANES, QK_ROPE), 1) + ROPE_LANE0)
    q_rope = jnp.dot(q[:, LANES:], jnp.where(sel, 1.0, 0.0).astype(BF),
                     preferred_element_type=F32).astype(BF)
    s1 = (lax.dot_general(q[:, :LANES], ckv, _NT, preferred_element_type=F32)
          + lax.dot_general(q_rope, kr, _NT, preferred_element_type=F32))
    s2 = lax.dot_general(q, kn, _NT, preferred_element_type=F32)
    m = jnp.maximum(jnp.max(s1, axis=-1, keepdims=True), jnp.max(s2, axis=-1, keepdims=True))
    p1 = jnp.exp2(s1 - m)
    p2 = jnp.exp2(s2 - m)
    l = jnp.sum(p1, axis=-1, keepdims=True) + jnp.sum(p2, axis=-1, keepdims=True)
    o = ((jnp.dot(p1.astype(BF), ckv, preferred_element_type=F32)
          + jnp.dot(p2.astype(BF), kn[:, :KV_LORA], preferred_element_type=F32)) / l).astype(BF)
    y = [jnp.dot(o[h * S:(h + 1) * S], wuv_ref[h], preferred_element_type=F32)
         for h in range(MLA_HEADS)]
    o_ref[0] = jnp.concatenate(y, axis=-1).astype(BF)


def _attn_full(qcat, past_ckv, past_kr, knew, wuv_h, l):
    B, H, S, _ = qcat.shape
    P = past_ckv.shape[2]
    return pl.pallas_call(
        functools.partial(_attn_full_kernel, S=S),
        grid=(B,),
        in_specs=[
            pl.BlockSpec((1, H, S, 256), lambda b: (b, 0, 0, 0)),
            pl.BlockSpec((None, 1, P, KV_LORA), lambda b: (l, b, 0, 0)),
            pl.BlockSpec((None, 1, P, QK_ROPE), lambda b: (l, b, 0, 0)),
            pl.BlockSpec((1, S, 256), lambda b: (b, 0, 0)),
            pl.BlockSpec((H, KV_LORA, V_HEAD), lambda b: (0, 0, 0)),
        ],
        out_specs=pl.BlockSpec((1, S, H * V_HEAD), lambda b: (b, 0, 0)),
        out_shape=jax.ShapeDtypeStruct((B, S, H * V_HEAD), BF),
        compiler_params=_cparams("parallel"),
        name="mla_full",
    )(qcat, past_ckv, past_kr, knew, wuv_h)


def _out_kernel(x_ref, yrg_ref, yrw_ref, ymla_ref, wout_ref, o_ref):
    ycat = jnp.concatenate([yrg_ref[...], yrw_ref[...], ymla_ref[...]], axis=-1)
    o_ref[...] = x_ref[...] + jnp.dot(ycat, wout_ref[...], preferred_element_type=F32)


def _out_proj(x2d, yrg, yrw, ymla, wout):
    T = x2d.shape[0]
    tm = min(ROW_TILE, T)
    row = lambda i: (i, 0)
    const = lambda i: (0, 0)
    return pl.pallas_call(
        _out_kernel,
        grid=(T // tm,),
        in_specs=[
            pl.BlockSpec((tm, D_MODEL), row),
            pl.BlockSpec((tm, RG_WIDTH), row),
            pl.BlockSpec((tm, RW_WIDTH), row),
            pl.BlockSpec((tm, MLA_HEADS * V_HEAD), row),
            pl.BlockSpec((D_MODEL, D_MODEL), const),
        ],
        out_specs=pl.BlockSpec((tm, D_MODEL), row),
        out_shape=jax.ShapeDtypeStruct((T, D_MODEL), F32),
        compiler_params=_cparams("parallel"),
        name="out_proj",
    )(x2d, yrg, yrw, ymla, wout)


def _mem_kernel(x_ref, g_ref, wq_ref, mk_ref, mv_ref, wo_ref, o_ref):
    x = x_ref[...]
    hn = _rms(x, g_ref[...]).astype(BF)
    q = jnp.dot(hn, wq_ref[...], preferred_element_type=F32).astype(BF)
    by_head = len(mk_ref.shape) == 4
    outs = []
    for h in range(MEM_HEADS):
        sl = slice(h * MEM_HEAD, (h + 1) * MEM_HEAD)
        mk_h = mk_ref[0, :, h, :].astype(BF) if by_head else mk_ref[0, :, sl]
        mv_h = mv_ref[0, :, h, :].astype(BF) if by_head else mv_ref[0, :, sl]
        s = lax.dot_general(q[:, sl], mk_h, _NT, preferred_element_type=F32) * MEM_SCALE
        e = jnp.exp(s - jnp.max(s, axis=-1, keepdims=True))
        pr = e * (1.0 / jnp.sum(e, axis=-1, keepdims=True))
        outs.append(jnp.dot(pr.astype(BF), mv_h, preferred_element_type=F32).astype(BF))
    o = jnp.concatenate(outs, axis=-1)
    o_ref[...] = x + jnp.dot(o, wo_ref[...], preferred_element_type=F32)


def _mem_attn(x2d, B, S, g, wq, mk, mv, wo, l):
    T = B * S
    tm = min(ROW_TILE, S)
    nst = S // tm
    row = lambda i: (i, 0)
    const = lambda i: (0, 0)
    if mk.ndim == 5:
        mem_spec = pl.BlockSpec((None, 1, N_MEM, MEM_HEADS, MEM_HEAD),
                                lambda i: (l, i // nst, 0, 0, 0))
    else:
        mem_spec = pl.BlockSpec((1, N_MEM, D_MODEL), lambda i: (i // nst, 0, 0))
    return pl.pallas_call(
        _mem_kernel,
        grid=(T // tm,),
        in_specs=[
            pl.BlockSpec((tm, D_MODEL), row),
            pl.BlockSpec((1, D_MODEL), const),
            pl.BlockSpec((D_MODEL, D_MODEL), const),
            mem_spec,
            mem_spec,
            pl.BlockSpec((D_MODEL, D_MODEL), const),
        ],
        out_specs=pl.BlockSpec((tm, D_MODEL), row),
        out_shape=jax.ShapeDtypeStruct((T, D_MODEL), F32),
        compiler_params=_cparams("parallel"),
        name="mem_attn",
    )(x2d, g, wq, mk, mv, wo)


def _ffn_kernel(x_ref, g_ref, w1_ref, w2_ref, gf_ref, o_ref, *, final):
    x = x_ref[...]
    xn = _rms(x, g_ref[...]).astype(BF)
    acc = x
    for c in range(D_FF // FF_TILE):
        h = jnp.dot(xn, w1_ref[:, c * FF_TILE:(c + 1) * FF_TILE], preferred_element_type=F32)
        h = jnp.square(jnp.maximum(h, 0.0)).astype(BF)
        acc = acc + jnp.dot(h, w2_ref[c * FF_TILE:(c + 1) * FF_TILE, :], preferred_element_type=F32)
    if final:
        acc = _rms(acc, gf_ref[...])
    o_ref[...] = acc


def _ffn(x2d, g, w1, w2, gf, final):
    T = x2d.shape[0]
    tm = min(FFN_ROW_TILE, T)
    const = lambda i: (0, 0)
    return pl.pallas_call(
        functools.partial(_ffn_kernel, final=final),
        grid=(T // tm,),
        in_specs=[
            pl.BlockSpec((tm, D_MODEL), lambda i: (i, 0)),
            pl.BlockSpec((1, D_MODEL), const),
            pl.BlockSpec((D_MODEL, D_FF), const, pipeline_mode=pl.Buffered(1)),
            pl.BlockSpec((D_FF, D_MODEL), const, pipeline_mode=pl.Buffered(1)),
            pl.BlockSpec((1, D_MODEL), const),
        ],
        out_specs=pl.BlockSpec((tm, D_MODEL), lambda i: (i, 0)),
        out_shape=jax.ShapeDtypeStruct((T, D_MODEL), F32),
        compiler_params=_cparams("parallel"),
        name="ffn",
    )(x2d, g, w1, w2, gf)


def _memproj_kernel(*refs, l):
    m_ref, g_ref, w_ref = refs[:3]
    prev = refs[3:5] if l else ()
    k_ref, v_ref, kb_ref, vb_ref = refs[3 + len(prev):]
    for src, dst in zip(prev, (k_ref, v_ref)):
        dst[0:l] = src[...]
    mn = _rms(m_ref[...], g_ref[...]).astype(BF)
    kv = jnp.dot(mn, w_ref[...], preferred_element_type=F32)
    k = kv[:, :D_MODEL]
    v = kv[:, D_MODEL:]
    for h in range(MEM_HEADS):
        k_ref[l, :, h, :] = k[:, h * MEM_HEAD:(h + 1) * MEM_HEAD]
        v_ref[l, :, h, :] = v[:, h * MEM_HEAD:(h + 1) * MEM_HEAD]
    kb_ref[...] = k.astype(BF)
    vb_ref[...] = v.astype(BF)


def _mem_project(mem2d, g, wkv, l, prev):
    T = mem2d.shape[0]
    tm = min(MEMPROJ_TILE, T)
    row = lambda i: (i, 0)
    const = lambda i: (0, 0)
    prev = () if prev is None else prev
    stacked = lambda n: pl.BlockSpec((n, tm, MEM_HEADS, MEM_HEAD), lambda i: (0, i, 0, 0))
    return pl.pallas_call(
        functools.partial(_memproj_kernel, l=l),
        grid=(T // tm,),
        in_specs=[
            pl.BlockSpec((tm, D_MODEL), row),
            pl.BlockSpec((1, D_MODEL), const),
            pl.BlockSpec((D_MODEL, 2 * D_MODEL), const),
        ] + [stacked(l) for _ in prev],
        out_specs=[stacked(l + 1), stacked(l + 1),
                   pl.BlockSpec((tm, D_MODEL), row), pl.BlockSpec((tm, D_MODEL), row)],
        out_shape=[jax.ShapeDtypeStruct((l + 1, T, MEM_HEADS, MEM_HEAD), F32)] * 2
                  + [jax.ShapeDtypeStruct((T, D_MODEL), BF)] * 2,
        compiler_params=_cparams("parallel"),
        name="mem_project",
    )(mem2d, g, wkv, *prev)


def _fold_kernel(a_ref, b_ref, o_ref):
    o_ref[0] = _mm3(a_ref[0], b_ref[0])


def _fold_heads(a, b):
    G, m, k = a.shape
    n = b.shape[2]
    return pl.pallas_call(
        _fold_kernel,
        grid=(G,),
        in_specs=[pl.BlockSpec((1, m, k), lambda g: (g, 0, 0)),
                  pl.BlockSpec((1, k, n), lambda g: (g, 0, 0))],
        out_specs=pl.BlockSpec((1, m, n), lambda g: (g, 0, 0)),
        out_shape=jax.ShapeDtypeStruct((G, m, n), F32),
        compiler_params=_cparams("parallel"),
        name="fold_heads",
    )(a, b)


def _block_diag(blocks):
    G, m, n = blocks.shape
    eye = jnp.eye(G, dtype=blocks.dtype)
    return jnp.einsum('gmn,gh->gmhn', blocks, eye).reshape(G * m, G * n)


def _prep_weights(l, W):
    r2 = lambda v: v.reshape(1, -1)
    w_in = W['w_in'][l]
    kr_cols = w_in[:, 1792:1824]
    kr_swap = jnp.concatenate([kr_cols[:, 16:], kr_cols[:, :16]], axis=1)
    w_ext = jnp.concatenate([w_in[:, :1792], jnp.tile(kr_cols, (1, 4)), jnp.tile(kr_swap, (1, 4))],
                            axis=1).astype(BF)
    w_uq = W['mla_w_uq'][l]
    rope = w_uq[:, :, QK_NOPE:]
    rope_swap = jnp.concatenate([rope[:, :, 16:], rope[:, :, :16]], axis=2)
    w_fold = _fold_heads(jnp.transpose(w_uq[:, :, :QK_NOPE], (1, 0, 2)),
                         jnp.transpose(W['mla_w_uk'][l], (1, 2, 0)))
    wq_ext = jnp.concatenate([jnp.transpose(w_fold, (1, 0, 2)).reshape(Q_LORA, -1),
                              rope.reshape(Q_LORA, -1), rope_swap.reshape(Q_LORA, -1)],
                             axis=1).astype(BF)
    w_uv = W['mla_w_uv'][l]
    wuv_t = jnp.transpose(w_uv, (1, 2, 0)).astype(BF)
    wuv_h = jnp.transpose(w_uv, (1, 0, 2)).astype(BF)
    zpad = lambda top, mat, bot: jnp.concatenate(
        [jnp.zeros((top, RW_WIDTH), F32), mat, jnp.zeros((bot, RW_WIDTH), F32)], axis=0)
    return dict(
        norm_mix=r2(W['norm_mix'][l]), w_ext=w_ext, gq=r2(W['mla_q_norm'][l]),
        gkv=r2(W['mla_kv_norm'][l]), wq_ext=wq_ext, wuv_t=wuv_t, wuv_h=wuv_h,
        cw=W['rg_conv_w'][l], cb=r2(W['rg_conv_b'][l]),
        wa_bd=_block_diag(W['rg_wa'][l]).astype(BF), ba=r2(W['rg_ba'][l]),
        wx_bd=_block_diag(W['rg_wx'][l]).astype(BF), bx=r2(W['rg_bx'][l]),
        lam=r2(W['rg_lambda'][l]),
        mu=r2(W['rw_mu'][l]), w0=r2(W['rw_w0'][l]), w2p=zpad(0, W['rw_w2'][l], 96),
        a0=r2(W['rw_a0'][l]), a2p=zpad(32, W['rw_a2'][l], 64), g2p=zpad(64, W['rw_g2'][l], 0),
        k_k=r2(W['rw_k_k'][l]), k_a=r2(W['rw_k_a'][l]), r_k=r2(W['rw_r_k'][l]),
        gn_w=r2(W['rw_gn_w'][l]), gn_b=r2(W['rw_gn_b'][l]),
        w_out=W['w_out'][l].astype(BF),
        norm_mem=r2(W['norm_mem'][l]), wq=W['mem_w_q'][l].reshape(D_MODEL, D_MODEL).astype(BF),
        wo=W['mem_w_o'][l].reshape(D_MODEL, D_MODEL).astype(BF),
        norm_ffn=r2(W['norm_ffn'][l]), w1=W['ffn_w1'][l].astype(BF), w2=W['ffn_w2'][l].astype(BF),
        norm_mem_kv=r2(W['norm_mem_kv'][l]),
        wkv=jnp.concatenate([W['mem_w_k'][l].reshape(D_MODEL, D_MODEL),
                             W['mem_w_v'][l].reshape(D_MODEL, D_MODEL)], axis=1).astype(BF),
    )


def _rope_tables(pos, rows):
    half = QK_ROPE // 2
    inv = ROPE_BASE ** (-jnp.arange(half, dtype=F32) / half)
    ang = pos.astype(F32)[:, None] * inv
    cos, sin = jnp.cos(ang), jnp.sin(ang)
    cosq = jnp.tile(jnp.concatenate([cos, cos], axis=1), (1, MLA_HEADS))
    sinq = jnp.tile(jnp.concatenate([-sin, sin], axis=1), (1, MLA_HEADS))
    reps = max(1, rows // pos.shape[0])
    return jnp.tile(cosq, (reps, 1)), jnp.tile(sinq, (reps, 1))


def _layer(x2d, B, S, P, tabs, rg_buf, rg_h, rw_shift, rw_state, past, mem_k, mem_v, gf, l, depth,
           prev_kv):
    cosq, sinq = tabs
    final = l == depth - 1
    pa, prw, ckv, kr, kcat, qcat = _in_proj(x2d, B, S, P['norm_mix'], P['w_ext'], P['gq'], P['gkv'],
                                            P['wq_ext'], cosq, sinq, l, prev_kv)
    buf8 = jnp.concatenate([jnp.zeros((B, 5, RG_WIDTH), F32), rg_buf], axis=1)
    yrg, buf_new, h_new = _rglru(pa.reshape(B, S, 512), buf8, rg_h.reshape(B, 1, RG_WIDTH),
                                 P['cw'], P['cb'], P['wa_bd'], P['ba'], P['wx_bd'], P['bx'], P['lam'])
    eye_h = jnp.eye(RW_HEADS, dtype=F32)
    state_bd = jnp.einsum('bhij,hg->bhigj', rw_state, eye_h).reshape(B, RW_WIDTH, RW_WIDTH)
    ones_bd = _block_diag(jnp.ones((RW_HEADS, RW_HEAD, RW_HEAD), F32)).astype(BF)
    yrw, shift_new, state_new = _rwkv(prw.reshape(B, S, RW_IN), rw_shift.reshape(B, 1, RW_IN), state_bd,
                                      P['mu'], P['w0'], P['w2p'], P['a0'], P['a2p'], P['g2p'],
                                      P['k_k'], P['k_a'], P['r_k'], P['gn_w'], P['gn_b'], ones_bd)
    wkv_new = jnp.einsum('bhigj,hg->bhij',
                         state_new.reshape(B, RW_HEADS, RW_HEAD, RW_HEADS, RW_HEAD), eye_h)
    kcat3 = kcat.reshape(B, S, 256)
    if past is None:
        ymla = _attn_causal(qcat, kcat3, P['wuv_t'])
    else:
        ymla = _attn_full(qcat, past[0], past[1], kcat3, P['wuv_h'], l)
    x1 = _out_proj(x2d, yrg.reshape(B * S, RG_WIDTH), yrw.reshape(B * S, RW_WIDTH),
                   ymla.reshape(B * S, MLA_HEADS * V_HEAD), P['w_out'])
    x2 = _mem_attn(x1, B, S, P['norm_mem'], P['wq'], mem_k, mem_v, P['wo'], l)
    x3 = _ffn(x2, P['norm_ffn'], P['w1'], P['w2'], gf, final)
    new_state = (buf_new[:, 5:8], h_new.reshape(B, RG_WIDTH), shift_new.reshape(B, RW_IN), wkv_new)
    return x3, new_state, (ckv, kr)


def kernel(x_prompt, x_sample, state_rg_conv, state_rg_h, state_rw_shift, state_rw_wkv, cache_mla_ckv, cache_mla_krope, cache_mem_k, cache_mem_v, mem_prompt, norm_mix, w_in, rg_conv_w, rg_conv_b, rg_wa, rg_ba, rg_wx, rg_bx, rg_lambda, rw_mu, rw_w0, rw_w2, rw_a0, rw_a2, rw_g2, rw_k_k, rw_k_a, rw_r_k, rw_gn_w, rw_gn_b, mla_q_norm, mla_kv_norm, mla_w_uq, mla_w_uk, mla_w_uv, w_out, norm_mem, norm_mem_kv, mem_w_q, mem_w_k, mem_w_v, mem_w_o, norm_ffn, ffn_w1, ffn_w2, norm_final):
    W = dict(norm_mix=norm_mix, w_in=w_in, rg_conv_w=rg_conv_w, rg_conv_b=rg_conv_b, rg_wa=rg_wa,
             rg_ba=rg_ba, rg_wx=rg_wx, rg_bx=rg_bx, rg_lambda=rg_lambda, rw_mu=rw_mu, rw_w0=rw_w0,
             rw_w2=rw_w2, rw_a0=rw_a0, rw_a2=rw_a2, rw_g2=rw_g2, rw_k_k=rw_k_k, rw_k_a=rw_k_a,
             rw_r_k=rw_r_k.reshape(rw_r_k.shape[0], RW_WIDTH), rw_gn_w=rw_gn_w, rw_gn_b=rw_gn_b,
             mla_q_norm=mla_q_norm, mla_kv_norm=mla_kv_norm, mla_w_uq=mla_w_uq, mla_w_uk=mla_w_uk,
             mla_w_uv=mla_w_uv, w_out=w_out, norm_mem=norm_mem, norm_mem_kv=norm_mem_kv,
             mem_w_q=mem_w_q, mem_w_k=mem_w_k, mem_w_v=mem_w_v, mem_w_o=mem_w_o, norm_ffn=norm_ffn,
             ffn_w1=ffn_w1, ffn_w2=ffn_w2)
    depth = norm_mix.shape[0]
    Bp, Sp, _ = x_prompt.shape
    Bs, Ss, _ = x_sample.shape
    past_len = cache_mla_ckv.shape[2]
    tabs_p = _rope_tables(jnp.arange(Sp, dtype=jnp.int32), _in_tile(Bp, Sp))
    tabs_s = _rope_tables(past_len + jnp.arange(Ss, dtype=jnp.int32), _in_tile(Bs, Ss))
    gf = norm_final.reshape(1, D_MODEL)

    xp = x_prompt.reshape(Bp * Sp, D_MODEL)
    xs = x_sample.reshape(Bs * Ss, D_MODEL)
    z = lambda *s: jnp.zeros(s, F32)
    p_states, s_states = [], []
    mem_kv = kv_p = kv_s = None
    for l in range(depth):
        P = _prep_weights(l, W)
        mk, mv, mkb, mvb = _mem_project(mem_prompt.reshape(Bp * N_MEM, D_MODEL), P['norm_mem_kv'],
                                        P['wkv'], l, mem_kv)
        mem_kv = (mk, mv)
        xp, st_p, kv_p = _layer(xp, Bp, Sp, P, tabs_p, z(Bp, 3, RG_WIDTH), z(Bp, RG_WIDTH),
                                z(Bp, RW_IN), z(Bp, RW_HEADS, RW_HEAD, RW_HEAD), None,
                                mkb.reshape(Bp, N_MEM, D_MODEL), mvb.reshape(Bp, N_MEM, D_MODEL),
                                gf, l, depth, kv_p)
        xs, st_s, kv_s = _layer(xs, Bs, Ss, P, tabs_s, state_rg_conv[l], state_rg_h[l],
                                state_rw_shift[l], state_rw_wkv[l],
                                (cache_mla_ckv, cache_mla_krope),
                                cache_mem_k, cache_mem_v, gf, l, depth, kv_s)
        p_states.append(st_p)
        s_states.append(st_s)

    sp = [jnp.stack(t) for t in zip(*p_states)]
    ss = [jnp.stack(t) for t in zip(*s_states)]
    mem_shape = (depth, Bp, N_MEM, MEM_HEADS, MEM_HEAD)
    return (xp.reshape(Bp, Sp, D_MODEL), xs.reshape(Bs, Ss, D_MODEL),
            sp[0], sp[1], sp[2], sp[3],
            kv_p[0].reshape(depth, Bp, Sp, KV_LORA), kv_p[1].reshape(depth, Bp, Sp, QK_ROPE),
            mem_kv[0].reshape(mem_shape), mem_kv[1].reshape(mem_shape),
            ss[0], ss[1], ss[2], ss[3],
            kv_s[0].reshape(depth, Bs, Ss, KV_LORA), kv_s[1].reshape(depth, Bs, Ss, QK_ROPE))
```
